```python
import math
import numpy as np
import jax
import jax.numpy as jnp
from jax import lax

D_MODEL = 1024
BATCH = 16
SEQ = 256
DEPTH = 2
DEC_BATCH = 2
DEC_SEQ = 1024
PAST_LEN = 256

F32 = jnp.float32
GRID_W = 64
ROPE_BASE = 10000.0
EPS = 1e-6
CHUNK = 64
Q_BLOCK = 128
TOK_BLOCK = 128
R_HEADS = 4
R_DK = 128
R_DV = 256
D_HEADS = 8
D_HD = 64
G_HEADS = 8
G_DK = 128
G_DV = 128
CONV_K = 7
G_QKV = G_HEADS * (2 * G_DK + G_DV)
N_BRANCH = 3
BRANCH_W = R_HEADS * R_DV
PEER_HEADS = 8
PEER_DQ = 256
N_KEYS = 128
N_EXPERTS = N_KEYS * N_KEYS
PEER_TOPK = 16
IN_SPLITS = (R_HEADS * R_DK, R_HEADS * R_DK, R_HEADS * R_DV, R_HEADS * R_DV,
             D_HEADS * 2 * D_HD, D_HEADS * 2 * D_HD, D_HEADS * 2 * D_HD,
             G_QKV, G_HEADS * G_DV, 2 * G_HEADS, 2 * G_HEADS, N_BRANCH * D_MODEL)
IN_COLS = sum(IN_SPLITS)

kernel_name = 'hybrid_ret_diffattn_gdn_peer_diffusion_step'


def rmsnorm(x, w):
    xf = x.astype(F32)
    y = xf * lax.rsqrt(jnp.mean(xf * xf, axis=-1, keepdims=True) + EPS)
    return (y * w.astype(F32)).astype(x.dtype)


def l2norm(x):
    xf = x.astype(F32)
    return (xf * lax.rsqrt(jnp.sum(xf * xf, axis=-1, keepdims=True) + EPS)).astype(x.dtype)


def flip(x):
    return jnp.flip(x, axis=1)


def axial_rope(x):
    L, dim = x.shape[1], x.shape[-1]
    n_rows = L // GRID_W
    row = jnp.repeat(jnp.arange(n_rows), GRID_W).astype(F32)
    col = jnp.tile(jnp.arange(GRID_W), n_rows).astype(F32)
    nf = dim // 4
    inv = jnp.power(ROPE_BASE, -jnp.arange(nf, dtype=F32) / nf)
    ang_r = row[:, None] * inv
    ang_c = col[:, None] * inv
    ang = jnp.concatenate([ang_r, ang_r, ang_c, ang_c], axis=-1)
    shape = (1, L) + (1,) * (x.ndim - 3) + (dim,)
    cos = jnp.cos(ang).reshape(shape)
    sin = jnp.sin(ang).reshape(shape)
    xf = x.astype(F32)
    x_r1, x_r2, x_c1, x_c2 = jnp.split(xf, 4, axis=-1)
    rot = jnp.concatenate([-x_r2, x_r1, -x_c2, x_c1], axis=-1)
    return (xf * cos + rot * sin).astype(x.dtype)


def to_chunks(x):
    B, L, H = x.shape[:3]
    x = x.reshape((B, L // CHUNK, CHUNK, H) + x.shape[3:])
    return jnp.moveaxis(x, 3, 1)


def from_chunks(x):
    x = jnp.moveaxis(x, 1, 3)
    B, N, C, H = x.shape[:4]
    return x.reshape((B, N * C, H) + x.shape[4:])


def retention_scan(q, k, v, log_gamma, s0):
    dtype = v.dtype
    q = to_chunks(q.astype(F32) * R_DK ** -0.5)
    k = to_chunks(k.astype(F32))
    v = to_chunks(v.astype(F32))
    pos = jnp.arange(CHUNK, dtype=F32)
    lg = log_gamma.astype(F32)[:, None]
    dist = pos[:, None] - pos[None, :]
    intra = jnp.exp(jnp.where(dist >= 0, lg[:, :, None] * dist, -jnp.inf))
    q_dec = jnp.exp(lg * (pos + 1.0))[:, :, None]
    k_dec = jnp.exp(lg * (CHUNK - 1.0 - pos))[:, :, None]
    c_dec = jnp.exp(lg * CHUNK)[:, :, None]

    def step(s, xs):
        qc, kc, vc = xs
        a = jnp.einsum('bhqd,bhkd->bhqk', qc, kc) * intra
        o = jnp.einsum('bhqk,bhkv->bhqv', a, vc) + jnp.einsum('bhqd,bhdv->bhqv', qc, s) * q_dec
        s = s * c_dec + jnp.einsum('bhkd,bhkv->bhdv', kc * k_dec, vc)
        return s, o

    xs = (jnp.moveaxis(q, 2, 0), jnp.moveaxis(k, 2, 0), jnp.moveaxis(v, 2, 0))
    s, o = lax.scan(step, s0.astype(F32), xs)
    return from_chunks(jnp.moveaxis(o, 0, 2)).astype(dtype), s


def gdn_scan(q, k, v, g, beta, s0):
    dtype = v.dtype
    q = to_chunks(q.astype(F32) * G_DK ** -0.5)
    k = to_chunks(k.astype(F32))
    v = to_chunks(v.astype(F32))
    g = jnp.cumsum(to_chunks(g.astype(F32)), axis=-1)
    beta = to_chunks(beta.astype(F32))
    incl = jnp.tril(jnp.ones((CHUNK, CHUNK), dtype=bool))
    strict = jnp.tril(jnp.ones((CHUNK, CHUNK), dtype=bool), -1)
    decay = jnp.exp(jnp.where(incl, g[..., :, None] - g[..., None, :], -jnp.inf))
    kb = k * beta[..., None]
    a = jnp.where(strict, jnp.einsum('bhnid,bhnjd->bhnij', kb, k) * decay, 0.0)
    eye = jnp.eye(CHUNK, dtype=F32)
    t = lax.linalg.triangular_solve(a + eye, jnp.broadcast_to(eye, a.shape), left_side=True, lower=True)
    u = t @ (v * beta[..., None])
    w = t @ (kb * jnp.exp(g)[..., None])
    qk = jnp.einsum('bhnid,bhnjd->bhnij', q, k) * decay
    qg = q * jnp.exp(g)[..., None]
    kd = k * jnp.exp(g[..., -1:] - g)[..., None]
    gl = jnp.exp(g[..., -1])[..., None, None]

    def step(s, xs):
        qg_c, kd_c, u_c, w_c, qk_c, gl_c = xs
        u_new = u_c - w_c @ s
        o = qg_c @ s + qk_c @ u_new
        s = s * gl_c + jnp.swapaxes(kd_c, -1, -2) @ u_new
        return s, o

    xs = tuple(jnp.moveaxis(arr, 2, 0) for arr in (qg, kd, u, w, qk, gl))
    s, o = lax.scan(step, s0.astype(F32), xs)
    return from_chunks(jnp.moveaxis(o, 0, 2)).astype(dtype), s


def diff_attention(q, k, v, lam):
    B, Lq = q.shape[:2]
    nb = Lq // Q_BLOCK
    qb = jnp.moveaxis(q.reshape((B, nb, Q_BLOCK) + q.shape[2:]), 1, 0)
    scale = D_HD ** -0.5

    def attend(qblk):
        s = jnp.einsum('bqhpd,bkhpd->bhpqk', qblk, k).astype(F32) * scale
        p = jax.nn.softmax(s, axis=-1)
        wts = p[:, :, 0] - lam * p[:, :, 1]
        return jnp.einsum('bhqk,bkhe->bqhe', wts.astype(v.dtype), v)

    o = lax.map(attend, qb)
    return jnp.moveaxis(o, 0, 1).reshape(B, Lq, D_HEADS, 2 * D_HD)


def centred_dwconv(x, w):
    pad = (w.shape[0] - 1) // 2
    return lax.conv_general_dilated(x, w[:, None, :].astype(x.dtype), window_strides=(1,),
                                    padding=[(pad, pad)], dimension_numbers=('NWC', 'WIO', 'NWC'),
                                    feature_group_count=x.shape[-1])


def peer_ffn(h, w_q, sub_keys, exp_u, exp_v):
    B, L, D = h.shape
    hb = h.reshape(B * L // TOK_BLOCK, TOK_BLOCK, D)

    def block(hx):
        q = (hx @ w_q).reshape(TOK_BLOCK, PEER_HEADS, 2, PEER_DQ // 2)
        s = jnp.einsum('thpd,pnd->thpn', q, sub_keys).astype(F32)
        s_top, i_top = lax.top_k(s, PEER_TOPK)
        cand_s = (s_top[:, :, 0, :, None] + s_top[:, :, 1, None, :]).reshape(TOK_BLOCK, PEER_HEADS, PEER_TOPK * PEER_TOPK)
        cand_i = (i_top[:, :, 0, :, None] * N_KEYS + i_top[:, :, 1, None, :]).reshape(TOK_BLOCK, PEER_HEADS, PEER_TOPK * PEER_TOPK)
        best_s, pos = lax.top_k(cand_s, PEER_TOPK)
        idx = jnp.take_along_axis(cand_i, pos, axis=-1)
        gate = jax.nn.softmax(best_s, axis=-1)
        act = jax.nn.gelu(jnp.einsum('thkd,td->thk', exp_u[idx], hx).astype(F32), approximate=False)
        return jnp.einsum('thk,thkd->td', (gate * act).astype(hx.dtype), exp_v[idx])

    return lax.map(block, hb).reshape(B, L, D)


def token_mixing(h, lw, li, ctx):
    B, L, _ = h.shape
    latent = ctx is not None
    proj = h @ lw['w_in']
    cuts = np.cumsum(IN_SPLITS)[:-1].tolist()
    (r_q, r_k, r_v, r_g, d_q, d_k, d_v, g_qkv, g_z, g_a, g_b, m_g) = jnp.split(proj, cuts, axis=-1)

    r_q = r_q.reshape(B, L, R_HEADS, R_DK)
    r_k = r_k.reshape(B, L, R_HEADS, R_DK)
    r_v = r_v.reshape(B, L, R_HEADS, R_DV)
    if latent:
        r_q, r_k = axial_rope(r_q), axial_rope(r_k)
        r0_f, r0_b = ctx['ret'][:, 0], ctx['ret'][:, 1]
    else:
        r0_f = r0_b = jnp.zeros((B, R_HEADS, R_DK, R_DV), F32)
    log_gamma = jax.nn.log_sigmoid(lw['ret_decay'].astype(F32))
    or_f, sr_f = retention_scan(r_q, r_k, r_v, log_gamma[0], r0_f)
    or_b, sr_b = retention_scan(flip(r_q), flip(r_k), flip(r_v), log_gamma[1], r0_b)
    o_ret = rmsnorm(or_f + flip(or_b), lw['ret_norm']).reshape(B, L, BRANCH_W) * jax.nn.silu(r_g)

    d_q = d_q.reshape(B, L, D_HEADS, 2, D_HD)
    d_k = d_k.reshape(B, L, D_HEADS, 2, D_HD)
    d_v = d_v.reshape(B, L, D_HEADS, 2 * D_HD)
    if latent:
        q_att = axial_rope(d_q)
        k_all = jnp.concatenate([axial_rope(d_k), ctx['k']], axis=1)
        v_all = jnp.concatenate([d_v, ctx['v']], axis=1)
    else:
        q_att, k_all, v_all = d_q, d_k, d_v
    lam_init = 0.8 - 0.6 * math.exp(-0.3 * li)
    lp = lw['diff_lambda'].astype(F32)
    lam = jnp.exp(jnp.sum(lp[0] * lp[1])) - jnp.exp(jnp.sum(lp[2] * lp[3])) + lam_init
    o_diff = diff_attention(q_att, k_all, v_all, lam)
    o_diff = (rmsnorm(o_diff, lw['diff_norm']) * (1.0 - lam_init)).reshape(B, L, BRANCH_W)

    g_qkv = jax.nn.silu(centred_dwconv(g_qkv, lw['gdn_conv']))
    g_q, g_k, g_v = jnp.split(g_qkv, [G_HEADS * G_DK, 2 * G_HEADS * G_DK], axis=-1)
    g_q = l2norm(g_q.reshape(B, L, G_HEADS, G_DK))
    g_k = l2norm(g_k.reshape(B, L, G_HEADS, G_DK))
    g_v = g_v.reshape(B, L, G_HEADS, G_DV)
    g_a = g_a.reshape(B, L, 2, G_HEADS).astype(F32)
    g_b = g_b.reshape(B, L, 2, G_HEADS).astype(F32)
    g_log = -jnp.exp(lw['gdn_A_log'].astype(F32)) * jax.nn.softplus(g_a + lw['gdn_dt_bias'].astype(F32))
    g_beta = jax.nn.sigmoid(g_b)
    if latent:
        g0_f, g0_b = ctx['gdn'][:, 0], ctx['gdn'][:, 1]
    else:
        g0_f = g0_b = jnp.zeros((B, G_HEADS, G_DK, G_DV), F32)
    og_f, sg_f = gdn_scan(g_q, g_k, g_v, g_log[:, :, 0], g_beta[:, :, 0], g0_f)
    og_b, sg_b = gdn_scan(flip(g_q), flip(g_k), flip(g_v), flip(g_log[:, :, 1]), flip(g_beta[:, :, 1]), g0_b)
    o_gdn = rmsnorm(og_f + flip(og_b), lw['gdn_norm']) * jax.nn.silu(g_z.reshape(B, L, G_HEADS, G_DV))
    o_gdn = o_gdn.reshape(B, L, BRANCH_W)

    branches = jnp.stack([o_ret, o_diff, o_gdn], axis=2)
    merged = jnp.einsum('blnw,nwd->blnd', branches, lw['w_branch'])
    gates = jax.nn.sigmoid(m_g.reshape(B, L, N_BRANCH, D_MODEL))
    m = jnp.sum(gates * merged, axis=2) @ lw['w_out']
    if latent:
        return m, None
    state = (d_k, d_v,
             jnp.stack([sr_f, sr_b], axis=1).astype(h.dtype),
             jnp.stack([sg_f, sg_b], axis=1).astype(h.dtype))
    return m, state


def trunk_layer(x, cond, lw, li, ctx):
    mod = jax.nn.silu(cond) @ lw['w_mod'] + lw['b_mod']
    sh1, sc1, g1, sh2, sc2, g2 = jnp.split(mod[:, None, :], 6, axis=-1)
    h = rmsnorm(x, lw['norm1']) * (1 + sc1) + sh1
    m, state = token_mixing(h, lw, li, ctx)
    x = x + g1 * m
    h = rmsnorm(x, lw['norm2']) * (1 + sc2) + sh2
    x = x + g2 * peer_ffn(h, lw['peer_wq'], lw['peer_keys'], lw['peer_u'], lw['peer_v'])
    return x, state


def setup_inputs(seed: int = 0) -> dict:
    key = jax.random.key(seed)
    ks = jax.random.split(key, 32)
    D = D_MODEL

    def nrm(k, shape, s=1.0):
        return jax.random.normal(k, shape, F32) * s

    base = 1.0 - 2.0 ** (-5.0 - np.arange(R_HEADS))
    ret_logit = jnp.asarray(np.log(base) - np.log1p(-base), F32)
    dt = jnp.exp(jax.random.uniform(ks[20], (DEPTH, 2, G_HEADS), F32, math.log(1e-3), math.log(1e-1)))
    return {
        'x_prompt': nrm(ks[0], (BATCH, SEQ, D)),
        'x_sample': nrm(ks[1], (DEC_BATCH, DEC_SEQ, D)),
        'cache_diff_k': nrm(ks[2], (DEC_BATCH, DEPTH, PAST_LEN, D_HEADS, 2, D_HD)),
        'cache_diff_v': nrm(ks[3], (DEC_BATCH, DEPTH, PAST_LEN, D_HEADS, 2 * D_HD)),
        'state_ret': nrm(ks[4], (DEC_BATCH, DEPTH, 2, R_HEADS, R_DK, R_DV), 4.0),
        'state_gdn': nrm(ks[5], (DEC_BATCH, DEPTH, 2, G_HEADS, G_DK, G_DV), 0.5),
        'c': nrm(ks[6], (DEC_BATCH, D)),
        'c_ctx': nrm(ks[7], (D,)),
        'norm1_w': 1.0 + nrm(ks[8], (DEPTH, D), 0.05),
        'norm2_w': 1.0 + nrm(ks[9], (DEPTH, D), 0.05),
        'w_mod': nrm(ks[10], (DEPTH, D, 6 * D), 0.5 * D ** -0.5),
        'b_mod': nrm(ks[11], (DEPTH, 6 * D), 0.01),
        'w_in': nrm(ks[12], (DEPTH, D, IN_COLS), D ** -0.5),
        'ret_decay': ret_logit[None, None, :] + nrm(ks[13], (DEPTH, 2, R_HEADS), 0.1),
        'ret_norm': 1.0 + nrm(ks[14], (DEPTH, R_DV), 0.05),
        'diff_lambda': nrm(ks[15], (DEPTH, 4, D_HD), 0.1),
        'diff_norm': 1.0 + nrm(ks[16], (DEPTH, 2 * D_HD), 0.05),
        'gdn_conv': nrm(ks[17], (DEPTH, CONV_K, G_QKV), CONV_K ** -0.5),
        'gdn_A_log': jnp.log(jax.random.uniform(ks[18], (DEPTH, 2, G_HEADS), F32, 1.0, 16.0)),
        'gdn_dt_bias': dt + jnp.log(-jnp.expm1(-dt)),
        'gdn_norm': 1.0 + nrm(ks[19], (DEPTH, G_DV), 0.05),
        'w_branch': nrm(ks[21], (DEPTH, N_BRANCH, BRANCH_W, D), BRANCH_W ** -0.5),
        'w_out': nrm(ks[22], (DEPTH, D, D), D ** -0.5),
        'peer_wq': nrm(ks[23], (DEPTH, D, PEER_HEADS * PEER_DQ), D ** -0.5),
        'peer_keys': nrm(ks[24], (DEPTH, 2, N_KEYS, PEER_DQ // 2), (PEER_DQ // 2) ** -0.5),
        'peer_u': nrm(ks[25], (DEPTH, N_EXPERTS, D), D ** -0.5),
        'peer_v': nrm(ks[26], (DEPTH, N_EXPERTS, D), 1.0),
        'norm_f_w': 1.0 + nrm(ks[27], (D,), 0.05),
    }


def reference(x_prompt, x_sample, cache_diff_k, cache_diff_v, state_ret, state_gdn, c, c_ctx,
              norm1_w, norm2_w, w_mod, b_mod, w_in, ret_decay, ret_norm, diff_lambda, diff_norm,
              gdn_conv, gdn_A_log, gdn_dt_bias, gdn_norm, w_branch, w_out,
              peer_wq, peer_keys, peer_u, peer_v, norm_f_w):
    layers = [dict(norm1=norm1_w[i], norm2=norm2_w[i], w_mod=w_mod[i], b_mod=b_mod[i], w_in=w_in[i],
                   ret_decay=ret_decay[i], ret_norm=ret_norm[i], diff_lambda=diff_lambda[i],
                   diff_norm=diff_norm[i], gdn_conv=gdn_conv[i], gdn_A_log=gdn_A_log[i],
                   gdn_dt_bias=gdn_dt_bias[i], gdn_norm=gdn_norm[i], w_branch=w_branch[i], w_out=w_out[i],
                   peer_wq=peer_wq[i], peer_keys=peer_keys[i], peer_u=peer_u[i], peer_v=peer_v[i])
              for i in range(DEPTH)]

    y = x_prompt
    ctx_cond = c_ctx[None, :]
    ks, vs, srs, sgs = [], [], [], []
    for li in range(DEPTH):
        y, (k_l, v_l, sr_l, sg_l) = trunk_layer(y, ctx_cond, layers[li], li, None)
        ks.append(k_l)
        vs.append(v_l)
        srs.append(sr_l)
        sgs.append(sg_l)
    y_prompt = rmsnorm(y, norm_f_w)

    z = x_sample
    for li in range(DEPTH):
        ctx = {'k': cache_diff_k[:, li], 'v': cache_diff_v[:, li], 'ret': state_ret[:, li], 'gdn': state_gdn[:, li]}
        z, _ = trunk_layer(z, c, layers[li], li, ctx)
    y_sample = rmsnorm(z, norm_f_w)

    new_cache_diff_k = jnp.stack(ks, axis=1)
    new_cache_diff_v = jnp.stack(vs, axis=1)
    new_state_ret = jnp.stack(srs, axis=1)
    new_state_gdn = jnp.stack(sgs, axis=1)
    return (y_prompt, y_sample, new_cache_diff_k, new_cache_diff_v, new_state_ret, new_state_gdn)
```

```python
import functools
import math

import numpy as np
import jax
import jax.numpy as jnp
from jax import lax
from jax.experimental import pallas as pl
from jax.experimental.pallas import tpu as pltpu

F32 = jnp.float32
BF16 = jnp.bfloat16

D_MODEL = 1024
BATCH = 16
SEQ = 256
DEPTH = 2
DEC_BATCH = 2
DEC_SEQ = 1024
PAST_LEN = 256
GRID_W = 64
ROPE_BASE = 10000.0
EPS = 1e-6
CHUNK = 64
R_HEADS, R_DK, R_DV = 4, 128, 256
D_HEADS, D_HD = 8, 64
G_HEADS, G_DK, G_DV = 8, 128, 128
CONV_K = 7
G_QKV = G_HEADS * (2 * G_DK + G_DV)
N_BRANCH = 3
BRANCH_W = 1024
PEER_HEADS = 8
PEER_DQ = 256
N_KEYS = 128
PEER_TOPK = 16
TOK_BLOCK = 128

N_CTX = BATCH * SEQ
N_LAT = DEC_BATCH * DEC_SEQ
N_TOK = N_CTX + N_LAT
N_COND = 1 + DEC_BATCH
MOD_BLK = 256
N_MOD_BLK = N_TOK // MOD_BLK

C_RQ, C_RK, C_RV, C_RG = 0, 512, 1024, 2048
C_DQ, C_DK, C_DV = 3072, 4096, 5120
C_GQKV, C_GZ, C_MG = 6144, 9216, 10240
N_MAIN = 13312
AB_OFF = 10240

VMEM_LIMIT = 56 * 1024 * 1024


def _cparams(sem):
    return pltpu.CompilerParams(dimension_semantics=sem, vmem_limit_bytes=VMEM_LIMIT)


def _mod_kernel(c_ref, w_ref, b_ref, o_ref):
    c = c_ref[...]
    a = c * jax.nn.sigmoid(c)
    o_ref[0] = jnp.dot(a, w_ref[0], preferred_element_type=F32,
                       precision=lax.Precision.HIGHEST) + b_ref[0]


def _modulation(cond_pad, w_mod, b_mod):
    tn = 1536
    return pl.pallas_call(
        _mod_kernel,
        out_shape=jax.ShapeDtypeStruct((DEPTH, 8, 6 * D_MODEL), F32),
        grid=(DEPTH, 6 * D_MODEL // tn),
        in_specs=[pl.BlockSpec((8, D_MODEL), lambda l, j: (0, 0)),
                  pl.BlockSpec((1, D_MODEL, tn), lambda l, j: (l, 0, j)),
                  pl.BlockSpec((1, 1, tn), lambda l, j: (l, 0, j))],
        out_specs=pl.BlockSpec((1, 8, tn), lambda l, j: (l, 0, j)),
        compiler_params=_cparams(("arbitrary", "arbitrary")),
        name="modulation",
    )(cond_pad, w_mod, b_mod.reshape(DEPTH, 1, 6 * D_MODEL))


def _normmod_kernel(x_ref, w_ref, sc_ref, sh_ref, o_ref):
    x = x_ref[...]
    y = x * lax.rsqrt(jnp.mean(x * x, axis=-1, keepdims=True) + EPS)
    y = y * w_ref[...]
    o_ref[...] = (y * (1.0 + sc_ref[0]) + sh_ref[0]).astype(o_ref.dtype)


def _normmod(x, w, sc_blk, sh_blk, out_dtype):
    tm = MOD_BLK
    return pl.pallas_call(
        _normmod_kernel,
        out_shape=jax.ShapeDtypeStruct((N_TOK, D_MODEL), out_dtype),
        grid=(N_TOK // tm,),
        in_specs=[pl.BlockSpec((tm, D_MODEL), lambda i: (i, 0)),
                  pl.BlockSpec((1, D_MODEL), lambda i: (0, 0)),
                  pl.BlockSpec((1, 1, D_MODEL), lambda i: (i, 0, 0)),
                  pl.BlockSpec((1, 1, D_MODEL), lambda i: (i, 0, 0))],
        out_specs=pl.BlockSpec((tm, D_MODEL), lambda i: (i, 0)),
        compiler_params=_cparams(("arbitrary",)),
        name="normmod",
    )(x, w.reshape(1, D_MODEL), sc_blk, sh_blk)


def _matmul_kernel(a_ref, b_ref, o_ref):
    o_ref[...] = jnp.dot(a_ref[...], b_ref[...], preferred_element_type=F32)


def _matmul(a, b, tm, tn):
    m, k = a.shape
    n = b.shape[1]
    return pl.pallas_call(
        _matmul_kernel,
        out_shape=jax.ShapeDtypeStruct((m, n), F32),
        grid=(n // tn, m // tm),
        in_specs=[pl.BlockSpec((tm, k), lambda j, i: (i, 0)),
                  pl.BlockSpec((k, tn), lambda j, i: (0, j))],
        out_specs=pl.BlockSpec((tm, tn), lambda j, i: (i, j)),
        compiler_params=_cparams(("arbitrary", "arbitrary")),
        name="matmul",
    )(a, b)


def _merge_kernel(oa_ref, ob_ref, oc_ref, mga_ref, mgb_ref, mgc_ref, wb_ref, wo_ref, x_ref, g1_ref,
                  n2_ref, sc_ref, sh_ref, xo_ref, h_ref):
    mix = None
    for n, (o_ref, mg_ref) in enumerate(((oa_ref, mga_ref), (ob_ref, mgb_ref), (oc_ref, mgc_ref))):
        merged = jnp.dot(o_ref[...], wb_ref[n], preferred_element_type=F32)
        gate = jax.nn.sigmoid(mg_ref[...])
        mix = gate * merged if mix is None else mix + gate * merged
    m = jnp.dot(mix.astype(BF16), wo_ref[...], preferred_element_type=F32)
    x = x_ref[...] + g1_ref[0] * m
    xo_ref[...] = x
    y = x * lax.rsqrt(jnp.mean(x * x, axis=-1, keepdims=True) + EPS)
    y = y * n2_ref[...]
    h_ref[...] = (y * (1.0 + sc_ref[0]) + sh_ref[0]).astype(h_ref.dtype)


def _merge(o_ret, o_diff, o_gdn, proj, wb, wo, x, g1_blk, n2w, sc2_blk, sh2_blk):
    tm = MOD_BLK
    tok = lambda i: (i, 0)
    blk = lambda i: (i, 0, 0)
    return pl.pallas_call(
        _merge_kernel,
        out_shape=(jax.ShapeDtypeStruct((N_TOK, D_MODEL), F32),
                   jax.ShapeDtypeStruct((N_TOK, D_MODEL), BF16)),
        grid=(N_TOK // tm,),
        in_specs=[pl.BlockSpec((tm, BRANCH_W), tok),
                  pl.BlockSpec((tm, BRANCH_W), tok),
                  pl.BlockSpec((tm, BRANCH_W), tok),
                  pl.BlockSpec((tm, D_MODEL), lambda i: (i, C_MG // D_MODEL)),
                  pl.BlockSpec((tm, D_MODEL), lambda i: (i, C_MG // D_MODEL + 1)),
                  pl.BlockSpec((tm, D_MODEL), lambda i: (i, C_MG // D_MODEL + 2)),
                  pl.BlockSpec((N_BRANCH, BRANCH_W, D_MODEL), lambda i: (0, 0, 0)),
                  pl.BlockSpec((D_MODEL, D_MODEL), lambda i: (0, 0)),
                  pl.BlockSpec((tm, D_MODEL), tok),
                  pl.BlockSpec((1, 1, D_MODEL), blk),
                  pl.BlockSpec((1, D_MODEL), lambda i: (0, 0)),
                  pl.BlockSpec((1, 1, D_MODEL), blk),
                  pl.BlockSpec((1, 1, D_MODEL), blk)],
        out_specs=(pl.BlockSpec((tm, D_MODEL), tok), pl.BlockSpec((tm, D_MODEL), tok)),
        compiler_params=_cparams(("arbitrary",)),
        name="merge",
    )(o_ret, o_diff, o_gdn, proj, proj, proj, wb, wo, x, g1_blk, n2w.reshape(1, D_MODEL), sc2_blk, sh2_blk)


def _rmsnorm(x, w):
    return x * lax.rsqrt(jnp.mean(x * x, axis=-1, keepdims=True) + EPS) * w


def _l2norm(x):
    return x * lax.rsqrt(jnp.sum(x * x, axis=-1, keepdims=True) + EPS)


def _flip(x):
    return jnp.flip(x, axis=1)


def _axial_rope(x):
    L, dim = x.shape[1], x.shape[-1]
    n_rows = L // GRID_W
    row = jnp.repeat(jnp.arange(n_rows), GRID_W).astype(F32)
    col = jnp.tile(jnp.arange(GRID_W), n_rows).astype(F32)
    nf = dim // 4
    inv = jnp.power(ROPE_BASE, -jnp.arange(nf, dtype=F32) / nf)
    ang_r = row[:, None] * inv
    ang_c = col[:, None] * inv
    ang = jnp.concatenate([ang_r, ang_r, ang_c, ang_c], axis=-1)
    shape = (1, L) + (1,) * (x.ndim - 3) + (dim,)
    cos = jnp.cos(ang).reshape(shape)
    sin = jnp.sin(ang).reshape(shape)
    x_r1, x_r2, x_c1, x_c2 = jnp.split(x, 4, axis=-1)
    rot = jnp.concatenate([-x_r2, x_r1, -x_c2, x_c1], axis=-1)
    return x * cos + rot * sin


def _to_chunks(x):
    B, L, H = x.shape[:3]
    x = x.reshape((B, L // CHUNK, CHUNK, H) + x.shape[3:])
    return jnp.moveaxis(x, 3, 1)


def _from_chunks(x):
    x = jnp.moveaxis(x, 1, 3)
    B, N, C, H = x.shape[:4]
    return x.reshape((B, N * C, H) + x.shape[4:])


def _retention_scan(q, k, v, log_gamma, s0):
    q = _to_chunks(q * R_DK ** -0.5)
    k = _to_chunks(k)
    v = _to_chunks(v)
    pos = jnp.arange(CHUNK, dtype=F32)
    lg = log_gamma[:, None]
    dist = pos[:, None] - pos[None, :]
    intra = jnp.exp(jnp.where(dist >= 0, lg[:, :, None] * dist, -jnp.inf))
    q_dec = jnp.exp(lg * (pos + 1.0))[:, :, None]
    k_dec = jnp.exp(lg * (CHUNK - 1.0 - pos))[:, :, None]
    c_dec = jnp.exp(lg * CHUNK)[:, :, None]

    def step(s, xs):
        qc, kc, vc = xs
        a = jnp.einsum('bhqd,bhkd->bhqk', qc, kc) * intra
        o = jnp.einsum('bhqk,bhkv->bhqv', a, vc) + jnp.einsum('bhqd,bhdv->bhqv', qc, s) * q_dec
        s = s * c_dec + jnp.einsum('bhkd,bhkv->bhdv', kc * k_dec, vc)
        return s, o

    xs = (jnp.moveaxis(q, 2, 0), jnp.moveaxis(k, 2, 0), jnp.moveaxis(v, 2, 0))
    s, o = lax.scan(step, s0, xs)
    return _from_chunks(jnp.moveaxis(o, 0, 2)), s


def _gdn_scan(q, k, v, g, beta, s0):
    q = _to_chunks(q * G_DK ** -0.5)
    k = _to_chunks(k)
    v = _to_chunks(v)
    g = jnp.cumsum(_to_chunks(g), axis=-1)
    beta = _to_chunks(beta)
    incl = jnp.tril(jnp.ones((CHUNK, CHUNK), dtype=bool))
    strict = jnp.tril(jnp.ones((CHUNK, CHUNK), dtype=bool), -1)
    decay = jnp.exp(jnp.where(incl, g[..., :, None] - g[..., None, :], -jnp.inf))
    kb = k * beta[..., None]
    a = jnp.where(strict, jnp.einsum('bhnid,bhnjd->bhnij', kb, k) * decay, 0.0)
    eye = jnp.eye(CHUNK, dtype=F32)
    t = lax.linalg.triangular_solve(a + eye, jnp.broadcast_to(eye, a.shape), left_side=True, lower=True)
    u = t @ (v * beta[..., None])
    w = t @ (kb * jnp.exp(g)[..., None])
    qk = jnp.einsum('bhnid,bhnjd->bhnij', q, k) * decay
    qg = q * jnp.exp(g)[..., None]
    kd = k * jnp.exp(g[..., -1:] - g)[..., None]
    gl = jnp.exp(g[..., -1])[..., None, None]

    def step(s, xs):
        qg_c, kd_c, u_c, w_c, qk_c, gl_c = xs
        u_new = u_c - w_c @ s
        o = qg_c @ s + qk_c @ u_new
        s = s * gl_c + jnp.swapaxes(kd_c, -1, -2) @ u_new
        return s, o

    xs = tuple(jnp.moveaxis(arr, 2, 0) for arr in (qg, kd, u, w, qk, gl))
    s, o = lax.scan(step, s0, xs)
    return _from_chunks(jnp.moveaxis(o, 0, 2)), s


def _diff_attention(q, k, v, lam):
    scale = D_HD ** -0.5
    s = jnp.einsum('bqhpd,bkhpd->bhpqk', q, k) * scale
    p = jax.nn.softmax(s, axis=-1)
    wts = p[:, :, 0] - lam * p[:, :, 1]
    return jnp.einsum('bhqk,bkhe->bqhe', wts, v)


def _centred_dwconv(x, w):
    pad = (w.shape[0] - 1) // 2
    return lax.conv_general_dilated(x, w[:, None, :], window_strides=(1,), padding=[(pad, pad)],
                                    dimension_numbers=('NWC', 'WIO', 'NWC'),
                                    feature_group_count=x.shape[-1])


def _mixers_jax(proj, gab, lw, li, ctx):
    B, L, _ = proj.shape
    latent = ctx is not None
    r_q = proj[..., C_RQ:C_RK].reshape(B, L, R_HEADS, R_DK)
    r_k = proj[..., C_RK:C_RV].reshape(B, L, R_HEADS, R_DK)
    r_v = proj[..., C_RV:C_RG].reshape(B, L, R_HEADS, R_DV)
    r_g = proj[..., C_RG:C_DQ]
    if latent:
        r_q, r_k = _axial_rope(r_q), _axial_rope(r_k)
        r0_f, r0_b = ctx['ret'][:, 0], ctx['ret'][:, 1]
    else:
        r0_f = r0_b = jnp.zeros((B, R_HEADS, R_DK, R_DV), F32)
    log_gamma = jax.nn.log_sigmoid(lw['ret_decay'])
    or_f, sr_f = _retention_scan(r_q, r_k, r_v, log_gamma[0], r0_f)
    or_b, sr_b = _retention_scan(_flip(r_q), _flip(r_k), _flip(r_v), log_gamma[1], r0_b)
    o_ret = _rmsnorm(or_f + _flip(or_b), lw['ret_norm']).reshape(B, L, BRANCH_W) * jax.nn.silu(r_g)

    d_q = proj[..., C_DQ:C_DK].reshape(B, L, D_HEADS, 2, D_HD)
    d_k = proj[..., C_DK:C_DV].reshape(B, L, D_HEADS, 2, D_HD)
    d_v = proj[..., C_DV:C_GQKV].reshape(B, L, D_HEADS, 2 * D_HD)
    if latent:
        q_att = _axial_rope(d_q)
        k_all = jnp.concatenate([_axial_rope(d_k), ctx['k']], axis=1)
        v_all = jnp.concatenate([d_v, ctx['v']], axis=1)
    else:
        q_att, k_all, v_all = d_q, d_k, d_v
    lam_init = 0.8 - 0.6 * math.exp(-0.3 * li)
    lp = lw['diff_lambda']
    lam = jnp.exp(jnp.sum(lp[0] * lp[1])) - jnp.exp(jnp.sum(lp[2] * lp[3])) + lam_init
    o_diff = _diff_attention(q_att, k_all, v_all, lam)
    o_diff = (_rmsnorm(o_diff, lw['diff_norm']) * (1.0 - lam_init)).reshape(B, L, BRANCH_W)

    g_qkv = jax.nn.silu(_centred_dwconv(proj[..., C_GQKV:C_GZ], lw['gdn_conv']))
    g_q, g_k, g_v = jnp.split(g_qkv, [G_HEADS * G_DK, 2 * G_HEADS * G_DK], axis=-1)
    g_q = _l2norm(g_q.reshape(B, L, G_HEADS, G_DK))
    g_k = _l2norm(g_k.reshape(B, L, G_HEADS, G_DK))
    g_v = g_v.reshape(B, L, G_HEADS, G_DV)
    g_z = proj[..., C_GZ:C_MG]
    g_a = gab[..., :16].reshape(B, L, 2, G_HEADS)
    g_b = gab[..., 16:32].reshape(B, L, 2, G_HEADS)
    g_log = -jnp.exp(lw['gdn_A_log']) * jax.nn.softplus(g_a + lw['gdn_dt_bias'])
    g_beta = jax.nn.sigmoid(g_b)
    if latent:
        g0_f, g0_b = ctx['gdn'][:, 0], ctx['gdn'][:, 1]
    else:
        g0_f = g0_b = jnp.zeros((B, G_HEADS, G_DK, G_DV), F32)
    og_f, sg_f = _gdn_scan(g_q, g_k, g_v, g_log[:, :, 0], g_beta[:, :, 0], g0_f)
    og_b, sg_b = _gdn_scan(_flip(g_q), _flip(g_k), _flip(g_v), _flip(g_log[:, :, 1]), _flip(g_beta[:, :, 1]), g0_b)
    o_gdn = _rmsnorm(og_f + _flip(og_b), lw['gdn_norm']) * jax.nn.silu(g_z.reshape(B, L, G_HEADS, G_DV))
    o_gdn = o_gdn.reshape(B, L, BRANCH_W)
    state = (d_k, d_v, jnp.stack([sr_f, sr_b], axis=1), jnp.stack([sg_f, sg_b], axis=1))
    return o_ret, o_diff, o_gdn, state


def _peer_jax(h, w_q, sub_keys, exp_u, exp_v):
    hb = h.reshape(N_TOK // TOK_BLOCK, TOK_BLOCK, D_MODEL)

    def block(hx):
        q = jnp.dot(hx, w_q, preferred_element_type=F32).reshape(TOK_BLOCK, PEER_HEADS, 2, PEER_DQ // 2)
        s = jnp.einsum('thpd,pnd->thpn', q, sub_keys)
        s_top, i_top = lax.top_k(s, PEER_TOPK)
        cand_s = (s_top[:, :, 0, :, None] + s_top[:, :, 1, None, :]).reshape(TOK_BLOCK, PEER_HEADS, -1)
        cand_i = (i_top[:, :, 0, :, None] * N_KEYS + i_top[:, :, 1, None, :]).reshape(TOK_BLOCK, PEER_HEADS, -1)
        best_s, pos = lax.top_k(cand_s, PEER_TOPK)
        idx = jnp.take_along_axis(cand_i, pos, axis=-1)
        gate = jax.nn.softmax(best_s, axis=-1)
        hf = hx.astype(F32)
        act = jax.nn.gelu(jnp.einsum('thkd,td->thk', exp_u[idx], hf), approximate=False)
        return jnp.einsum('thk,thkd->td', gate * act, exp_v[idx])

    return lax.map(block, hb).reshape(N_TOK, D_MODEL)


def _per_block(rows):
    idx = np.concatenate([np.zeros(N_CTX // MOD_BLK, np.int32)] +
                         [np.full(DEC_SEQ // MOD_BLK, 1 + b, np.int32) for b in range(DEC_BATCH)])
    return rows[idx][:, None, :]


def kernel(x_prompt, x_sample, cache_diff_k, cache_diff_v, state_ret, state_gdn, c, c_ctx,
           norm1_w, norm2_w, w_mod, b_mod, w_in, ret_decay, ret_norm, diff_lambda, diff_norm,
           gdn_conv, gdn_A_log, gdn_dt_bias, gdn_norm, w_branch, w_out,
           peer_wq, peer_keys, peer_u, peer_v, norm_f_w):
    x = jnp.concatenate([x_prompt.reshape(N_CTX, D_MODEL), x_sample.reshape(N_LAT, D_MODEL)], axis=0)
    cond = jnp.concatenate([c_ctx[None, :], c, jnp.zeros((8 - N_COND, D_MODEL), F32)], axis=0)
    mod_all = _modulation(cond, w_mod, b_mod)

    w_main = jnp.concatenate([w_in[:, :, :AB_OFF], w_in[:, :, AB_OFF + 32:]], axis=-1).astype(BF16)
    w_ab = jnp.pad(w_in[:, :, AB_OFF:AB_OFF + 32], ((0, 0), (0, 0), (0, 96))).astype(BF16)
    wb_bf = w_branch.astype(BF16)
    wo_bf = w_out.astype(BF16)
    wq_bf = peer_wq.astype(BF16)

    ks, vs, srs, sgs = [], [], [], []
    for li in range(DEPTH):
        lw = dict(ret_decay=ret_decay[li], ret_norm=ret_norm[li], diff_lambda=diff_lambda[li],
                  diff_norm=diff_norm[li], gdn_conv=gdn_conv[li], gdn_A_log=gdn_A_log[li],
                  gdn_dt_bias=gdn_dt_bias[li], gdn_norm=gdn_norm[li])
        mod = mod_all[li, :N_COND]
        sh1, sc1, g1, sh2, sc2, g2 = [_per_block(m) for m in jnp.split(mod, 6, axis=-1)]
        h = _normmod(x, norm1_w[li], sc1, sh1, BF16)
        proj = _matmul(h, w_main[li], 512, 1024)
        gab = _matmul(h, w_ab[li], 512, 128)

        pc = proj[:N_CTX].reshape(BATCH, SEQ, N_MAIN)
        gc = gab[:N_CTX, :32].reshape(BATCH, SEQ, 32)
        oa, ob, oc, (k_l, v_l, sr_l, sg_l) = _mixers_jax(pc, gc, lw, li, None)
        ks.append(k_l); vs.append(v_l); srs.append(sr_l); sgs.append(sg_l)
        pz = proj[N_CTX:].reshape(DEC_BATCH, DEC_SEQ, N_MAIN)
        gz = gab[N_CTX:, :32].reshape(DEC_BATCH, DEC_SEQ, 32)
        ctx = {'k': cache_diff_k[:, li], 'v': cache_diff_v[:, li], 'ret': state_ret[:, li], 'gdn': state_gdn[:, li]}
        za, zb, zc, _ = _mixers_jax(pz, gz, lw, li, ctx)
        cat = lambda a, b: jnp.concatenate([a.reshape(N_CTX, BRANCH_W), b.reshape(N_LAT, BRANCH_W)], axis=0).astype(BF16)
        o_ret, o_diff, o_gdn = cat(oa, za), cat(ob, zb), cat(oc, zc)

        x, h2 = _merge(o_ret, o_diff, o_gdn, proj, wb_bf[li], wo_bf[li], x, g1, norm2_w[li], sc2, sh2)
        p = _peer_jax(h2, wq_bf[li], peer_keys[li], peer_u[li], peer_v[li])
        x = x + jnp.repeat(g2[:, 0, :], MOD_BLK, axis=0) * p

    zeros_blk = jnp.zeros((N_MOD_BLK, 1, D_MODEL), F32)
    y = _normmod(x, norm_f_w, zeros_blk, zeros_blk, F32)
    y_prompt = y[:N_CTX].reshape(BATCH, SEQ, D_MODEL)
    y_sample = y[N_CTX:].reshape(DEC_BATCH, DEC_SEQ, D_MODEL)
    return (y_prompt, y_sample, jnp.stack(ks, axis=1), jnp.stack(vs, axis=1),
            jnp.stack(srs, axis=1), jnp.stack(sgs, axis=1))
```

```python
import functools
import math

import numpy as np
import jax
import jax.numpy as jnp
from jax import lax
from jax.experimental import pallas as pl
from jax.experimental.pallas import tpu as pltpu

F32 = jnp.float32
BF16 = jnp.bfloat16

D_MODEL = 1024
BATCH = 16
SEQ = 256
DEPTH = 2
DEC_BATCH = 2
DEC_SEQ = 1024
PAST_LEN = 256
GRID_W = 64
ROPE_BASE = 10000.0
EPS = 1e-6
CHUNK = 64
R_HEADS, R_DK, R_DV = 4, 128, 256
D_HEADS, D_HD = 8, 64
G_HEADS, G_DK, G_DV = 8, 128, 128
CONV_K = 7
G_QKV = G_HEADS * (2 * G_DK + G_DV)
N_BRANCH = 3
BRANCH_W = 1024
PEER_HEADS = 8
PEER_DQ = 256
N_KEYS = 128
PEER_TOPK = 16
TOK_BLOCK = 128

N_CTX = BATCH * SEQ
N_LAT = DEC_BATCH * DEC_SEQ
N_TOK = N_CTX + N_LAT
N_COND = 1 + DEC_BATCH
MOD_BLK = 256
N_MOD_BLK = N_TOK // MOD_BLK

C_RQ, C_RK, C_RV, C_RG = 0, 512, 1024, 2048
C_DQ, C_DK, C_DV = 3072, 4096, 5120
C_GQKV, C_GZ, C_MG = 6144, 9216, 10240
N_MAIN = 13312
AB_OFF = 10240

VMEM_LIMIT = 56 * 1024 * 1024


def _cparams(sem):
    return pltpu.CompilerParams(dimension_semantics=sem, vmem_limit_bytes=VMEM_LIMIT)


def _mod_kernel(c_ref, w_ref, b_ref, o_ref):
    c = c_ref[...]
    a = c * jax.nn.sigmoid(c)
    o_ref[0] = jnp.dot(a, w_ref[0], preferred_element_type=F32,
                       precision=lax.Precision.HIGHEST) + b_ref[0]


def _modulation(cond_pad, w_mod, b_mod):
    tn = 1536
    return pl.pallas_call(
        _mod_kernel,
        out_shape=jax.ShapeDtypeStruct((DEPTH, 8, 6 * D_MODEL), F32),
        grid=(DEPTH, 6 * D_MODEL // tn),
        in_specs=[pl.BlockSpec((8, D_MODEL), lambda l, j: (0, 0)),
                  pl.BlockSpec((1, D_MODEL, tn), lambda l, j: (l, 0, j)),
                  pl.BlockSpec((1, 1, tn), lambda l, j: (l, 0, j))],
        out_specs=pl.BlockSpec((1, 8, tn), lambda l, j: (l, 0, j)),
        compiler_params=_cparams(("arbitrary", "arbitrary")),
        name="modulation",
    )(cond_pad, w_mod, b_mod.reshape(DEPTH, 1, 6 * D_MODEL))


def _normmod_kernel(x_ref, w_ref, sc_ref, sh_ref, o_ref):
    x = x_ref[...]
    y = x * lax.rsqrt(jnp.mean(x * x, axis=-1, keepdims=True) + EPS)
    y = y * w_ref[...]
    o_ref[...] = (y * (1.0 + sc_ref[0]) + sh_ref[0]).astype(o_ref.dtype)


def _normmod(x, w, sc_blk, sh_blk, out_dtype):
    tm = MOD_BLK
    return pl.pallas_call(
        _normmod_kernel,
        out_shape=jax.ShapeDtypeStruct((N_TOK, D_MODEL), out_dtype),
        grid=(N_TOK // tm,),
        in_specs=[pl.BlockSpec((tm, D_MODEL), lambda i: (i, 0)),
                  pl.BlockSpec((1, D_MODEL), lambda i: (0, 0)),
                  pl.BlockSpec((1, 1, D_MODEL), lambda i: (i, 0, 0)),
                  pl.BlockSpec((1, 1, D_MODEL), lambda i: (i, 0, 0))],
        out_specs=pl.BlockSpec((tm, D_MODEL), lambda i: (i, 0)),
        compiler_params=_cparams(("arbitrary",)),
        name="normmod",
    )(x, w.reshape(1, D_MODEL), sc_blk, sh_blk)


def _matmul_kernel(a_ref, b_ref, o_ref):
    o_ref[...] = jnp.dot(a_ref[...], b_ref[...], preferred_element_type=F32)


def _matmul(a, b, tm, tn):
    m, k = a.shape
    n = b.shape[1]
    return pl.pallas_call(
        _matmul_kernel,
        out_shape=jax.ShapeDtypeStruct((m, n), F32),
        grid=(n // tn, m // tm),
        in_specs=[pl.BlockSpec((tm, k), lambda j, i: (i, 0)),
                  pl.BlockSpec((k, tn), lambda j, i: (0, j))],
        out_specs=pl.BlockSpec((tm, tn), lambda j, i: (i, j)),
        compiler_params=_cparams(("arbitrary", "arbitrary")),
        name="matmul",
    )(a, b)


def _merge_kernel(oa_ref, ob_ref, oc_ref, mga_ref, mgb_ref, mgc_ref, wb_ref, wo_ref, x_ref, g1_ref,
                  n2_ref, sc_ref, sh_ref, xo_ref, h_ref):
    mix = None
    for n, (o_ref, mg_ref) in enumerate(((oa_ref, mga_ref), (ob_ref, mgb_ref), (oc_ref, mgc_ref))):
        merged = jnp.dot(o_ref[...], wb_ref[n], preferred_element_type=F32)
        gate = jax.nn.sigmoid(mg_ref[...])
        mix = gate * merged if mix is None else mix + gate * merged
    m = jnp.dot(mix.astype(BF16), wo_ref[...], preferred_element_type=F32)
    x = x_ref[...] + g1_ref[0] * m
    xo_ref[...] = x
    y = x * lax.rsqrt(jnp.mean(x * x, axis=-1, keepdims=True) + EPS)
    y = y * n2_ref[...]
    h_ref[...] = (y * (1.0 + sc_ref[0]) + sh_ref[0]).astype(h_ref.dtype)


def _merge(o_ret, o_diff, o_gdn, proj, wb, wo, x, g1_blk, n2w, sc2_blk, sh2_blk):
    tm = MOD_BLK
    tok = lambda i: (i, 0)
    blk = lambda i: (i, 0, 0)
    return pl.pallas_call(
        _merge_kernel,
        out_shape=(jax.ShapeDtypeStruct((N_TOK, D_MODEL), F32),
                   jax.ShapeDtypeStruct((N_TOK, D_MODEL), BF16)),
        grid=(N_TOK // tm,),
        in_specs=[pl.BlockSpec((tm, BRANCH_W), tok),
                  pl.BlockSpec((tm, BRANCH_W), tok),
                  pl.BlockSpec((tm, BRANCH_W), tok),
                  pl.BlockSpec((tm, D_MODEL), lambda i: (i, C_MG // D_MODEL)),
                  pl.BlockSpec((tm, D_MODEL), lambda i: (i, C_MG // D_MODEL + 1)),
                  pl.BlockSpec((tm, D_MODEL), lambda i: (i, C_MG // D_MODEL + 2)),
                  pl.BlockSpec((N_BRANCH, BRANCH_W, D_MODEL), lambda i: (0, 0, 0)),
                  pl.BlockSpec((D_MODEL, D_MODEL), lambda i: (0, 0)),
                  pl.BlockSpec((tm, D_MODEL), tok),
                  pl.BlockSpec((1, 1, D_MODEL), blk),
                  pl.BlockSpec((1, D_MODEL), lambda i: (0, 0)),
                  pl.BlockSpec((1, 1, D_MODEL), blk),
                  pl.BlockSpec((1, 1, D_MODEL), blk)],
        out_specs=(pl.BlockSpec((tm, D_MODEL), tok), pl.BlockSpec((tm, D_MODEL), tok)),
        compiler_params=_cparams(("arbitrary",)),
        name="merge",
    )(o_ret, o_diff, o_gdn, proj, proj, proj, wb, wo, x, g1_blk, n2w.reshape(1, D_MODEL), sc2_blk, sh2_blk)


ROUTE_T = 128
N_PAIR = PEER_HEADS * PEER_TOPK
NEG_INF = float("-inf")


def _split_bf16(x):
    hi = x.astype(BF16)
    return hi, (x - hi.astype(F32)).astype(BF16)


def _topk_rows(s, rows, n_rows, payload=None):
    vals, outs = [], []
    for _ in range(PEER_TOPK):
        m = jnp.max(s, axis=0, keepdims=True)
        pos = jnp.min(jnp.where(s == m, rows, n_rows), axis=0, keepdims=True)
        sel = rows == pos
        vals.append(m)
        if payload is None:
            outs.append(pos)
        else:
            outs.append(jnp.sum(jnp.where(sel, payload, 0), axis=0, keepdims=True))
        s = jnp.where(sel, NEG_INF, s)
    return jnp.concatenate(vals, axis=0), jnp.concatenate(outs, axis=0)


def _route_kernel(h_ref, wq_ref, khi_ref, klo_ref, idx_ref, gate_ref, q_scr, idx_scr, gate_scr):
    t = h_ref.shape[0]
    q = jnp.dot(h_ref[...], wq_ref[...], preferred_element_type=F32)
    for hp in range(2 * PEER_HEADS):
        q_scr[hp] = q[:, hp * 128:(hp + 1) * 128]
    rows_k = lax.broadcasted_iota(jnp.int32, (N_KEYS, t), 0)
    rows_c = lax.broadcasted_iota(jnp.int32, (PEER_TOPK * PEER_TOPK, t), 0)
    nt = (((1,), (1,)), ((), ()))

    def head(h, carry):
        tops = []
        for p in range(2):
            qhi, qlo = _split_bf16(q_scr[2 * h + p])
            s = (lax.dot_general(khi_ref[p], qhi, nt, preferred_element_type=F32)
                 + lax.dot_general(khi_ref[p], qlo, nt, preferred_element_type=F32)
                 + lax.dot_general(klo_ref[p], qhi, nt, preferred_element_type=F32))
            tops.append(_topk_rows(s, rows_k, N_KEYS))
        (v1, i1), (v2, i2) = tops
        cand = jnp.concatenate([v1[k:k + 1] + v2 for k in range(PEER_TOPK)], axis=0)
        cand_i = jnp.concatenate([i1[k:k + 1] * N_KEYS + i2 for k in range(PEER_TOPK)], axis=0)
        best, bidx = _topk_rows(cand, rows_c, PEER_TOPK * PEER_TOPK, payload=cand_i)
        e = jnp.exp(best - best[0:1])
        gate_scr[h] = e / jnp.sum(e, axis=0, keepdims=True)
        idx_scr[h] = bidx
        return carry

    lax.fori_loop(0, PEER_HEADS, head, 0)
    idx_ref[...] = idx_scr[...].reshape(N_PAIR, t).T
    gate_ref[...] = gate_scr[...].reshape(N_PAIR, t).T


def _peer_route(h2, wq, keys):
    khi, klo = _split_bf16(keys)
    t = ROUTE_T
    n_tok = h2.shape[0]
    return pl.pallas_call(
        _route_kernel,
        out_shape=(jax.ShapeDtypeStruct((n_tok, N_PAIR), jnp.int32),
                   jax.ShapeDtypeStruct((n_tok, N_PAIR), F32)),
        grid=(n_tok // t,),
        in_specs=[pl.BlockSpec((t, D_MODEL), lambda i: (i, 0)),
                  pl.BlockSpec((D_MODEL, PEER_HEADS * PEER_DQ), lambda i: (0, 0)),
                  pl.BlockSpec((2, N_KEYS, PEER_DQ // 2), lambda i: (0, 0, 0)),
                  pl.BlockSpec((2, N_KEYS, PEER_DQ // 2), lambda i: (0, 0, 0))],
        out_specs=(pl.BlockSpec((t, N_PAIR), lambda i: (i, 0)),
                   pl.BlockSpec((t, N_PAIR), lambda i: (i, 0))),
        scratch_shapes=[pltpu.VMEM((2 * PEER_HEADS, t, PEER_DQ // 2), F32),
                        pltpu.VMEM((PEER_HEADS, PEER_TOPK, t), jnp.int32),
                        pltpu.VMEM((PEER_HEADS, PEER_TOPK, t), F32)],
        compiler_params=_cparams(("arbitrary",)),
        name="peer_route",
    )(h2, wq, khi, klo)


EXP_TB = 256
EXP_EB = 512
N_EXPERTS = N_KEYS * N_KEYS


def _expert_kernel(h_ref, idx_ref, gate_ref, ut_ref, v_ref, x_ref, g2_ref, o_ref, g_scr, acc_ref):
    j = pl.program_id(1)
    tb = h_ref.shape[0]
    nt = (((1,), (1,)), ((), ()))

    @pl.when(j == 0)
    def _():
        acc_ref[...] = jnp.zeros_like(acc_ref)
        sub = lax.broadcasted_iota(jnp.int32, (N_KEYS, N_PAIR), 0)

        def body(t, carry):
            e = idx_ref[pl.ds(t, 1), :]
            g = gate_ref[pl.ds(t, 1), :]
            ghi = g.astype(BF16).astype(F32)
            glo = g - ghi
            xa = jnp.where(sub == (e >> 7), 1.0, 0.0).astype(BF16)
            mb = sub == (e & (N_KEYS - 1))
            yhi = jnp.where(mb, ghi, 0.0).astype(BF16)
            ylo = jnp.where(mb, glo, 0.0).astype(BF16)
            gt = lax.dot_general(jnp.concatenate([xa, xa], axis=1), jnp.concatenate([yhi, ylo], axis=1), nt,
                                 preferred_element_type=F32)
            g_scr[pl.ds(pl.multiple_of(t * N_KEYS, N_KEYS), N_KEYS), :] = gt
            return carry

        lax.fori_loop(0, tb, body, 0)

    s = jnp.dot(h_ref[...], ut_ref[...], preferred_element_type=F32)
    nb = EXP_EB // N_KEYS
    gj = jnp.concatenate([g_scr[pl.ds(j * nb + c, tb, stride=N_KEYS), :] for c in range(nb)], axis=1)
    w = 0.5 * s * (1.0 + lax.erf(s * (2.0 ** -0.5))) * gj
    acc_ref[...] += jnp.dot(w.astype(BF16), v_ref[...], preferred_element_type=F32)

    @pl.when(j == pl.num_programs(1) - 1)
    def _():
        o_ref[...] = x_ref[...] + g2_ref[0] * acc_ref[...]


def _peer_experts(h2, idx, gate, u_t, v, x, g2_blk):
    tb, eb = EXP_TB, EXP_EB
    n_tok = h2.shape[0]
    return pl.pallas_call(
        _expert_kernel,
        out_shape=jax.ShapeDtypeStruct((n_tok, D_MODEL), F32),
        grid=(n_tok // tb, N_EXPERTS // eb),
        in_specs=[pl.BlockSpec((tb, D_MODEL), lambda i, j: (i, 0)),
                  pl.BlockSpec((tb, N_PAIR), lambda i, j: (i, 0)),
                  pl.BlockSpec((tb, N_PAIR), lambda i, j: (i, 0)),
                  pl.BlockSpec((D_MODEL, eb), lambda i, j: (0, j)),
                  pl.BlockSpec((eb, D_MODEL), lambda i, j: (j, 0)),
                  pl.BlockSpec((tb, D_MODEL), lambda i, j: (i, 0)),
                  pl.BlockSpec((1, 1, D_MODEL), lambda i, j: (i * tb // MOD_BLK, 0, 0))],
        out_specs=pl.BlockSpec((tb, D_MODEL), lambda i, j: (i, 0)),
        scratch_shapes=[pltpu.VMEM((tb * N_KEYS, N_KEYS), F32),
                        pltpu.VMEM((tb, D_MODEL), F32)],
        compiler_params=_cparams(("arbitrary", "arbitrary")),
        name="peer_experts",
    )(h2, idx, gate, u_t, v, x, g2_blk)


def _rmsnorm(x, w):
    return x * lax.rsqrt(jnp.mean(x * x, axis=-1, keepdims=True) + EPS) * w


def _l2norm(x):
    return x * lax.rsqrt(jnp.sum(x * x, axis=-1, keepdims=True) + EPS)


def _flip(x):
    return jnp.flip(x, axis=1)


def _axial_rope(x):
    L, dim = x.shape[1], x.shape[-1]
    n_rows = L // GRID_W
    row = jnp.repeat(jnp.arange(n_rows), GRID_W).astype(F32)
    col = jnp.tile(jnp.arange(GRID_W), n_rows).astype(F32)
    nf = dim // 4
    inv = jnp.power(ROPE_BASE, -jnp.arange(nf, dtype=F32) / nf)
    ang_r = row[:, None] * inv
    ang_c = col[:, None] * inv
    ang = jnp.concatenate([ang_r, ang_r, ang_c, ang_c], axis=-1)
    shape = (1, L) + (1,) * (x.ndim - 3) + (dim,)
    cos = jnp.cos(ang).reshape(shape)
    sin = jnp.sin(ang).reshape(shape)
    x_r1, x_r2, x_c1, x_c2 = jnp.split(x, 4, axis=-1)
    rot = jnp.concatenate([-x_r2, x_r1, -x_c2, x_c1], axis=-1)
    return x * cos + rot * sin


def _to_chunks(x):
    B, L, H = x.shape[:3]
    x = x.reshape((B, L // CHUNK, CHUNK, H) + x.shape[3:])
    return jnp.moveaxis(x, 3, 1)


def _from_chunks(x):
    x = jnp.moveaxis(x, 1, 3)
    B, N, C, H = x.shape[:4]
    return x.reshape((B, N * C, H) + x.shape[4:])


def _retention_scan(q, k, v, log_gamma, s0):
    q = _to_chunks(q * R_DK ** -0.5)
    k = _to_chunks(k)
    v = _to_chunks(v)
    pos = jnp.arange(CHUNK, dtype=F32)
    lg = log_gamma[:, None]
    dist = pos[:, None] - pos[None, :]
    intra = jnp.exp(jnp.where(dist >= 0, lg[:, :, None] * dist, -jnp.inf))
    q_dec = jnp.exp(lg * (pos + 1.0))[:, :, None]
    k_dec = jnp.exp(lg * (CHUNK - 1.0 - pos))[:, :, None]
    c_dec = jnp.exp(lg * CHUNK)[:, :, None]

    def step(s, xs):
        qc, kc, vc = xs
        a = jnp.einsum('bhqd,bhkd->bhqk', qc, kc) * intra
        o = jnp.einsum('bhqk,bhkv->bhqv', a, vc) + jnp.einsum('bhqd,bhdv->bhqv', qc, s) * q_dec
        s = s * c_dec + jnp.einsum('bhkd,bhkv->bhdv', kc * k_dec, vc)
        return s, o

    xs = (jnp.moveaxis(q, 2, 0), jnp.moveaxis(k, 2, 0), jnp.moveaxis(v, 2, 0))
    s, o = lax.scan(step, s0, xs)
    return _from_chunks(jnp.moveaxis(o, 0, 2)), s


def _gdn_scan(q, k, v, g, beta, s0):
    q = _to_chunks(q * G_DK ** -0.5)
    k = _to_chunks(k)
    v = _to_chunks(v)
    g = jnp.cumsum(_to_chunks(g), axis=-1)
    beta = _to_chunks(beta)
    incl = jnp.tril(jnp.ones((CHUNK, CHUNK), dtype=bool))
    strict = jnp.tril(jnp.ones((CHUNK, CHUNK), dtype=bool), -1)
    decay = jnp.exp(jnp.where(incl, g[..., :, None] - g[..., None, :], -jnp.inf))
    kb = k * beta[..., None]
    a = jnp.where(strict, jnp.einsum('bhnid,bhnjd->bhnij', kb, k) * decay, 0.0)
    eye = jnp.eye(CHUNK, dtype=F32)
    t = lax.linalg.triangular_solve(a + eye, jnp.broadcast_to(eye, a.shape), left_side=True, lower=True)
    u = t @ (v * beta[..., None])
    w = t @ (kb * jnp.exp(g)[..., None])
    qk = jnp.einsum('bhnid,bhnjd->bhnij', q, k) * decay
    qg = q * jnp.exp(g)[..., None]
    kd = k * jnp.exp(g[..., -1:] - g)[..., None]
    gl = jnp.exp(g[..., -1])[..., None, None]

    def step(s, xs):
        qg_c, kd_c, u_c, w_c, qk_c, gl_c = xs
        u_new = u_c - w_c @ s
        o = qg_c @ s + qk_c @ u_new
        s = s * gl_c + jnp.swapaxes(kd_c, -1, -2) @ u_new
        return s, o

    xs = tuple(jnp.moveaxis(arr, 2, 0) for arr in (qg, kd, u, w, qk, gl))
    s, o = lax.scan(step, s0, xs)
    return _from_chunks(jnp.moveaxis(o, 0, 2)), s


def _diff_attention(q, k, v, lam):
    scale = D_HD ** -0.5
    s = jnp.einsum('bqhpd,bkhpd->bhpqk', q, k) * scale
    p = jax.nn.softmax(s, axis=-1)
    wts = p[:, :, 0] - lam * p[:, :, 1]
    return jnp.einsum('bhqk,bkhe->bqhe', wts, v)


def _centred_dwconv(x, w):
    pad = (w.shape[0] - 1) // 2
    return lax.conv_general_dilated(x, w[:, None, :], window_strides=(1,), padding=[(pad, pad)],
                                    dimension_numbers=('NWC', 'WIO', 'NWC'),
                                    feature_group_count=x.shape[-1])


def _mixers_jax(proj, gab, lw, li, ctx):
    B, L, _ = proj.shape
    latent = ctx is not None
    r_q = proj[..., C_RQ:C_RK].reshape(B, L, R_HEADS, R_DK)
    r_k = proj[..., C_RK:C_RV].reshape(B, L, R_HEADS, R_DK)
    r_v = proj[..., C_RV:C_RG].reshape(B, L, R_HEADS, R_DV)
    r_g = proj[..., C_RG:C_DQ]
    if latent:
        r_q, r_k = _axial_rope(r_q), _axial_rope(r_k)
        r0_f, r0_b = ctx['ret'][:, 0], ctx['ret'][:, 1]
    else:
        r0_f = r0_b = jnp.zeros((B, R_HEADS, R_DK, R_DV), F32)
    log_gamma = jax.nn.log_sigmoid(lw['ret_decay'])
    or_f, sr_f = _retention_scan(r_q, r_k, r_v, log_gamma[0], r0_f)
    or_b, sr_b = _retention_scan(_flip(r_q), _flip(r_k), _flip(r_v), log_gamma[1], r0_b)
    o_ret = _rmsnorm(or_f + _flip(or_b), lw['ret_norm']).reshape(B, L, BRANCH_W) * jax.nn.silu(r_g)

    d_q = proj[..., C_DQ:C_DK].reshape(B, L, D_HEADS, 2, D_HD)
    d_k = proj[..., C_DK:C_DV].reshape(B, L, D_HEADS, 2, D_HD)
    d_v = proj[..., C_DV:C_GQKV].reshape(B, L, D_HEADS, 2 * D_HD)
    if latent:
        q_att = _axial_rope(d_q)
        k_all = jnp.concatenate([_axial_rope(d_k), ctx['k']], axis=1)
        v_all = jnp.concatenate([d_v, ctx['v']], axis=1)
    else:
        q_att, k_all, v_all = d_q, d_k, d_v
    lam_init = 0.8 - 0.6 * math.exp(-0.3 * li)
    lp = lw['diff_lambda']
    lam = jnp.exp(jnp.sum(lp[0] * lp[1])) - jnp.exp(jnp.sum(lp[2] * lp[3])) + lam_init
    o_diff = _diff_attention(q_att, k_all, v_all, lam)
    o_diff = (_rmsnorm(o_diff, lw['diff_norm']) * (1.0 - lam_init)).reshape(B, L, BRANCH_W)

    g_qkv = jax.nn.silu(_centred_dwconv(proj[..., C_GQKV:C_GZ], lw['gdn_conv']))
    g_q, g_k, g_v = jnp.split(g_qkv, [G_HEADS * G_DK, 2 * G_HEADS * G_DK], axis=-1)
    g_q = _l2norm(g_q.reshape(B, L, G_HEADS, G_DK))
    g_k = _l2norm(g_k.reshape(B, L, G_HEADS, G_DK))
    g_v = g_v.reshape(B, L, G_HEADS, G_DV)
    g_z = proj[..., C_GZ:C_MG]
    g_a = gab[..., :16].reshape(B, L, 2, G_HEADS)
    g_b = gab[..., 16:32].reshape(B, L, 2, G_HEADS)
    g_log = -jnp.exp(lw['gdn_A_log']) * jax.nn.softplus(g_a + lw['gdn_dt_bias'])
    g_beta = jax.nn.sigmoid(g_b)
    if latent:
        g0_f, g0_b = ctx['gdn'][:, 0], ctx['gdn'][:, 1]
    else:
        g0_f = g0_b = jnp.zeros((B, G_HEADS, G_DK, G_DV), F32)
    og_f, sg_f = _gdn_scan(g_q, g_k, g_v, g_log[:, :, 0], g_beta[:, :, 0], g0_f)
    og_b, sg_b = _gdn_scan(_flip(g_q), _flip(g_k), _flip(g_v), _flip(g_log[:, :, 1]), _flip(g_beta[:, :, 1]), g0_b)
    o_gdn = _rmsnorm(og_f + _flip(og_b), lw['gdn_norm']) * jax.nn.silu(g_z.reshape(B, L, G_HEADS, G_DV))
    o_gdn = o_gdn.reshape(B, L, BRANCH_W)
    state = (d_k, d_v, jnp.stack([sr_f, sr_b], axis=1), jnp.stack([sg_f, sg_b], axis=1))
    return o_ret, o_diff, o_gdn, state


def _peer_jax(h, w_q, sub_keys, exp_u, exp_v):
    hb = h.reshape(N_TOK // TOK_BLOCK, TOK_BLOCK, D_MODEL)

    def block(hx):
        q = jnp.dot(hx, w_q, preferred_element_type=F32).reshape(TOK_BLOCK, PEER_HEADS, 2, PEER_DQ // 2)
        s = jnp.einsum('thpd,pnd->thpn', q, sub_keys)
        s_top, i_top = lax.top_k(s, PEER_TOPK)
        cand_s = (s_top[:, :, 0, :, None] + s_top[:, :, 1, None, :]).reshape(TOK_BLOCK, PEER_HEADS, -1)
        cand_i = (i_top[:, :, 0, :, None] * N_KEYS + i_top[:, :, 1, None, :]).reshape(TOK_BLOCK, PEER_HEADS, -1)
        best_s, pos = lax.top_k(cand_s, PEER_TOPK)
        idx = jnp.take_along_axis(cand_i, pos, axis=-1)
        gate = jax.nn.softmax(best_s, axis=-1)
        hf = hx.astype(F32)
        act = jax.nn.gelu(jnp.einsum('thkd,td->thk', exp_u[idx], hf), approximate=False)
        return jnp.einsum('thk,thkd->td', gate * act, exp_v[idx])

    return lax.map(block, hb).reshape(N_TOK, D_MODEL)


def _per_block(rows):
    idx = np.concatenate([np.zeros(N_CTX // MOD_BLK, np.int32)] +
                         [np.full(DEC_SEQ // MOD_BLK, 1 + b, np.int32) for b in range(DEC_BATCH)])
    return rows[idx][:, None, :]


def kernel(x_prompt, x_sample, cache_diff_k, cache_diff_v, state_ret, state_gdn, c, c_ctx,
           norm1_w, norm2_w, w_mod, b_mod, w_in, ret_decay, ret_norm, diff_lambda, diff_norm,
           gdn_conv, gdn_A_log, gdn_dt_bias, gdn_norm, w_branch, w_out,
           peer_wq, peer_keys, peer_u, peer_v, norm_f_w):
    x = jnp.concatenate([x_prompt.reshape(N_CTX, D_MODEL), x_sample.reshape(N_LAT, D_MODEL)], axis=0)
    cond = jnp.concatenate([c_ctx[None, :], c, jnp.zeros((8 - N_COND, D_MODEL), F32)], axis=0)
    mod_all = _modulation(cond, w_mod, b_mod)

    w_main = jnp.concatenate([w_in[:, :, :AB_OFF], w_in[:, :, AB_OFF + 32:]], axis=-1).astype(BF16)
    w_ab = jnp.pad(w_in[:, :, AB_OFF:AB_OFF + 32], ((0, 0), (0, 0), (0, 96))).astype(BF16)
    wb_bf = w_branch.astype(BF16)
    wo_bf = w_out.astype(BF16)
    wq_bf = peer_wq.astype(BF16)

    ks, vs, srs, sgs = [], [], [], []
    for li in range(DEPTH):
        lw = dict(ret_decay=ret_decay[li], ret_norm=ret_norm[li], diff_lambda=diff_lambda[li],
                  diff_norm=diff_norm[li], gdn_conv=gdn_conv[li], gdn_A_log=gdn_A_log[li],
                  gdn_dt_bias=gdn_dt_bias[li], gdn_norm=gdn_norm[li])
        mod = mod_all[li, :N_COND]
        sh1, sc1, g1, sh2, sc2, g2 = [_per_block(m) for m in jnp.split(mod, 6, axis=-1)]
        h = _normmod(x, norm1_w[li], sc1, sh1, BF16)
        proj = _matmul(h, w_main[li], 512, 1024)
        gab = _matmul(h, w_ab[li], 512, 128)

        pc = proj[:N_CTX].reshape(BATCH, SEQ, N_MAIN)
        gc = gab[:N_CTX, :32].reshape(BATCH, SEQ, 32)
        oa, ob, oc, (k_l, v_l, sr_l, sg_l) = _mixers_jax(pc, gc, lw, li, None)
        ks.append(k_l); vs.append(v_l); srs.append(sr_l); sgs.append(sg_l)
        pz = proj[N_CTX:].reshape(DEC_BATCH, DEC_SEQ, N_MAIN)
        gz = gab[N_CTX:, :32].reshape(DEC_BATCH, DEC_SEQ, 32)
        ctx = {'k': cache_diff_k[:, li], 'v': cache_diff_v[:, li], 'ret': state_ret[:, li], 'gdn': state_gdn[:, li]}
        za, zb, zc, _ = _mixers_jax(pz, gz, lw, li, ctx)
        cat = lambda a, b: jnp.concatenate([a.reshape(N_CTX, BRANCH_W), b.reshape(N_LAT, BRANCH_W)], axis=0).astype(BF16)
        o_ret, o_diff, o_gdn = cat(oa, za), cat(ob, zb), cat(oc, zc)

        x, h2 = _merge(o_ret, o_diff, o_gdn, proj, wb_bf[li], wo_bf[li], x, g1, norm2_w[li], sc2, sh2)
        idx, gate = _peer_route(h2, wq_bf[li], peer_keys[li])
        x = _peer_experts(h2, idx, gate, peer_u[li].T.astype(BF16), peer_v[li].astype(BF16), x, g2)

    zeros_blk = jnp.zeros((N_MOD_BLK, 1, D_MODEL), F32)
    y = _normmod(x, norm_f_w, zeros_blk, zeros_blk, F32)
    y_prompt = y[:N_CTX].reshape(BATCH, SEQ, D_MODEL)
    y_sample = y[N_CTX:].reshape(DEC_BATCH, DEC_SEQ, D_MODEL)
    return (y_prompt, y_sample, jnp.stack(ks, axis=1), jnp.stack(vs, axis=1),
            jnp.stack(srs, axis=1), jnp.stack(sgs, axis=1))
```

```python
import functools
import math

import numpy as np
import jax
import jax.numpy as jnp
from jax import lax
from jax.experimental import pallas as pl
from jax.experimental.pallas import tpu as pltpu

F32 = jnp.float32
BF16 = jnp.bfloat16

D_MODEL = 1024
BATCH = 16
SEQ = 256
DEPTH = 2
DEC_BATCH = 2
DEC_SEQ = 1024
PAST_LEN = 256
GRID_W = 64
ROPE_BASE = 10000.0
EPS = 1e-6
CHUNK = 64
R_HEADS, R_DK, R_DV = 4, 128, 256
D_HEADS, D_HD = 8, 64
G_HEADS, G_DK, G_DV = 8, 128, 128
CONV_K = 7
G_QKV = G_HEADS * (2 * G_DK + G_DV)
N_BRANCH = 3
BRANCH_W = 1024
PEER_HEADS = 8
PEER_DQ = 256
N_KEYS = 128
PEER_TOPK = 16
TOK_BLOCK = 128

N_CTX = BATCH * SEQ
N_LAT = DEC_BATCH * DEC_SEQ
N_TOK = N_CTX + N_LAT
N_COND = 1 + DEC_BATCH
MOD_BLK = 256
N_MOD_BLK = N_TOK // MOD_BLK

C_RQ, C_RK, C_RV, C_RG = 0, 512, 1024, 2048
C_DQ, C_DK, C_DV = 3072, 4096, 5120
C_GQKV, C_GZ, C_MG = 6144, 9216, 10240
N_MAIN = 13312
AB_OFF = 10240

VMEM_LIMIT = 56 * 1024 * 1024


def _cparams(sem):
    return pltpu.CompilerParams(dimension_semantics=sem, vmem_limit_bytes=VMEM_LIMIT)


def _mod_kernel(c_ref, w_ref, b_ref, o_ref):
    c = c_ref[...]
    a = c * jax.nn.sigmoid(c)
    o_ref[0] = jnp.dot(a, w_ref[0], preferred_element_type=F32,
                       precision=lax.Precision.HIGHEST) + b_ref[0]


def _modulation(cond_pad, w_mod, b_mod):
    tn = 1536
    return pl.pallas_call(
        _mod_kernel,
        out_shape=jax.ShapeDtypeStruct((DEPTH, 8, 6 * D_MODEL), F32),
        grid=(DEPTH, 6 * D_MODEL // tn),
        in_specs=[pl.BlockSpec((8, D_MODEL), lambda l, j: (0, 0)),
                  pl.BlockSpec((1, D_MODEL, tn), lambda l, j: (l, 0, j)),
                  pl.BlockSpec((1, 1, tn), lambda l, j: (l, 0, j))],
        out_specs=pl.BlockSpec((1, 8, tn), lambda l, j: (l, 0, j)),
        compiler_params=_cparams(("arbitrary", "arbitrary")),
        name="modulation",
    )(cond_pad, w_mod, b_mod.reshape(DEPTH, 1, 6 * D_MODEL))


def _normmod_kernel(x_ref, w_ref, sc_ref, sh_ref, o_ref):
    x = x_ref[...]
    y = x * lax.rsqrt(jnp.mean(x * x, axis=-1, keepdims=True) + EPS)
    y = y * w_ref[...]
    o_ref[...] = (y * (1.0 + sc_ref[0]) + sh_ref[0]).astype(o_ref.dtype)


def _normmod(x, w, sc_blk, sh_blk, out_dtype):
    tm = MOD_BLK
    return pl.pallas_call(
        _normmod_kernel,
        out_shape=jax.ShapeDtypeStruct((N_TOK, D_MODEL), out_dtype),
        grid=(N_TOK // tm,),
        in_specs=[pl.BlockSpec((tm, D_MODEL), lambda i: (i, 0)),
                  pl.BlockSpec((1, D_MODEL), lambda i: (0, 0)),
                  pl.BlockSpec((1, 1, D_MODEL), lambda i: (i, 0, 0)),
                  pl.BlockSpec((1, 1, D_MODEL), lambda i: (i, 0, 0))],
        out_specs=pl.BlockSpec((tm, D_MODEL), lambda i: (i, 0)),
        compiler_params=_cparams(("arbitrary",)),
        name="normmod",
    )(x, w.reshape(1, D_MODEL), sc_blk, sh_blk)


def _matmul_kernel(a_ref, b_ref, o_ref):
    o_ref[...] = jnp.dot(a_ref[...], b_ref[...], preferred_element_type=F32)


def _matmul(a, b, tm, tn):
    m, k = a.shape
    n = b.shape[1]
    return pl.pallas_call(
        _matmul_kernel,
        out_shape=jax.ShapeDtypeStruct((m, n), F32),
        grid=(n // tn, m // tm),
        in_specs=[pl.BlockSpec((tm, k), lambda j, i: (i, 0)),
                  pl.BlockSpec((k, tn), lambda j, i: (0, j))],
        out_specs=pl.BlockSpec((tm, tn), lambda j, i: (i, j)),
        compiler_params=_cparams(("arbitrary", "arbitrary")),
        name="matmul",
    )(a, b)


def _merge_kernel(oa_ref, ob_ref, oc_ref, mga_ref, mgb_ref, mgc_ref, wb_ref, wo_ref, x_ref, g1_ref,
                  n2_ref, sc_ref, sh_ref, xo_ref, h_ref):
    mix = None
    for n, (o_ref, mg_ref) in enumerate(((oa_ref, mga_ref), (ob_ref, mgb_ref), (oc_ref, mgc_ref))):
        merged = jnp.dot(o_ref[...], wb_ref[n], preferred_element_type=F32)
        gate = jax.nn.sigmoid(mg_ref[...])
        mix = gate * merged if mix is None else mix + gate * merged
    m = jnp.dot(mix.astype(BF16), wo_ref[...], preferred_element_type=F32)
    x = x_ref[...] + g1_ref[0] * m
    xo_ref[...] = x
    y = x * lax.rsqrt(jnp.mean(x * x, axis=-1, keepdims=True) + EPS)
    y = y * n2_ref[...]
    h_ref[...] = (y * (1.0 + sc_ref[0]) + sh_ref[0]).astype(h_ref.dtype)


def _merge(o_ret, o_diff, o_gdn, proj, wb, wo, x, g1_blk, n2w, sc2_blk, sh2_blk):
    tm = MOD_BLK
    tok = lambda i: (i, 0)
    blk = lambda i: (i, 0, 0)
    return pl.pallas_call(
        _merge_kernel,
        out_shape=(jax.ShapeDtypeStruct((N_TOK, D_MODEL), F32),
                   jax.ShapeDtypeStruct((N_TOK, D_MODEL), BF16)),
        grid=(N_TOK // tm,),
        in_specs=[pl.BlockSpec((tm, BRANCH_W), tok),
                  pl.BlockSpec((tm, BRANCH_W), tok),
                  pl.BlockSpec((tm, BRANCH_W), tok),
                  pl.BlockSpec((tm, D_MODEL), lambda i: (i, C_MG // D_MODEL)),
                  pl.BlockSpec((tm, D_MODEL), lambda i: (i, C_MG // D_MODEL + 1)),
                  pl.BlockSpec((tm, D_MODEL), lambda i: (i, C_MG // D_MODEL + 2)),
                  pl.BlockSpec((N_BRANCH, BRANCH_W, D_MODEL), lambda i: (0, 0, 0)),
                  pl.BlockSpec((D_MODEL, D_MODEL), lambda i: (0, 0)),
                  pl.BlockSpec((tm, D_MODEL), tok),
                  pl.BlockSpec((1, 1, D_MODEL), blk),
                  pl.BlockSpec((1, D_MODEL), lambda i: (0, 0)),
                  pl.BlockSpec((1, 1, D_MODEL), blk),
                  pl.BlockSpec((1, 1, D_MODEL), blk)],
        out_specs=(pl.BlockSpec((tm, D_MODEL), tok), pl.BlockSpec((tm, D_MODEL), tok)),
        compiler_params=_cparams(("arbitrary",)),
        name="merge",
    )(o_ret, o_diff, o_gdn, proj, proj, proj, wb, wo, x, g1_blk, n2w.reshape(1, D_MODEL), sc2_blk, sh2_blk)


ROUTE_T = 128
N_PAIR = PEER_HEADS * PEER_TOPK
NEG_INF = float("-inf")


def _split_bf16(x):
    hi = x.astype(BF16)
    return hi, (x - hi.astype(F32)).astype(BF16)


def _topk_rows(s, rows, n_rows, payload=None):
    vals, outs = [], []
    for _ in range(PEER_TOPK):
        m = jnp.max(s, axis=0, keepdims=True)
        pos = jnp.min(jnp.where(s == m, rows, n_rows), axis=0, keepdims=True)
        sel = rows == pos
        vals.append(m)
        if payload is None:
            outs.append(pos)
        else:
            outs.append(jnp.sum(jnp.where(sel, payload, 0), axis=0, keepdims=True))
        s = jnp.where(sel, NEG_INF, s)
    return jnp.concatenate(vals, axis=0), jnp.concatenate(outs, axis=0)


def _route_kernel(h_ref, wq_ref, khi_ref, klo_ref, idx_ref, gate_ref, q_scr, idx_scr, gate_scr):
    t = h_ref.shape[0]
    q = jnp.dot(h_ref[...], wq_ref[...], preferred_element_type=F32)
    for hp in range(2 * PEER_HEADS):
        q_scr[hp] = q[:, hp * 128:(hp + 1) * 128]
    rows_k = lax.broadcasted_iota(jnp.int32, (N_KEYS, t), 0)
    rows_c = lax.broadcasted_iota(jnp.int32, (PEER_TOPK * PEER_TOPK, t), 0)
    nt = (((1,), (1,)), ((), ()))

    def head(h, carry):
        tops = []
        for p in range(2):
            qhi, qlo = _split_bf16(q_scr[2 * h + p])
            s = (lax.dot_general(khi_ref[p], qhi, nt, preferred_element_type=F32)
                 + lax.dot_general(khi_ref[p], qlo, nt, preferred_element_type=F32)
                 + lax.dot_general(klo_ref[p], qhi, nt, preferred_element_type=F32))
            tops.append(_topk_rows(s, rows_k, N_KEYS))
        (v1, i1), (v2, i2) = tops
        cand = jnp.concatenate([v1[k:k + 1] + v2 for k in range(PEER_TOPK)], axis=0)
        cand_i = jnp.concatenate([i1[k:k + 1] * N_KEYS + i2 for k in range(PEER_TOPK)], axis=0)
        best, bidx = _topk_rows(cand, rows_c, PEER_TOPK * PEER_TOPK, payload=cand_i)
        e = jnp.exp(best - best[0:1])
        gate_scr[h] = e / jnp.sum(e, axis=0, keepdims=True)
        idx_scr[h] = bidx
        return carry

    lax.fori_loop(0, PEER_HEADS, head, 0)
    idx_ref[...] = idx_scr[...].reshape(N_PAIR, t).T
    gate_ref[...] = gate_scr[...].reshape(N_PAIR, t).T


def _peer_route(h2, wq, keys):
    khi, klo = _split_bf16(keys)
    t = ROUTE_T
    n_tok = h2.shape[0]
    return pl.pallas_call(
        _route_kernel,
        out_shape=(jax.ShapeDtypeStruct((n_tok, N_PAIR), jnp.int32),
                   jax.ShapeDtypeStruct((n_tok, N_PAIR), F32)),
        grid=(n_tok // t,),
        in_specs=[pl.BlockSpec((t, D_MODEL), lambda i: (i, 0)),
                  pl.BlockSpec((D_MODEL, PEER_HEADS * PEER_DQ), lambda i: (0, 0)),
                  pl.BlockSpec((2, N_KEYS, PEER_DQ // 2), lambda i: (0, 0, 0)),
                  pl.BlockSpec((2, N_KEYS, PEER_DQ // 2), lambda i: (0, 0, 0))],
        out_specs=(pl.BlockSpec((t, N_PAIR), lambda i: (i, 0)),
                   pl.BlockSpec((t, N_PAIR), lambda i: (i, 0))),
        scratch_shapes=[pltpu.VMEM((2 * PEER_HEADS, t, PEER_DQ // 2), F32),
                        pltpu.VMEM((PEER_HEADS, PEER_TOPK, t), jnp.int32),
                        pltpu.VMEM((PEER_HEADS, PEER_TOPK, t), F32)],
        compiler_params=_cparams(("arbitrary",)),
        name="peer_route",
    )(h2, wq, khi, klo)


EXP_TB = 256
EXP_EB = 512
N_EXPERTS = N_KEYS * N_KEYS


def _expert_kernel(h_ref, idx_ref, gate_ref, ut_ref, v_ref, x_ref, g2_ref, o_ref, g_scr, acc_ref):
    j = pl.program_id(1)
    tb = h_ref.shape[0]
    nt = (((1,), (1,)), ((), ()))

    @pl.when(j == 0)
    def _():
        acc_ref[...] = jnp.zeros_like(acc_ref)
        sub = lax.broadcasted_iota(jnp.int32, (N_KEYS, N_PAIR), 0)

        def body(t, carry):
            e = idx_ref[pl.ds(t, 1), :]
            g = gate_ref[pl.ds(t, 1), :]
            ghi = g.astype(BF16).astype(F32)
            glo = g - ghi
            xa = jnp.where(sub == (e >> 7), 1.0, 0.0).astype(BF16)
            mb = sub == (e & (N_KEYS - 1))
            yhi = jnp.where(mb, ghi, 0.0).astype(BF16)
            ylo = jnp.where(mb, glo, 0.0).astype(BF16)
            gt = lax.dot_general(jnp.concatenate([xa, xa], axis=1), jnp.concatenate([yhi, ylo], axis=1), nt,
                                 preferred_element_type=F32)
            g_scr[pl.ds(pl.multiple_of(t * N_KEYS, N_KEYS), N_KEYS), :] = gt
            return carry

        lax.fori_loop(0, tb, body, 0)

    s = jnp.dot(h_ref[...], ut_ref[...], preferred_element_type=F32)
    nb = EXP_EB // N_KEYS
    gj = jnp.concatenate([g_scr[pl.ds(j * nb + c, tb, stride=N_KEYS), :] for c in range(nb)], axis=1)
    w = 0.5 * s * (1.0 + lax.erf(s * (2.0 ** -0.5))) * gj
    acc_ref[...] += jnp.dot(w.astype(BF16), v_ref[...], preferred_element_type=F32)

    @pl.when(j == pl.num_programs(1) - 1)
    def _():
        o_ref[...] = x_ref[...] + g2_ref[0] * acc_ref[...]


def _peer_experts(h2, idx, gate, u_t, v, x, g2_blk):
    tb, eb = EXP_TB, EXP_EB
    n_tok = h2.shape[0]
    return pl.pallas_call(
        _expert_kernel,
        out_shape=jax.ShapeDtypeStruct((n_tok, D_MODEL), F32),
        grid=(n_tok // tb, N_EXPERTS // eb),
        in_specs=[pl.BlockSpec((tb, D_MODEL), lambda i, j: (i, 0)),
                  pl.BlockSpec((tb, N_PAIR), lambda i, j: (i, 0)),
                  pl.BlockSpec((tb, N_PAIR), lambda i, j: (i, 0)),
                  pl.BlockSpec((D_MODEL, eb), lambda i, j: (0, j)),
                  pl.BlockSpec((eb, D_MODEL), lambda i, j: (j, 0)),
                  pl.BlockSpec((tb, D_MODEL), lambda i, j: (i, 0)),
                  pl.BlockSpec((1, 1, D_MODEL), lambda i, j: (i * tb // MOD_BLK, 0, 0))],
        out_specs=pl.BlockSpec((tb, D_MODEL), lambda i, j: (i, 0)),
        scratch_shapes=[pltpu.VMEM((tb * N_KEYS, N_KEYS), F32),
                        pltpu.VMEM((tb, D_MODEL), F32)],
        compiler_params=_cparams(("arbitrary", "arbitrary")),
        name="peer_experts",
    )(h2, idx, gate, u_t, v, x, g2_blk)


CONV_ROWS = 256
CONV_HALO = 8


def _gdn_prep_kernel(x_ref, w_ref, o_ref):
    sec = pl.program_id(1)
    L = x_ref.shape[0]
    half = CONV_K // 2
    for r0 in range(0, L, CONV_ROWS):
        s0, s1 = max(r0 - CONV_HALO, 0), min(r0 + CONV_ROWS + CONV_HALO, L)
        n = s1 - s0
        t_idx = lax.broadcasted_iota(jnp.int32, (n, 128), 0) + s0
        for cb in range(G_HEADS):
            cols = slice(cb * 128, (cb + 1) * 128)
            x = x_ref[s0:s1, cols]
            w = w_ref[0, :, cols]
            acc = x * w[half:half + 1]
            for d in range(-half, half + 1):
                if d == 0:
                    continue
                xs = pltpu.roll(x, (-d) % n, axis=0)
                valid = (t_idx + d >= 0) if d < 0 else (t_idx + d < L)
                acc = acc + jnp.where(valid, xs, 0.0) * w[half + d:half + d + 1]
            y = acc[r0 - s0:r0 - s0 + CONV_ROWS]
            y = y * jax.nn.sigmoid(y)
            inv = lax.rsqrt(jnp.sum(y * y, axis=-1, keepdims=True) + EPS)
            o_ref[r0:r0 + CONV_ROWS, cols] = y * jnp.where(sec < 2, inv, 1.0)


def _gdn_prep(proj, conv_w, seq_len, n_seq, tok_off):
    blk0 = tok_off // seq_len
    sec0 = C_GQKV // 1024
    return pl.pallas_call(
        _gdn_prep_kernel,
        out_shape=jax.ShapeDtypeStruct((n_seq * seq_len, G_QKV), F32),
        grid=(n_seq, 3),
        in_specs=[pl.BlockSpec((seq_len, 1024), lambda s, c: (blk0 + s, sec0 + c)),
                  pl.BlockSpec((1, CONV_K, 1024), lambda s, c: (0, 0, c))],
        out_specs=pl.BlockSpec((seq_len, 1024), lambda s, c: (s, c)),
        compiler_params=_cparams(("arbitrary", "arbitrary")),
        name="gdn_prep",
    )(proj, conv_w.reshape(1, CONV_K, G_QKV))


def _dot(a, b):
    return jnp.dot(a, b, preferred_element_type=F32)


def _dot_nt(a, b):
    return lax.dot_general(a, b, (((1,), (1,)), ((), ())), preferred_element_type=F32)


def _dot_tn(a, b):
    return lax.dot_general(a, b, (((0,), (0,)), ((), ())), preferred_element_type=F32)


def _dot3(a, b):
    ah, al = _split_bf16(a)
    bh, bl = _split_bf16(b)
    return _dot(ah, bh) + _dot(ah, bl) + _dot(al, bh)


def _mask_dot3(m, b):
    m16 = m.astype(BF16)
    bh = b.astype(BF16)
    r1 = b - bh.astype(F32)
    bm = r1.astype(BF16)
    bl = (r1 - bm.astype(F32)).astype(BF16)
    return _dot(m16, bh) + _dot(m16, bm) + _dot(m16, bl)


def _gdn_kernel(*refs, seq_len, has_init, emit_state):
    it = iter(refs)
    q_ref, k_ref, v_ref, z_ref, ga_ref, gb_ref, alog_ref, dtb_ref, nw_ref = [next(it) for _ in range(9)]
    s0_ref = next(it) if has_init else None
    o_ref = next(it)
    so_ref = next(it) if emit_state else None
    u_s, w_s, qg_s, kd_s, qk_s, gl_s, o_s = [next(it) for _ in range(7)]
    C = CHUNK
    nc = seq_len // C
    ri = lax.broadcasted_iota(jnp.int32, (C, C), 0)
    ci = lax.broadcasted_iota(jnp.int32, (C, C), 1)
    eye = jnp.where(ri == ci, 1.0, 0.0)
    ones = jnp.ones((C, C), F32)
    scale = G_DK ** -0.5
    incl = (ri >= ci, ri <= ci)
    strict = (ri > ci, ri < ci)
    levels = []
    for lv in range(6):
        b = 1 << lv
        same = (ri >> (lv + 1)) == (ci >> (lv + 1))
        r_hi, c_hi = (ri & b) != 0, (ci & b) != 0
        levels.append((same & r_hi & jnp.logical_not(c_hi), same & jnp.logical_not(r_hi) & c_hi))

    GROUP = 4

    def prep(grp, carry):
        chunks = [grp * GROUP + j for j in range(GROUP)]
        rows = [pl.ds(pl.multiple_of(c * C, C), C) for c in chunks]
        q = [q_ref[r, :] * scale for r in rows]
        k = [k_ref[r, :] for r in rows]
        v = [v_ref[r, :] for r in rows]
        k16 = [x.astype(BF16) for x in k]
        kk = [_dot_nt(x, x) for x in k16]
        qk = [_dot_nt(a.astype(BF16), b) for a, b in zip(q, k16)]
        probs = [(j, d) for j in range(GROUP) for d in range(2)]
        g = [-jnp.exp(alog_ref[d]) * jax.nn.softplus(ga_ref[d, rows[j], :] + dtb_ref[d]) for j, d in probs]
        beta = [jax.nn.sigmoid(gb_ref[d, rows[j], :]) for j, d in probs]
        tri = [jnp.where(m, 1.0, 0.0) for m in incl]
        gcol = [_mask_dot3(tri[d], jnp.broadcast_to(g[p], (C, 128))) for p, (j, d) in enumerate(probs)]
        grow = [_mask_dot3(ones, jnp.where(incl[1 - d], jnp.broadcast_to(g[p], (C, C)), 0.0))
                for p, (j, d) in enumerate(probs)]
        dec = [jnp.exp(jnp.where(incl[d], gcol[p][:, :C] - grow[p], NEG_INF)) for p, (j, d) in enumerate(probs)]
        a = [jnp.where(strict[d], kk[j] * beta[p] * dec[p], 0.0) for p, (j, d) in enumerate(probs)]
        t = [eye - jnp.where(levels[0][d], a[p], 0.0) for p, (j, d) in enumerate(probs)]
        for lv in range(1, 6):
            ct = [_dot3(jnp.where(levels[lv][d], a[p], 0.0), t[p]) for p, (j, d) in enumerate(probs)]
            t = [t[p] - _dot3(t[p], ct[p]) for p in range(len(probs))]
        for p, (j, d) in enumerate(probs):
            eg = jnp.exp(gcol[p])
            t16 = t[p].astype(BF16)
            glast = gcol[p][C - 1:C, :] if d == 0 else gcol[p][0:1, :]
            u_s[d, rows[j], :] = _dot(t16, (v[j] * beta[p]).astype(BF16))
            w_s[d, rows[j], :] = _dot(t16, (k[j] * beta[p] * eg).astype(BF16))
            qg_s[d, rows[j], :] = q[j] * eg
            kd_s[d, rows[j], :] = k[j] * jnp.exp(glast - gcol[p])
            qk_s[d, rows[j], :] = qk[j] * dec[p]
            gl_s[d * nc + chunks[j]] = jnp.broadcast_to(jnp.exp(glast), (8, 128))
        return carry

    lax.fori_loop(0, nc // GROUP, prep, 0)
    o_s[...] = jnp.zeros_like(o_s)

    def step(i, states):
        cs = (i, nc - 1 - i)
        rows = [pl.ds(pl.multiple_of(c * C, C), C) for c in cs]
        s16 = [s.astype(BF16) for s in states]
        ws = [_dot(w_s[d, rows[d], :].astype(BF16), s16[d]) for d in range(2)]
        un16 = [(u_s[d, rows[d], :] - ws[d]).astype(BF16) for d in range(2)]
        o = [_dot(qg_s[d, rows[d], :].astype(BF16), s16[d]) + _dot(qk_s[d, rows[d], :].astype(BF16), un16[d])
             for d in range(2)]
        new = tuple(states[d] * gl_s[d * nc + cs[d]][0:1, :] + _dot_tn(kd_s[d, rows[d], :].astype(BF16), un16[d])
                    for d in range(2))
        for d in range(2):
            o_s[rows[d], :] += o[d]
        return new

    if has_init:
        init = (s0_ref[0, 0, 0], s0_ref[0, 1, 0])
    else:
        init = (jnp.zeros((G_DK, G_DV), F32),) * 2
    final = lax.fori_loop(0, nc, step, init)
    if emit_state:
        so_ref[0, 0, 0] = final[0]
        so_ref[0, 1, 0] = final[1]
    o = o_s[...]
    y = o * lax.rsqrt(jnp.mean(o * o, axis=-1, keepdims=True) + EPS) * nw_ref[...]
    z = z_ref[...]
    o_ref[...] = (y * (z * jax.nn.sigmoid(z))).astype(o_ref.dtype)


def _gdn(qkv, proj, gcol, alog, dtb, nw, s0, seq_len, n_seq, tok_off):
    has_init = s0 is not None
    emit_state = not has_init
    blk0 = tok_off // seq_len
    tokcol = lambda off: pl.BlockSpec((seq_len, 128), lambda s, h: (s, off + h))
    in_specs = [tokcol(0), tokcol(G_HEADS), tokcol(2 * G_HEADS),
                pl.BlockSpec((seq_len, 128), lambda s, h: (blk0 + s, C_GZ // 128 + h)),
                pl.BlockSpec((2, seq_len, 1), lambda s, h: (h, blk0 + s, 0)),
                pl.BlockSpec((2, seq_len, 1), lambda s, h: (G_HEADS + h, blk0 + s, 0)),
                pl.BlockSpec((2, 1, 1), lambda s, h: (h, 0, 0)),
                pl.BlockSpec((2, 1, 1), lambda s, h: (h, 0, 0)),
                pl.BlockSpec((1, G_DV), lambda s, h: (0, 0))]
    args = [qkv, qkv, qkv, proj, gcol, gcol, alog, dtb, nw.reshape(1, G_DV)]
    if has_init:
        in_specs.append(pl.BlockSpec((1, 2, 1, G_DK, G_DV), lambda s, h: (s, 0, h, 0, 0)))
        args.append(s0)
    out_shape = [jax.ShapeDtypeStruct((n_seq * seq_len, BRANCH_W), BF16)]
    out_specs = [pl.BlockSpec((seq_len, 128), lambda s, h: (s, h))]
    if emit_state:
        out_shape.append(jax.ShapeDtypeStruct((n_seq, 2, G_HEADS, G_DK, G_DV), F32))
        out_specs.append(pl.BlockSpec((1, 2, 1, G_DK, G_DV), lambda s, h: (s, 0, h, 0, 0)))
    nc = seq_len // CHUNK
    res = pl.pallas_call(
        functools.partial(_gdn_kernel, seq_len=seq_len, has_init=has_init, emit_state=emit_state),
        out_shape=tuple(out_shape),
        grid=(n_seq, G_HEADS),
        in_specs=in_specs,
        out_specs=tuple(out_specs),
        scratch_shapes=[pltpu.VMEM((2, seq_len, 128), F32)] * 4
                       + [pltpu.VMEM((2, seq_len, CHUNK), F32), pltpu.VMEM((2 * nc, 8, 128), F32),
                          pltpu.VMEM((seq_len, 128), F32)],
        compiler_params=_cparams(("arbitrary", "arbitrary")),
        name="gdn",
    )(*args)
    return res if emit_state else (res[0], None)


def _rmsnorm(x, w):
    return x * lax.rsqrt(jnp.mean(x * x, axis=-1, keepdims=True) + EPS) * w


def _l2norm(x):
    return x * lax.rsqrt(jnp.sum(x * x, axis=-1, keepdims=True) + EPS)


def _flip(x):
    return jnp.flip(x, axis=1)


def _axial_rope(x):
    L, dim = x.shape[1], x.shape[-1]
    n_rows = L // GRID_W
    row = jnp.repeat(jnp.arange(n_rows), GRID_W).astype(F32)
    col = jnp.tile(jnp.arange(GRID_W), n_rows).astype(F32)
    nf = dim // 4
    inv = jnp.power(ROPE_BASE, -jnp.arange(nf, dtype=F32) / nf)
    ang_r = row[:, None] * inv
    ang_c = col[:, None] * inv
    ang = jnp.concatenate([ang_r, ang_r, ang_c, ang_c], axis=-1)
    shape = (1, L) + (1,) * (x.ndim - 3) + (dim,)
    cos = jnp.cos(ang).reshape(shape)
    sin = jnp.sin(ang).reshape(shape)
    x_r1, x_r2, x_c1, x_c2 = jnp.split(x, 4, axis=-1)
    rot = jnp.concatenate([-x_r2, x_r1, -x_c2, x_c1], axis=-1)
    return x * cos + rot * sin


def _to_chunks(x):
    B, L, H = x.shape[:3]
    x = x.reshape((B, L // CHUNK, CHUNK, H) + x.shape[3:])
    return jnp.moveaxis(x, 3, 1)


def _from_chunks(x):
    x = jnp.moveaxis(x, 1, 3)
    B, N, C, H = x.shape[:4]
    return x.reshape((B, N * C, H) + x.shape[4:])


def _retention_scan(q, k, v, log_gamma, s0):
    q = _to_chunks(q * R_DK ** -0.5)
    k = _to_chunks(k)
    v = _to_chunks(v)
    pos = jnp.arange(CHUNK, dtype=F32)
    lg = log_gamma[:, None]
    dist = pos[:, None] - pos[None, :]
    intra = jnp.exp(jnp.where(dist >= 0, lg[:, :, None] * dist, -jnp.inf))
    q_dec = jnp.exp(lg * (pos + 1.0))[:, :, None]
    k_dec = jnp.exp(lg * (CHUNK - 1.0 - pos))[:, :, None]
    c_dec = jnp.exp(lg * CHUNK)[:, :, None]

    def step(s, xs):
        qc, kc, vc = xs
        a = jnp.einsum('bhqd,bhkd->bhqk', qc, kc) * intra
        o = jnp.einsum('bhqk,bhkv->bhqv', a, vc) + jnp.einsum('bhqd,bhdv->bhqv', qc, s) * q_dec
        s = s * c_dec + jnp.einsum('bhkd,bhkv->bhdv', kc * k_dec, vc)
        return s, o

    xs = (jnp.moveaxis(q, 2, 0), jnp.moveaxis(k, 2, 0), jnp.moveaxis(v, 2, 0))
    s, o = lax.scan(step, s0, xs)
    return _from_chunks(jnp.moveaxis(o, 0, 2)), s


def _gdn_scan(q, k, v, g, beta, s0):
    q = _to_chunks(q * G_DK ** -0.5)
    k = _to_chunks(k)
    v = _to_chunks(v)
    g = jnp.cumsum(_to_chunks(g), axis=-1)
    beta = _to_chunks(beta)
    incl = jnp.tril(jnp.ones((CHUNK, CHUNK), dtype=bool))
    strict = jnp.tril(jnp.ones((CHUNK, CHUNK), dtype=bool), -1)
    decay = jnp.exp(jnp.where(incl, g[..., :, None] - g[..., None, :], -jnp.inf))
    kb = k * beta[..., None]
    a = jnp.where(strict, jnp.einsum('bhnid,bhnjd->bhnij', kb, k) * decay, 0.0)
    eye = jnp.eye(CHUNK, dtype=F32)
    t = lax.linalg.triangular_solve(a + eye, jnp.broadcast_to(eye, a.shape), left_side=True, lower=True)
    u = t @ (v * beta[..., None])
    w = t @ (kb * jnp.exp(g)[..., None])
    qk = jnp.einsum('bhnid,bhnjd->bhnij', q, k) * decay
    qg = q * jnp.exp(g)[..., None]
    kd = k * jnp.exp(g[..., -1:] - g)[..., None]
    gl = jnp.exp(g[..., -1])[..., None, None]

    def step(s, xs):
        qg_c, kd_c, u_c, w_c, qk_c, gl_c = xs
        u_new = u_c - w_c @ s
        o = qg_c @ s + qk_c @ u_new
        s = s * gl_c + jnp.swapaxes(kd_c, -1, -2) @ u_new
        return s, o

    xs = tuple(jnp.moveaxis(arr, 2, 0) for arr in (qg, kd, u, w, qk, gl))
    s, o = lax.scan(step, s0, xs)
    return _from_chunks(jnp.moveaxis(o, 0, 2)), s


def _diff_attention(q, k, v, lam):
    scale = D_HD ** -0.5
    s = jnp.einsum('bqhpd,bkhpd->bhpqk', q, k) * scale
    p = jax.nn.softmax(s, axis=-1)
    wts = p[:, :, 0] - lam * p[:, :, 1]
    return jnp.einsum('bhqk,bkhe->bqhe', wts, v)


def _centred_dwconv(x, w):
    pad = (w.shape[0] - 1) // 2
    return lax.conv_general_dilated(x, w[:, None, :], window_strides=(1,), padding=[(pad, pad)],
                                    dimension_numbers=('NWC', 'WIO', 'NWC'),
                                    feature_group_count=x.shape[-1])


def _mixers_jax(proj, gab, lw, li, ctx, with_gdn=True):
    B, L, _ = proj.shape
    latent = ctx is not None
    r_q = proj[..., C_RQ:C_RK].reshape(B, L, R_HEADS, R_DK)
    r_k = proj[..., C_RK:C_RV].reshape(B, L, R_HEADS, R_DK)
    r_v = proj[..., C_RV:C_RG].reshape(B, L, R_HEADS, R_DV)
    r_g = proj[..., C_RG:C_DQ]
    if latent:
        r_q, r_k = _axial_rope(r_q), _axial_rope(r_k)
        r0_f, r0_b = ctx['ret'][:, 0], ctx['ret'][:, 1]
    else:
        r0_f = r0_b = jnp.zeros((B, R_HEADS, R_DK, R_DV), F32)
    log_gamma = jax.nn.log_sigmoid(lw['ret_decay'])
    or_f, sr_f = _retention_scan(r_q, r_k, r_v, log_gamma[0], r0_f)
    or_b, sr_b = _retention_scan(_flip(r_q), _flip(r_k), _flip(r_v), log_gamma[1], r0_b)
    o_ret = _rmsnorm(or_f + _flip(or_b), lw['ret_norm']).reshape(B, L, BRANCH_W) * jax.nn.silu(r_g)

    d_q = proj[..., C_DQ:C_DK].reshape(B, L, D_HEADS, 2, D_HD)
    d_k = proj[..., C_DK:C_DV].reshape(B, L, D_HEADS, 2, D_HD)
    d_v = proj[..., C_DV:C_GQKV].reshape(B, L, D_HEADS, 2 * D_HD)
    if latent:
        q_att = _axial_rope(d_q)
        k_all = jnp.concatenate([_axial_rope(d_k), ctx['k']], axis=1)
        v_all = jnp.concatenate([d_v, ctx['v']], axis=1)
    else:
        q_att, k_all, v_all = d_q, d_k, d_v
    lam_init = 0.8 - 0.6 * math.exp(-0.3 * li)
    lp = lw['diff_lambda']
    lam = jnp.exp(jnp.sum(lp[0] * lp[1])) - jnp.exp(jnp.sum(lp[2] * lp[3])) + lam_init
    o_diff = _diff_attention(q_att, k_all, v_all, lam)
    o_diff = (_rmsnorm(o_diff, lw['diff_norm']) * (1.0 - lam_init)).reshape(B, L, BRANCH_W)
    if not with_gdn:
        return o_ret, o_diff, None, (d_k, d_v, jnp.stack([sr_f, sr_b], axis=1), None)

    g_qkv = jax.nn.silu(_centred_dwconv(proj[..., C_GQKV:C_GZ], lw['gdn_conv']))
    g_q, g_k, g_v = jnp.split(g_qkv, [G_HEADS * G_DK, 2 * G_HEADS * G_DK], axis=-1)
    g_q = _l2norm(g_q.reshape(B, L, G_HEADS, G_DK))
    g_k = _l2norm(g_k.reshape(B, L, G_HEADS, G_DK))
    g_v = g_v.reshape(B, L, G_HEADS, G_DV)
    g_z = proj[..., C_GZ:C_MG]
    g_a = gab[..., :16].reshape(B, L, 2, G_HEADS)
    g_b = gab[..., 16:32].reshape(B, L, 2, G_HEADS)
    g_log = -jnp.exp(lw['gdn_A_log']) * jax.nn.softplus(g_a + lw['gdn_dt_bias'])
    g_beta = jax.nn.sigmoid(g_b)
    if latent:
        g0_f, g0_b = ctx['gdn'][:, 0], ctx['gdn'][:, 1]
    else:
        g0_f = g0_b = jnp.zeros((B, G_HEADS, G_DK, G_DV), F32)
    og_f, sg_f = _gdn_scan(g_q, g_k, g_v, g_log[:, :, 0], g_beta[:, :, 0], g0_f)
    og_b, sg_b = _gdn_scan(_flip(g_q), _flip(g_k), _flip(g_v), _flip(g_log[:, :, 1]), _flip(g_beta[:, :, 1]), g0_b)
    o_gdn = _rmsnorm(og_f + _flip(og_b), lw['gdn_norm']) * jax.nn.silu(g_z.reshape(B, L, G_HEADS, G_DV))
    o_gdn = o_gdn.reshape(B, L, BRANCH_W)
    state = (d_k, d_v, jnp.stack([sr_f, sr_b], axis=1), jnp.stack([sg_f, sg_b], axis=1))
    return o_ret, o_diff, o_gdn, state


def _peer_jax(h, w_q, sub_keys, exp_u, exp_v):
    hb = h.reshape(N_TOK // TOK_BLOCK, TOK_BLOCK, D_MODEL)

    def block(hx):
        q = jnp.dot(hx, w_q, preferred_element_type=F32).reshape(TOK_BLOCK, PEER_HEADS, 2, PEER_DQ // 2)
        s = jnp.einsum('thpd,pnd->thpn', q, sub_keys)
        s_top, i_top = lax.top_k(s, PEER_TOPK)
        cand_s = (s_top[:, :, 0, :, None] + s_top[:, :, 1, None, :]).reshape(TOK_BLOCK, PEER_HEADS, -1)
        cand_i = (i_top[:, :, 0, :, None] * N_KEYS + i_top[:, :, 1, None, :]).reshape(TOK_BLOCK, PEER_HEADS, -1)
        best_s, pos = lax.top_k(cand_s, PEER_TOPK)
        idx = jnp.take_along_axis(cand_i, pos, axis=-1)
        gate = jax.nn.softmax(best_s, axis=-1)
        hf = hx.astype(F32)
        act = jax.nn.gelu(jnp.einsum('thkd,td->thk', exp_u[idx], hf), approximate=False)
        return jnp.einsum('thk,thkd->td', gate * act, exp_v[idx])

    return lax.map(block, hb).reshape(N_TOK, D_MODEL)


def _per_block(rows):
    idx = np.concatenate([np.zeros(N_CTX // MOD_BLK, np.int32)] +
                         [np.full(DEC_SEQ // MOD_BLK, 1 + b, np.int32) for b in range(DEC_BATCH)])
    return rows[idx][:, None, :]


def kernel(x_prompt, x_sample, cache_diff_k, cache_diff_v, state_ret, state_gdn, c, c_ctx,
           norm1_w, norm2_w, w_mod, b_mod, w_in, ret_decay, ret_norm, diff_lambda, diff_norm,
           gdn_conv, gdn_A_log, gdn_dt_bias, gdn_norm, w_branch, w_out,
           peer_wq, peer_keys, peer_u, peer_v, norm_f_w):
    x = jnp.concatenate([x_prompt.reshape(N_CTX, D_MODEL), x_sample.reshape(N_LAT, D_MODEL)], axis=0)
    cond = jnp.concatenate([c_ctx[None, :], c, jnp.zeros((8 - N_COND, D_MODEL), F32)], axis=0)
    mod_all = _modulation(cond, w_mod, b_mod)

    w_main = jnp.concatenate([w_in[:, :, :AB_OFF], w_in[:, :, AB_OFF + 32:]], axis=-1).astype(BF16)
    w_ab = jnp.pad(w_in[:, :, AB_OFF:AB_OFF + 32], ((0, 0), (0, 0), (0, 96))).astype(BF16)
    wb_bf = w_branch.astype(BF16)
    wo_bf = w_out.astype(BF16)
    wq_bf = peer_wq.astype(BF16)

    ks, vs, srs, sgs = [], [], [], []
    for li in range(DEPTH):
        lw = dict(ret_decay=ret_decay[li], ret_norm=ret_norm[li], diff_lambda=diff_lambda[li],
                  diff_norm=diff_norm[li], gdn_conv=gdn_conv[li], gdn_A_log=gdn_A_log[li],
                  gdn_dt_bias=gdn_dt_bias[li], gdn_norm=gdn_norm[li])
        mod = mod_all[li, :N_COND]
        sh1, sc1, g1, sh2, sc2, g2 = [_per_block(m) for m in jnp.split(mod, 6, axis=-1)]
        h = _normmod(x, norm1_w[li], sc1, sh1, BF16)
        proj = _matmul(h, w_main[li], 512, 1024)
        gab = _matmul(h, w_ab[li], 512, 128)

        pc = proj[:N_CTX].reshape(BATCH, SEQ, N_MAIN)
        gc = gab[:N_CTX, :32].reshape(BATCH, SEQ, 32)
        oa, ob, _, (k_l, v_l, sr_l, _) = _mixers_jax(pc, gc, lw, li, None, with_gdn=False)
        pz = proj[N_CTX:].reshape(DEC_BATCH, DEC_SEQ, N_MAIN)
        gz = gab[N_CTX:, :32].reshape(DEC_BATCH, DEC_SEQ, 32)
        ctx = {'k': cache_diff_k[:, li], 'v': cache_diff_v[:, li], 'ret': state_ret[:, li], 'gdn': state_gdn[:, li]}
        za, zb, _, _ = _mixers_jax(pz, gz, lw, li, ctx, with_gdn=False)

        gcol = gab[:, :32].reshape(N_TOK, 2, 2, G_HEADS).transpose(1, 3, 2, 0).reshape(32, N_TOK, 1)
        alog = gdn_A_log[li].T.reshape(2 * G_HEADS, 1, 1)
        dtb = gdn_dt_bias[li].T.reshape(2 * G_HEADS, 1, 1)
        qkv_c = _gdn_prep(proj, gdn_conv[li], SEQ, BATCH, 0)
        oc, sg_l = _gdn(qkv_c, proj, gcol, alog, dtb, gdn_norm[li], None, SEQ, BATCH, 0)
        qkv_z = _gdn_prep(proj, gdn_conv[li], DEC_SEQ, DEC_BATCH, N_CTX)
        zc, _ = _gdn(qkv_z, proj, gcol, alog, dtb, gdn_norm[li], state_gdn[:, li], DEC_SEQ, DEC_BATCH, N_CTX)
        ks.append(k_l); vs.append(v_l); srs.append(sr_l); sgs.append(sg_l)
        cat = lambda a, b: jnp.concatenate([a.reshape(N_CTX, BRANCH_W), b.reshape(N_LAT, BRANCH_W)], axis=0).astype(BF16)
        o_ret, o_diff, o_gdn = cat(oa, za), cat(ob, zb), jnp.concatenate([oc, zc], axis=0)

        x, h2 = _merge(o_ret, o_diff, o_gdn, proj, wb_bf[li], wo_bf[li], x, g1, norm2_w[li], sc2, sh2)
        idx, gate = _peer_route(h2, wq_bf[li], peer_keys[li])
        x = _peer_experts(h2, idx, gate, peer_u[li].T.astype(BF16), peer_v[li].astype(BF16), x, g2)

    zeros_blk = jnp.zeros((N_MOD_BLK, 1, D_MODEL), F32)
    y = _normmod(x, norm_f_w, zeros_blk, zeros_blk, F32)
    y_prompt = y[:N_CTX].reshape(BATCH, SEQ, D_MODEL)
    y_sample = y[N_CTX:].reshape(DEC_BATCH, DEC_SEQ, D_MODEL)
    return (y_prompt, y_sample, jnp.stack(ks, axis=1), jnp.stack(vs, axis=1),
            jnp.stack(srs, axis=1), jnp.stack(sgs, axis=1))
```

```python
import functools
import math

import numpy as np
import jax
import jax.numpy as jnp
from jax import lax
from jax.experimental import pallas as pl
from jax.experimental.pallas import tpu as pltpu

F32 = jnp.float32
BF16 = jnp.bfloat16

D_MODEL = 1024
BATCH = 16
SEQ = 256
DEPTH = 2
DEC_BATCH = 2
DEC_SEQ = 1024
PAST_LEN = 256
GRID_W = 64
ROPE_BASE = 10000.0
EPS = 1e-6
CHUNK = 64
R_HEADS, R_DK, R_DV = 4, 128, 256
D_HEADS, D_HD = 8, 64
G_HEADS, G_DK, G_DV = 8, 128, 128
CONV_K = 7
G_QKV = G_HEADS * (2 * G_DK + G_DV)
N_BRANCH = 3
BRANCH_W = 1024
PEER_HEADS = 8
PEER_DQ = 256
N_KEYS = 128
PEER_TOPK = 16
TOK_BLOCK = 128

N_CTX = BATCH * SEQ
N_LAT = DEC_BATCH * DEC_SEQ
N_TOK = N_CTX + N_LAT
N_COND = 1 + DEC_BATCH
MOD_BLK = 256
N_MOD_BLK = N_TOK // MOD_BLK

C_RQ, C_RK, C_RV, C_RG = 0, 512, 1024, 2048
C_DQ, C_DK, C_DV = 3072, 4096, 5120
C_GQKV, C_GZ, C_MG = 6144, 9216, 10240
N_MAIN = 13312
AB_OFF = 10240

VMEM_LIMIT = 56 * 1024 * 1024


def _cparams(sem):
    return pltpu.CompilerParams(dimension_semantics=sem, vmem_limit_bytes=VMEM_LIMIT)


def _mod_kernel(c_ref, w_ref, b_ref, o_ref):
    c = c_ref[...]
    a = c * jax.nn.sigmoid(c)
    o_ref[0] = jnp.dot(a, w_ref[0], preferred_element_type=F32,
                       precision=lax.Precision.HIGHEST) + b_ref[0]


def _modulation(cond_pad, w_mod, b_mod):
    tn = 1536
    return pl.pallas_call(
        _mod_kernel,
        out_shape=jax.ShapeDtypeStruct((DEPTH, 8, 6 * D_MODEL), F32),
        grid=(DEPTH, 6 * D_MODEL // tn),
        in_specs=[pl.BlockSpec((8, D_MODEL), lambda l, j: (0, 0)),
                  pl.BlockSpec((1, D_MODEL, tn), lambda l, j: (l, 0, j)),
                  pl.BlockSpec((1, 1, tn), lambda l, j: (l, 0, j))],
        out_specs=pl.BlockSpec((1, 8, tn), lambda l, j: (l, 0, j)),
        compiler_params=_cparams(("arbitrary", "arbitrary")),
        name="modulation",
    )(cond_pad, w_mod, b_mod.reshape(DEPTH, 1, 6 * D_MODEL))


def _normmod_kernel(x_ref, w_ref, sc_ref, sh_ref, o_ref):
    x = x_ref[...]
    y = x * lax.rsqrt(jnp.mean(x * x, axis=-1, keepdims=True) + EPS)
    y = y * w_ref[...]
    o_ref[...] = (y * (1.0 + sc_ref[0]) + sh_ref[0]).astype(o_ref.dtype)


def _normmod(x, w, sc_blk, sh_blk, out_dtype):
    tm = MOD_BLK
    return pl.pallas_call(
        _normmod_kernel,
        out_shape=jax.ShapeDtypeStruct((N_TOK, D_MODEL), out_dtype),
        grid=(N_TOK // tm,),
        in_specs=[pl.BlockSpec((tm, D_MODEL), lambda i: (i, 0)),
                  pl.BlockSpec((1, D_MODEL), lambda i: (0, 0)),
                  pl.BlockSpec((1, 1, D_MODEL), lambda i: (i, 0, 0)),
                  pl.BlockSpec((1, 1, D_MODEL), lambda i: (i, 0, 0))],
        out_specs=pl.BlockSpec((tm, D_MODEL), lambda i: (i, 0)),
        compiler_params=_cparams(("arbitrary",)),
        name="normmod",
    )(x, w.reshape(1, D_MODEL), sc_blk, sh_blk)


def _matmul_kernel(a_ref, b_ref, o_ref):
    o_ref[...] = jnp.dot(a_ref[...], b_ref[...], preferred_element_type=F32)


def _matmul(a, b, tm, tn):
    m, k = a.shape
    n = b.shape[1]
    return pl.pallas_call(
        _matmul_kernel,
        out_shape=jax.ShapeDtypeStruct((m, n), F32),
        grid=(n // tn, m // tm),
        in_specs=[pl.BlockSpec((tm, k), lambda j, i: (i, 0)),
                  pl.BlockSpec((k, tn), lambda j, i: (0, j))],
        out_specs=pl.BlockSpec((tm, tn), lambda j, i: (i, j)),
        compiler_params=_cparams(("arbitrary", "arbitrary")),
        name="matmul",
    )(a, b)


def _merge_kernel(oa_ref, ob_ref, oc_ref, mga_ref, mgb_ref, mgc_ref, wb_ref, wo_ref, x_ref, g1_ref,
                  n2_ref, sc_ref, sh_ref, xo_ref, h_ref):
    mix = None
    for n, (o_ref, mg_ref) in enumerate(((oa_ref, mga_ref), (ob_ref, mgb_ref), (oc_ref, mgc_ref))):
        merged = jnp.dot(o_ref[...], wb_ref[n], preferred_element_type=F32)
        gate = jax.nn.sigmoid(mg_ref[...])
        mix = gate * merged if mix is None else mix + gate * merged
    m = jnp.dot(mix.astype(BF16), wo_ref[...], preferred_element_type=F32)
    x = x_ref[...] + g1_ref[0] * m
    xo_ref[...] = x
    y = x * lax.rsqrt(jnp.mean(x * x, axis=-1, keepdims=True) + EPS)
    y = y * n2_ref[...]
    h_ref[...] = (y * (1.0 + sc_ref[0]) + sh_ref[0]).astype(h_ref.dtype)


def _merge(o_ret, o_diff, o_gdn, proj, wb, wo, x, g1_blk, n2w, sc2_blk, sh2_blk):
    tm = MOD_BLK
    tok = lambda i: (i, 0)
    blk = lambda i: (i, 0, 0)
    return pl.pallas_call(
        _merge_kernel,
        out_shape=(jax.ShapeDtypeStruct((N_TOK, D_MODEL), F32),
                   jax.ShapeDtypeStruct((N_TOK, D_MODEL), BF16)),
        grid=(N_TOK // tm,),
        in_specs=[pl.BlockSpec((tm, BRANCH_W), tok),
                  pl.BlockSpec((tm, BRANCH_W), tok),
                  pl.BlockSpec((tm, BRANCH_W), tok),
                  pl.BlockSpec((tm, D_MODEL), lambda i: (i, C_MG // D_MODEL)),
                  pl.BlockSpec((tm, D_MODEL), lambda i: (i, C_MG // D_MODEL + 1)),
                  pl.BlockSpec((tm, D_MODEL), lambda i: (i, C_MG // D_MODEL + 2)),
                  pl.BlockSpec((N_BRANCH, BRANCH_W, D_MODEL), lambda i: (0, 0, 0)),
                  pl.BlockSpec((D_MODEL, D_MODEL), lambda i: (0, 0)),
                  pl.BlockSpec((tm, D_MODEL), tok),
                  pl.BlockSpec((1, 1, D_MODEL), blk),
                  pl.BlockSpec((1, D_MODEL), lambda i: (0, 0)),
                  pl.BlockSpec((1, 1, D_MODEL), blk),
                  pl.BlockSpec((1, 1, D_MODEL), blk)],
        out_specs=(pl.BlockSpec((tm, D_MODEL), tok), pl.BlockSpec((tm, D_MODEL), tok)),
        compiler_params=_cparams(("arbitrary",)),
        name="merge",
    )(o_ret, o_diff, o_gdn, proj, proj, proj, wb, wo, x, g1_blk, n2w.reshape(1, D_MODEL), sc2_blk, sh2_blk)


ROUTE_T = 128
N_PAIR = PEER_HEADS * PEER_TOPK
NEG_INF = float("-inf")


def _split_bf16(x):
    hi = x.astype(BF16)
    return hi, (x - hi.astype(F32)).astype(BF16)


def _topk_rows(s, rows, n_rows, payload=None):
    vals, outs = [], []
    for _ in range(PEER_TOPK):
        m = jnp.max(s, axis=0, keepdims=True)
        pos = jnp.min(jnp.where(s == m, rows, n_rows), axis=0, keepdims=True)
        sel = rows == pos
        vals.append(m)
        if payload is None:
            outs.append(pos)
        else:
            outs.append(jnp.sum(jnp.where(sel, payload, 0), axis=0, keepdims=True))
        s = jnp.where(sel, NEG_INF, s)
    return jnp.concatenate(vals, axis=0), jnp.concatenate(outs, axis=0)


def _route_kernel(h_ref, wq_ref, khi_ref, klo_ref, idx_ref, gate_ref, q_scr, idx_scr, gate_scr):
    t = h_ref.shape[0]
    q = jnp.dot(h_ref[...], wq_ref[...], preferred_element_type=F32)
    for hp in range(2 * PEER_HEADS):
        q_scr[hp] = q[:, hp * 128:(hp + 1) * 128]
    rows_k = lax.broadcasted_iota(jnp.int32, (N_KEYS, t), 0)
    rows_c = lax.broadcasted_iota(jnp.int32, (PEER_TOPK * PEER_TOPK, t), 0)
    nt = (((1,), (1,)), ((), ()))

    def head(h, carry):
        tops = []
        for p in range(2):
            qhi, qlo = _split_bf16(q_scr[2 * h + p])
            s = (lax.dot_general(khi_ref[p], qhi, nt, preferred_element_type=F32)
                 + lax.dot_general(khi_ref[p], qlo, nt, preferred_element_type=F32)
                 + lax.dot_general(klo_ref[p], qhi, nt, preferred_element_type=F32))
            tops.append(_topk_rows(s, rows_k, N_KEYS))
        (v1, i1), (v2, i2) = tops
        cand = jnp.concatenate([v1[k:k + 1] + v2 for k in range(PEER_TOPK)], axis=0)
        cand_i = jnp.concatenate([i1[k:k + 1] * N_KEYS + i2 for k in range(PEER_TOPK)], axis=0)
        best, bidx = _topk_rows(cand, rows_c, PEER_TOPK * PEER_TOPK, payload=cand_i)
        e = jnp.exp(best - best[0:1])
        gate_scr[h] = e / jnp.sum(e, axis=0, keepdims=True)
        idx_scr[h] = bidx
        return carry

    lax.fori_loop(0, PEER_HEADS, head, 0)
    idx_ref[...] = idx_scr[...].reshape(N_PAIR, t).T
    gate_ref[...] = gate_scr[...].reshape(N_PAIR, t).T


def _peer_route(h2, wq, keys):
    khi, klo = _split_bf16(keys)
    t = ROUTE_T
    n_tok = h2.shape[0]
    return pl.pallas_call(
        _route_kernel,
        out_shape=(jax.ShapeDtypeStruct((n_tok, N_PAIR), jnp.int32),
                   jax.ShapeDtypeStruct((n_tok, N_PAIR), F32)),
        grid=(n_tok // t,),
        in_specs=[pl.BlockSpec((t, D_MODEL), lambda i: (i, 0)),
                  pl.BlockSpec((D_MODEL, PEER_HEADS * PEER_DQ), lambda i: (0, 0)),
                  pl.BlockSpec((2, N_KEYS, PEER_DQ // 2), lambda i: (0, 0, 0)),
                  pl.BlockSpec((2, N_KEYS, PEER_DQ // 2), lambda i: (0, 0, 0))],
        out_specs=(pl.BlockSpec((t, N_PAIR), lambda i: (i, 0)),
                   pl.BlockSpec((t, N_PAIR), lambda i: (i, 0))),
        scratch_shapes=[pltpu.VMEM((2 * PEER_HEADS, t, PEER_DQ // 2), F32),
                        pltpu.VMEM((PEER_HEADS, PEER_TOPK, t), jnp.int32),
                        pltpu.VMEM((PEER_HEADS, PEER_TOPK, t), F32)],
        compiler_params=_cparams(("arbitrary",)),
        name="peer_route",
    )(h2, wq, khi, klo)


EXP_TB = 256
EXP_EB = 512
N_EXPERTS = N_KEYS * N_KEYS


def _expert_kernel(h_ref, idx_ref, gate_ref, ut_ref, v_ref, x_ref, g2_ref, o_ref, g_scr, acc_ref):
    j = pl.program_id(1)
    tb = h_ref.shape[0]
    nt = (((1,), (1,)), ((), ()))

    @pl.when(j == 0)
    def _():
        acc_ref[...] = jnp.zeros_like(acc_ref)
        sub = lax.broadcasted_iota(jnp.int32, (N_KEYS, N_PAIR), 0)

        def body(t, carry):
            e = idx_ref[pl.ds(t, 1), :]
            g = gate_ref[pl.ds(t, 1), :]
            ghi = g.astype(BF16).astype(F32)
            glo = g - ghi
            xa = jnp.where(sub == (e >> 7), 1.0, 0.0).astype(BF16)
            mb = sub == (e & (N_KEYS - 1))
            yhi = jnp.where(mb, ghi, 0.0).astype(BF16)
            ylo = jnp.where(mb, glo, 0.0).astype(BF16)
            gt = lax.dot_general(jnp.concatenate([xa, xa], axis=1), jnp.concatenate([yhi, ylo], axis=1), nt,
                                 preferred_element_type=F32)
            g_scr[pl.ds(pl.multiple_of(t * N_KEYS, N_KEYS), N_KEYS), :] = gt
            return carry

        lax.fori_loop(0, tb, body, 0)

    s = jnp.dot(h_ref[...], ut_ref[...], preferred_element_type=F32)
    nb = EXP_EB // N_KEYS
    gj = jnp.concatenate([g_scr[pl.ds(j * nb + c, tb, stride=N_KEYS), :] for c in range(nb)], axis=1)
    w = 0.5 * s * (1.0 + lax.erf(s * (2.0 ** -0.5))) * gj
    acc_ref[...] += jnp.dot(w.astype(BF16), v_ref[...], preferred_element_type=F32)

    @pl.when(j == pl.num_programs(1) - 1)
    def _():
        o_ref[...] = x_ref[...] + g2_ref[0] * acc_ref[...]


def _peer_experts(h2, idx, gate, u_t, v, x, g2_blk):
    tb, eb = EXP_TB, EXP_EB
    n_tok = h2.shape[0]
    return pl.pallas_call(
        _expert_kernel,
        out_shape=jax.ShapeDtypeStruct((n_tok, D_MODEL), F32),
        grid=(n_tok // tb, N_EXPERTS // eb),
        in_specs=[pl.BlockSpec((tb, D_MODEL), lambda i, j: (i, 0)),
                  pl.BlockSpec((tb, N_PAIR), lambda i, j: (i, 0)),
                  pl.BlockSpec((tb, N_PAIR), lambda i, j: (i, 0)),
                  pl.BlockSpec((D_MODEL, eb), lambda i, j: (0, j)),
                  pl.BlockSpec((eb, D_MODEL), lambda i, j: (j, 0)),
                  pl.BlockSpec((tb, D_MODEL), lambda i, j: (i, 0)),
                  pl.BlockSpec((1, 1, D_MODEL), lambda i, j: (i * tb // MOD_BLK, 0, 0))],
        out_specs=pl.BlockSpec((tb, D_MODEL), lambda i, j: (i, 0)),
        scratch_shapes=[pltpu.VMEM((tb * N_KEYS, N_KEYS), F32),
                        pltpu.VMEM((tb, D_MODEL), F32)],
        compiler_params=_cparams(("arbitrary", "arbitrary")),
        name="peer_experts",
    )(h2, idx, gate, u_t, v, x, g2_blk)


CONV_ROWS = 256
CONV_HALO = 8


def _gdn_prep_kernel(x_ref, w_ref, o_ref):
    sec = pl.program_id(1)
    L = x_ref.shape[0]
    half = CONV_K // 2
    for r0 in range(0, L, CONV_ROWS):
        s0, s1 = max(r0 - CONV_HALO, 0), min(r0 + CONV_ROWS + CONV_HALO, L)
        n = s1 - s0
        t_idx = lax.broadcasted_iota(jnp.int32, (n, 128), 0) + s0
        for cb in range(G_HEADS):
            cols = slice(cb * 128, (cb + 1) * 128)
            x = x_ref[s0:s1, cols]
            w = w_ref[0, :, cols]
            acc = x * w[half:half + 1]
            for d in range(-half, half + 1):
                if d == 0:
                    continue
                xs = pltpu.roll(x, (-d) % n, axis=0)
                valid = (t_idx + d >= 0) if d < 0 else (t_idx + d < L)
                acc = acc + jnp.where(valid, xs, 0.0) * w[half + d:half + d + 1]
            y = acc[r0 - s0:r0 - s0 + CONV_ROWS]
            y = y * jax.nn.sigmoid(y)
            inv = lax.rsqrt(jnp.sum(y * y, axis=-1, keepdims=True) + EPS)
            o_ref[r0:r0 + CONV_ROWS, cols] = y * jnp.where(sec < 2, inv, 1.0)


def _gdn_prep(proj, conv_w, seq_len, n_seq, tok_off):
    blk0 = tok_off // seq_len
    sec0 = C_GQKV // 1024
    return pl.pallas_call(
        _gdn_prep_kernel,
        out_shape=jax.ShapeDtypeStruct((n_seq * seq_len, G_QKV), F32),
        grid=(n_seq, 3),
        in_specs=[pl.BlockSpec((seq_len, 1024), lambda s, c: (blk0 + s, sec0 + c)),
                  pl.BlockSpec((1, CONV_K, 1024), lambda s, c: (0, 0, c))],
        out_specs=pl.BlockSpec((seq_len, 1024), lambda s, c: (s, c)),
        compiler_params=_cparams(("arbitrary", "arbitrary")),
        name="gdn_prep",
    )(proj, conv_w.reshape(1, CONV_K, G_QKV))


def _dot(a, b):
    return jnp.dot(a, b, preferred_element_type=F32)


def _dot_nt(a, b):
    return lax.dot_general(a, b, (((1,), (1,)), ((), ())), preferred_element_type=F32)


def _dot_tn(a, b):
    return lax.dot_general(a, b, (((0,), (0,)), ((), ())), preferred_element_type=F32)


def _dot3(a, b):
    ah, al = _split_bf16(a)
    bh, bl = _split_bf16(b)
    return _dot(ah, bh) + _dot(ah, bl) + _dot(al, bh)


def _mask_dot3(m, b):
    m16 = m.astype(BF16)
    bh = b.astype(BF16)
    r1 = b - bh.astype(F32)
    bm = r1.astype(BF16)
    bl = (r1 - bm.astype(F32)).astype(BF16)
    return _dot(m16, bh) + _dot(m16, bm) + _dot(m16, bl)


def _gdn_kernel(*refs, seq_len, has_init, emit_state):
    it = iter(refs)
    q_ref, k_ref, v_ref, z_ref, ga_ref, gb_ref, alog_ref, dtb_ref, nw_ref = [next(it) for _ in range(9)]
    s0_ref = next(it) if has_init else None
    o_ref = next(it)
    so_ref = next(it) if emit_state else None
    u_s, w_s, qg_s, kd_s, qk_s, gl_s, o_s = [next(it) for _ in range(7)]
    C = CHUNK
    nc = seq_len // C
    ri = lax.broadcasted_iota(jnp.int32, (C, C), 0)
    ci = lax.broadcasted_iota(jnp.int32, (C, C), 1)
    eye = jnp.where(ri == ci, 1.0, 0.0)
    ones = jnp.ones((C, C), F32)
    scale = G_DK ** -0.5
    incl = (ri >= ci, ri <= ci)
    strict = (ri > ci, ri < ci)
    levels = []
    for lv in range(6):
        b = 1 << lv
        same = (ri >> (lv + 1)) == (ci >> (lv + 1))
        r_hi, c_hi = (ri & b) != 0, (ci & b) != 0
        levels.append((same & r_hi & jnp.logical_not(c_hi), same & jnp.logical_not(r_hi) & c_hi))

    GROUP = 4

    def prep(grp, carry):
        chunks = [grp * GROUP + j for j in range(GROUP)]
        rows = [pl.ds(pl.multiple_of(c * C, C), C) for c in chunks]
        q = [q_ref[r, :] * scale for r in rows]
        k = [k_ref[r, :] for r in rows]
        v = [v_ref[r, :] for r in rows]
        k16 = [x.astype(BF16) for x in k]
        kk = [_dot_nt(x, x) for x in k16]
        qk = [_dot_nt(a.astype(BF16), b) for a, b in zip(q, k16)]
        probs = [(j, d) for j in range(GROUP) for d in range(2)]
        g = [-jnp.exp(alog_ref[d]) * jax.nn.softplus(ga_ref[d, rows[j], :] + dtb_ref[d]) for j, d in probs]
        beta = [jax.nn.sigmoid(gb_ref[d, rows[j], :]) for j, d in probs]
        tri = [jnp.where(m, 1.0, 0.0) for m in incl]
        gcol = [_mask_dot3(tri[d], jnp.broadcast_to(g[p], (C, 128))) for p, (j, d) in enumerate(probs)]
        grow = [_mask_dot3(ones, jnp.where(incl[1 - d], jnp.broadcast_to(g[p], (C, C)), 0.0))
                for p, (j, d) in enumerate(probs)]
        dec = [jnp.exp(jnp.where(incl[d], gcol[p][:, :C] - grow[p], NEG_INF)) for p, (j, d) in enumerate(probs)]
        a = [jnp.where(strict[d], kk[j] * beta[p] * dec[p], 0.0) for p, (j, d) in enumerate(probs)]
        t = [eye - jnp.where(levels[0][d], a[p], 0.0) for p, (j, d) in enumerate(probs)]
        for lv in range(1, 6):
            ct = [_dot3(jnp.where(levels[lv][d], a[p], 0.0), t[p]) for p, (j, d) in enumerate(probs)]
            t = [t[p] - _dot3(t[p], ct[p]) for p in range(len(probs))]
        for p, (j, d) in enumerate(probs):
            eg = jnp.exp(gcol[p])
            t16 = t[p].astype(BF16)
            glast = gcol[p][C - 1:C, :] if d == 0 else gcol[p][0:1, :]
            u_s[d, rows[j], :] = _dot(t16, (v[j] * beta[p]).astype(BF16))
            w_s[d, rows[j], :] = _dot(t16, (k[j] * beta[p] * eg).astype(BF16))
            qg_s[d, rows[j], :] = q[j] * eg
            kd_s[d, rows[j], :] = k[j] * jnp.exp(glast - gcol[p])
            qk_s[d, rows[j], :] = qk[j] * dec[p]
            gl_s[d * nc + chunks[j]] = jnp.broadcast_to(jnp.exp(glast), (8, 128))
        return carry

    lax.fori_loop(0, nc // GROUP, prep, 0)
    o_s[...] = jnp.zeros_like(o_s)

    def step(i, states):
        cs = (i, nc - 1 - i)
        rows = [pl.ds(pl.multiple_of(c * C, C), C) for c in cs]
        s16 = [s.astype(BF16) for s in states]
        ws = [_dot(w_s[d, rows[d], :].astype(BF16), s16[d]) for d in range(2)]
        un16 = [(u_s[d, rows[d], :] - ws[d]).astype(BF16) for d in range(2)]
        o = [_dot(qg_s[d, rows[d], :].astype(BF16), s16[d]) + _dot(qk_s[d, rows[d], :].astype(BF16), un16[d])
             for d in range(2)]
        new = tuple(states[d] * gl_s[d * nc + cs[d]][0:1, :] + _dot_tn(kd_s[d, rows[d], :].astype(BF16), un16[d])
                    for d in range(2))
        for d in range(2):
            o_s[rows[d], :] += o[d]
        return new

    if has_init:
        init = (s0_ref[0, 0, 0], s0_ref[0, 1, 0])
    else:
        init = (jnp.zeros((G_DK, G_DV), F32),) * 2
    final = lax.fori_loop(0, nc, step, init)
    if emit_state:
        so_ref[0, 0, 0] = final[0]
        so_ref[0, 1, 0] = final[1]
    o = o_s[...]
    y = o * lax.rsqrt(jnp.mean(o * o, axis=-1, keepdims=True) + EPS) * nw_ref[...]
    z = z_ref[...]
    o_ref[...] = (y * (z * jax.nn.sigmoid(z))).astype(o_ref.dtype)


def _gdn(qkv, proj, gcol, alog, dtb, nw, s0, seq_len, n_seq, tok_off):
    has_init = s0 is not None
    emit_state = not has_init
    blk0 = tok_off // seq_len
    tokcol = lambda off: pl.BlockSpec((seq_len, 128), lambda s, h: (s, off + h))
    in_specs = [tokcol(0), tokcol(G_HEADS), tokcol(2 * G_HEADS),
                pl.BlockSpec((seq_len, 128), lambda s, h: (blk0 + s, C_GZ // 128 + h)),
                pl.BlockSpec((2, seq_len, 1), lambda s, h: (h, blk0 + s, 0)),
                pl.BlockSpec((2, seq_len, 1), lambda s, h: (G_HEADS + h, blk0 + s, 0)),
                pl.BlockSpec((2, 1, 1), lambda s, h: (h, 0, 0)),
                pl.BlockSpec((2, 1, 1), lambda s, h: (h, 0, 0)),
                pl.BlockSpec((1, G_DV), lambda s, h: (0, 0))]
    args = [qkv, qkv, qkv, proj, gcol, gcol, alog, dtb, nw.reshape(1, G_DV)]
    if has_init:
        in_specs.append(pl.BlockSpec((1, 2, 1, G_DK, G_DV), lambda s, h: (s, 0, h, 0, 0)))
        args.append(s0)
    out_shape = [jax.ShapeDtypeStruct((n_seq * seq_len, BRANCH_W), BF16)]
    out_specs = [pl.BlockSpec((seq_len, 128), lambda s, h: (s, h))]
    if emit_state:
        out_shape.append(jax.ShapeDtypeStruct((n_seq, 2, G_HEADS, G_DK, G_DV), F32))
        out_specs.append(pl.BlockSpec((1, 2, 1, G_DK, G_DV), lambda s, h: (s, 0, h, 0, 0)))
    nc = seq_len // CHUNK
    res = pl.pallas_call(
        functools.partial(_gdn_kernel, seq_len=seq_len, has_init=has_init, emit_state=emit_state),
        out_shape=tuple(out_shape),
        grid=(n_seq, G_HEADS),
        in_specs=in_specs,
        out_specs=tuple(out_specs),
        scratch_shapes=[pltpu.VMEM((2, seq_len, 128), F32)] * 4
                       + [pltpu.VMEM((2, seq_len, CHUNK), F32), pltpu.VMEM((2 * nc, 8, 128), F32),
                          pltpu.VMEM((seq_len, 128), F32)],
        compiler_params=_cparams(("arbitrary", "arbitrary")),
        name="gdn",
    )(*args)
    return res if emit_state else (res[0], None)


def _rope_tables(seq_len, dim):
    n_rows = seq_len // GRID_W
    row = np.repeat(np.arange(n_rows), GRID_W).astype(np.float32)
    col = np.tile(np.arange(GRID_W), n_rows).astype(np.float32)
    nf = dim // 4
    inv = np.power(np.float32(ROPE_BASE), -np.arange(nf, dtype=np.float32) / np.float32(nf)).astype(np.float32)
    ang_r = row[:, None] * inv
    ang_c = col[:, None] * inv
    ang = np.concatenate([ang_r, ang_r, ang_c, ang_c], axis=-1)
    ang = np.tile(ang, (1, 128 // dim))
    return jnp.asarray(np.cos(ang), F32), jnp.asarray(np.sin(ang), F32)


def _rope(x, cos, sin, dim):
    quarter = dim // 4
    lane = lax.broadcasted_iota(jnp.int32, x.shape, 1)
    first = (lane & (2 * quarter - 1)) < quarter
    rot = jnp.where(first, -pltpu.roll(x, 128 - quarter, axis=1), pltpu.roll(x, quarter, axis=1))
    return x * cos + rot * sin


Q_TILE = 256


def _retention_kernel(*refs, seq_len, latent):
    it = iter(refs)
    q_ref, k_ref, v_ref, rg_ref, lg_ref, nw_ref = [next(it) for _ in range(6)]
    if latent:
        cos_ref, sin_ref, s0_ref = next(it), next(it), next(it)
    o_ref = next(it)
    so_ref = None if latent else next(it)
    L = seq_len
    lg = lg_ref[0]
    lg = jnp.minimum(lg, 0.0) - jnp.log(1.0 + jnp.exp(-jnp.abs(lg)))
    lgf, lgb = lg[0:1, :], lg[1:2, :]
    k = k_ref[...]
    if latent:
        k = _rope(k, cos_ref[...], sin_ref[...], R_DK)
    k16 = k.astype(BF16)
    v16 = v_ref[...].astype(BF16)
    for qt in range(L // Q_TILE):
        rows = slice(qt * Q_TILE, (qt + 1) * Q_TILE)
        q = q_ref[rows, :]
        if latent:
            q = _rope(q, cos_ref[rows, :], sin_ref[rows, :], R_DK)
        q16 = (q * R_DK ** -0.5).astype(BF16)
        a = _dot_nt(q16, k16)
        i = lax.broadcasted_iota(jnp.int32, (Q_TILE, L), 0) + qt * Q_TILE
        j = lax.broadcasted_iota(jnp.int32, (Q_TILE, L), 1)
        dist = (i - j).astype(F32)
        dec = (jnp.exp(jnp.where(dist >= 0, lgf * dist, NEG_INF))
               + jnp.exp(jnp.where(dist <= 0, -lgb * dist, NEG_INF)))
        o = _dot((a * dec).astype(BF16), v16)
        if latent:
            pos = (lax.broadcasted_iota(jnp.int32, (Q_TILE, 1), 0) + qt * Q_TILE).astype(F32)
            o = o + _dot(q16, s0_ref[0, 0, 0].astype(BF16)) * jnp.exp(lgf * (pos + 1.0))
            o = o + _dot(q16, s0_ref[0, 1, 0].astype(BF16)) * jnp.exp(lgb * (L - pos))
        y = o * lax.rsqrt(jnp.mean(o * o, axis=-1, keepdims=True) + EPS) * nw_ref[...]
        g = rg_ref[rows, :]
        o_ref[rows, :] = (y * (g * jax.nn.sigmoid(g))).astype(o_ref.dtype)
    if not latent:
        pos = lax.broadcasted_iota(jnp.int32, (L, 1), 0).astype(F32)
        so_ref[0, 0, 0] = _dot_tn((k * jnp.exp(lgf * (L - 1.0 - pos))).astype(BF16), v16)
        so_ref[0, 1, 0] = _dot_tn((k * jnp.exp(lgb * pos)).astype(BF16), v16)


def _retention(proj, decay, nw, seq_len, n_seq, tok_off, s0=None):
    latent = s0 is not None
    blk0 = tok_off // seq_len
    in_specs = [pl.BlockSpec((seq_len, R_DK), lambda s, h: (blk0 + s, C_RQ // R_DK + h)),
                pl.BlockSpec((seq_len, R_DK), lambda s, h: (blk0 + s, C_RK // R_DK + h)),
                pl.BlockSpec((seq_len, R_DV), lambda s, h: (blk0 + s, C_RV // R_DV + h)),
                pl.BlockSpec((seq_len, R_DV), lambda s, h: (blk0 + s, C_RG // R_DV + h)),
                pl.BlockSpec((1, 2, 1), lambda s, h: (h, 0, 0)),
                pl.BlockSpec((1, R_DV), lambda s, h: (0, 0))]
    args = [proj, proj, proj, proj, decay.T.reshape(R_HEADS, 2, 1), nw.reshape(1, R_DV)]
    out_shape = [jax.ShapeDtypeStruct((n_seq * seq_len, BRANCH_W), BF16)]
    out_specs = [pl.BlockSpec((seq_len, R_DV), lambda s, h: (s, h))]
    if latent:
        cos, sin = _rope_tables(seq_len, R_DK)
        in_specs += [pl.BlockSpec((seq_len, 128), lambda s, h: (0, 0)),
                     pl.BlockSpec((seq_len, 128), lambda s, h: (0, 0)),
                     pl.BlockSpec((1, 2, 1, R_DK, R_DV), lambda s, h: (s, 0, h, 0, 0))]
        args += [cos, sin, s0]
    else:
        out_shape.append(jax.ShapeDtypeStruct((n_seq, 2, R_HEADS, R_DK, R_DV), F32))
        out_specs.append(pl.BlockSpec((1, 2, 1, R_DK, R_DV), lambda s, h: (s, 0, h, 0, 0)))
    res = pl.pallas_call(
        functools.partial(_retention_kernel, seq_len=seq_len, latent=latent),
        out_shape=tuple(out_shape),
        grid=(n_seq, R_HEADS),
        in_specs=in_specs,
        out_specs=tuple(out_specs),
        compiler_params=_cparams(("arbitrary", "arbitrary")),
        name="retention",
    )(*args)
    return (res[0], None) if latent else res


def _diff_attn_kernel(*refs, seq_len, latent, lam_init):
    it = iter(refs)
    q_ref, k_ref, v_ref, lp_ref, nw_ref = [next(it) for _ in range(5)]
    if latent:
        cos_ref, sin_ref, ck_ref, cv_ref = [next(it) for _ in range(4)]
    o_ref = next(it)
    L = seq_len
    lp = lp_ref[...]
    lam = (jnp.exp(jnp.sum(lp[0:1] * lp[1:2], axis=-1, keepdims=True))
           - jnp.exp(jnp.sum(lp[2:3] * lp[3:4], axis=-1, keepdims=True)) + lam_init)
    k = k_ref[...]
    if latent:
        k = _rope(k, cos_ref[...], sin_ref[...], D_HD)
    keys = [k.astype(BF16)]
    vals = [v_ref[...].astype(BF16)]
    if latent:
        keys.append(ck_ref[0, 0].astype(BF16))
        vals.append(cv_ref[0, 0].astype(BF16))
    lane = lax.broadcasted_iota(jnp.int32, (Q_TILE, 2 * D_HD), 1)
    for qt in range(L // Q_TILE):
        rows = slice(qt * Q_TILE, (qt + 1) * Q_TILE)
        q = q_ref[rows, :]
        if latent:
            q = _rope(q, cos_ref[rows, :], sin_ref[rows, :], D_HD)
        q = q * D_HD ** -0.5
        outs = []
        for p in range(2):
            qp = jnp.where((lane >= D_HD) == (p == 1), q, 0.0).astype(BF16)
            s = [_dot_nt(qp, kk) for kk in keys]
            m = s[0].max(axis=-1, keepdims=True)
            for x in s[1:]:
                m = jnp.maximum(m, x.max(axis=-1, keepdims=True))
            e = [jnp.exp(x - m) for x in s]
            z = sum(x.sum(axis=-1, keepdims=True) for x in e)
            pv = sum(_dot(x.astype(BF16), vv) for x, vv in zip(e, vals))
            outs.append(pv / z)
        o = outs[0] - lam * outs[1]
        y = o * lax.rsqrt(jnp.mean(o * o, axis=-1, keepdims=True) + EPS) * nw_ref[...]
        o_ref[rows, :] = (y * (1.0 - lam_init)).astype(o_ref.dtype)


def _diff_attn(proj, lam_p, nw, li, seq_len, n_seq, tok_off, cache_k=None, cache_v=None):
    latent = cache_k is not None
    blk0 = tok_off // seq_len
    w = 2 * D_HD
    in_specs = [pl.BlockSpec((seq_len, w), lambda s, h: (blk0 + s, C_DQ // w + h)),
                pl.BlockSpec((seq_len, w), lambda s, h: (blk0 + s, C_DK // w + h)),
                pl.BlockSpec((seq_len, w), lambda s, h: (blk0 + s, C_DV // w + h)),
                pl.BlockSpec((4, D_HD), lambda s, h: (0, 0)),
                pl.BlockSpec((1, w), lambda s, h: (0, 0))]
    args = [proj, proj, proj, lam_p, nw.reshape(1, w)]
    if latent:
        cos, sin = _rope_tables(seq_len, D_HD)
        in_specs += [pl.BlockSpec((seq_len, 128), lambda s, h: (0, 0)),
                     pl.BlockSpec((seq_len, 128), lambda s, h: (0, 0)),
                     pl.BlockSpec((1, 1, PAST_LEN, w), lambda s, h: (s, li, 0, h)),
                     pl.BlockSpec((1, 1, PAST_LEN, w), lambda s, h: (s, li, 0, h))]
        args += [cos, sin, cache_k.reshape(DEC_BATCH, DEPTH, PAST_LEN, D_HEADS * w),
                 cache_v.reshape(DEC_BATCH, DEPTH, PAST_LEN, D_HEADS * w)]
    lam_init = 0.8 - 0.6 * math.exp(-0.3 * li)
    return pl.pallas_call(
        functools.partial(_diff_attn_kernel, seq_len=seq_len, latent=latent, lam_init=lam_init),
        out_shape=jax.ShapeDtypeStruct((n_seq * seq_len, BRANCH_W), BF16),
        grid=(n_seq, D_HEADS),
        in_specs=in_specs,
        out_specs=pl.BlockSpec((seq_len, w), lambda s, h: (s, h)),
        compiler_params=_cparams(("arbitrary", "arbitrary")),
        name="diff_attn",
    )(*args)


def _rmsnorm(x, w):
    return x * lax.rsqrt(jnp.mean(x * x, axis=-1, keepdims=True) + EPS) * w


def _l2norm(x):
    return x * lax.rsqrt(jnp.sum(x * x, axis=-1, keepdims=True) + EPS)


def _flip(x):
    return jnp.flip(x, axis=1)


def _axial_rope(x):
    L, dim = x.shape[1], x.shape[-1]
    n_rows = L // GRID_W
    row = jnp.repeat(jnp.arange(n_rows), GRID_W).astype(F32)
    col = jnp.tile(jnp.arange(GRID_W), n_rows).astype(F32)
    nf = dim // 4
    inv = jnp.power(ROPE_BASE, -jnp.arange(nf, dtype=F32) / nf)
    ang_r = row[:, None] * inv
    ang_c = col[:, None] * inv
    ang = jnp.concatenate([ang_r, ang_r, ang_c, ang_c], axis=-1)
    shape = (1, L) + (1,) * (x.ndim - 3) + (dim,)
    cos = jnp.cos(ang).reshape(shape)
    sin = jnp.sin(ang).reshape(shape)
    x_r1, x_r2, x_c1, x_c2 = jnp.split(x, 4, axis=-1)
    rot = jnp.concatenate([-x_r2, x_r1, -x_c2, x_c1], axis=-1)
    return x * cos + rot * sin


def _to_chunks(x):
    B, L, H = x.shape[:3]
    x = x.reshape((B, L // CHUNK, CHUNK, H) + x.shape[3:])
    return jnp.moveaxis(x, 3, 1)


def _from_chunks(x):
    x = jnp.moveaxis(x, 1, 3)
    B, N, C, H = x.shape[:4]
    return x.reshape((B, N * C, H) + x.shape[4:])


def _retention_scan(q, k, v, log_gamma, s0):
    q = _to_chunks(q * R_DK ** -0.5)
    k = _to_chunks(k)
    v = _to_chunks(v)
    pos = jnp.arange(CHUNK, dtype=F32)
    lg = log_gamma[:, None]
    dist = pos[:, None] - pos[None, :]
    intra = jnp.exp(jnp.where(dist >= 0, lg[:, :, None] * dist, -jnp.inf))
    q_dec = jnp.exp(lg * (pos + 1.0))[:, :, None]
    k_dec = jnp.exp(lg * (CHUNK - 1.0 - pos))[:, :, None]
    c_dec = jnp.exp(lg * CHUNK)[:, :, None]

    def step(s, xs):
        qc, kc, vc = xs
        a = jnp.einsum('bhqd,bhkd->bhqk', qc, kc) * intra
        o = jnp.einsum('bhqk,bhkv->bhqv', a, vc) + jnp.einsum('bhqd,bhdv->bhqv', qc, s) * q_dec
        s = s * c_dec + jnp.einsum('bhkd,bhkv->bhdv', kc * k_dec, vc)
        return s, o

    xs = (jnp.moveaxis(q, 2, 0), jnp.moveaxis(k, 2, 0), jnp.moveaxis(v, 2, 0))
    s, o = lax.scan(step, s0, xs)
    return _from_chunks(jnp.moveaxis(o, 0, 2)), s


def _gdn_scan(q, k, v, g, beta, s0):
    q = _to_chunks(q * G_DK ** -0.5)
    k = _to_chunks(k)
    v = _to_chunks(v)
    g = jnp.cumsum(_to_chunks(g), axis=-1)
    beta = _to_chunks(beta)
    incl = jnp.tril(jnp.ones((CHUNK, CHUNK), dtype=bool))
    strict = jnp.tril(jnp.ones((CHUNK, CHUNK), dtype=bool), -1)
    decay = jnp.exp(jnp.where(incl, g[..., :, None] - g[..., None, :], -jnp.inf))
    kb = k * beta[..., None]
    a = jnp.where(strict, jnp.einsum('bhnid,bhnjd->bhnij', kb, k) * decay, 0.0)
    eye = jnp.eye(CHUNK, dtype=F32)
    t = lax.linalg.triangular_solve(a + eye, jnp.broadcast_to(eye, a.shape), left_side=True, lower=True)
    u = t @ (v * beta[..., None])
    w = t @ (kb * jnp.exp(g)[..., None])
    qk = jnp.einsum('bhnid,bhnjd->bhnij', q, k) * decay
    qg = q * jnp.exp(g)[..., None]
    kd = k * jnp.exp(g[..., -1:] - g)[..., None]
    gl = jnp.exp(g[..., -1])[..., None, None]

    def step(s, xs):
        qg_c, kd_c, u_c, w_c, qk_c, gl_c = xs
        u_new = u_c - w_c @ s
        o = qg_c @ s + qk_c @ u_new
        s = s * gl_c + jnp.swapaxes(kd_c, -1, -2) @ u_new
        return s, o

    xs = tuple(jnp.moveaxis(arr, 2, 0) for arr in (qg, kd, u, w, qk, gl))
    s, o = lax.scan(step, s0, xs)
    return _from_chunks(jnp.moveaxis(o, 0, 2)), s


def _diff_attention(q, k, v, lam):
    scale = D_HD ** -0.5
    s = jnp.einsum('bqhpd,bkhpd->bhpqk', q, k) * scale
    p = jax.nn.softmax(s, axis=-1)
    wts = p[:, :, 0] - lam * p[:, :, 1]
    return jnp.einsum('bhqk,bkhe->bqhe', wts, v)


def _centred_dwconv(x, w):
    pad = (w.shape[0] - 1) // 2
    return lax.conv_general_dilated(x, w[:, None, :], window_strides=(1,), padding=[(pad, pad)],
                                    dimension_numbers=('NWC', 'WIO', 'NWC'),
                                    feature_group_count=x.shape[-1])


def _mixers_jax(proj, gab, lw, li, ctx, with_gdn=True):
    B, L, _ = proj.shape
    latent = ctx is not None
    r_q = proj[..., C_RQ:C_RK].reshape(B, L, R_HEADS, R_DK)
    r_k = proj[..., C_RK:C_RV].reshape(B, L, R_HEADS, R_DK)
    r_v = proj[..., C_RV:C_RG].reshape(B, L, R_HEADS, R_DV)
    r_g = proj[..., C_RG:C_DQ]
    if latent:
        r_q, r_k = _axial_rope(r_q), _axial_rope(r_k)
        r0_f, r0_b = ctx['ret'][:, 0], ctx['ret'][:, 1]
    else:
        r0_f = r0_b = jnp.zeros((B, R_HEADS, R_DK, R_DV), F32)
    log_gamma = jax.nn.log_sigmoid(lw['ret_decay'])
    or_f, sr_f = _retention_scan(r_q, r_k, r_v, log_gamma[0], r0_f)
    or_b, sr_b = _retention_scan(_flip(r_q), _flip(r_k), _flip(r_v), log_gamma[1], r0_b)
    o_ret = _rmsnorm(or_f + _flip(or_b), lw['ret_norm']).reshape(B, L, BRANCH_W) * jax.nn.silu(r_g)

    d_q = proj[..., C_DQ:C_DK].reshape(B, L, D_HEADS, 2, D_HD)
    d_k = proj[..., C_DK:C_DV].reshape(B, L, D_HEADS, 2, D_HD)
    d_v = proj[..., C_DV:C_GQKV].reshape(B, L, D_HEADS, 2 * D_HD)
    if latent:
        q_att = _axial_rope(d_q)
        k_all = jnp.concatenate([_axial_rope(d_k), ctx['k']], axis=1)
        v_all = jnp.concatenate([d_v, ctx['v']], axis=1)
    else:
        q_att, k_all, v_all = d_q, d_k, d_v
    lam_init = 0.8 - 0.6 * math.exp(-0.3 * li)
    lp = lw['diff_lambda']
    lam = jnp.exp(jnp.sum(lp[0] * lp[1])) - jnp.exp(jnp.sum(lp[2] * lp[3])) + lam_init
    o_diff = _diff_attention(q_att, k_all, v_all, lam)
    o_diff = (_rmsnorm(o_diff, lw['diff_norm']) * (1.0 - lam_init)).reshape(B, L, BRANCH_W)
    if not with_gdn:
        return o_ret, o_diff, None, (d_k, d_v, jnp.stack([sr_f, sr_b], axis=1), None)

    g_qkv = jax.nn.silu(_centred_dwconv(proj[..., C_GQKV:C_GZ], lw['gdn_conv']))
    g_q, g_k, g_v = jnp.split(g_qkv, [G_HEADS * G_DK, 2 * G_HEADS * G_DK], axis=-1)
    g_q = _l2norm(g_q.reshape(B, L, G_HEADS, G_DK))
    g_k = _l2norm(g_k.reshape(B, L, G_HEADS, G_DK))
    g_v = g_v.reshape(B, L, G_HEADS, G_DV)
    g_z = proj[..., C_GZ:C_MG]
    g_a = gab[..., :16].reshape(B, L, 2, G_HEADS)
    g_b = gab[..., 16:32].reshape(B, L, 2, G_HEADS)
    g_log = -jnp.exp(lw['gdn_A_log']) * jax.nn.softplus(g_a + lw['gdn_dt_bias'])
    g_beta = jax.nn.sigmoid(g_b)
    if latent:
        g0_f, g0_b = ctx['gdn'][:, 0], ctx['gdn'][:, 1]
    else:
        g0_f = g0_b = jnp.zeros((B, G_HEADS, G_DK, G_DV), F32)
    og_f, sg_f = _gdn_scan(g_q, g_k, g_v, g_log[:, :, 0], g_beta[:, :, 0], g0_f)
    og_b, sg_b = _gdn_scan(_flip(g_q), _flip(g_k), _flip(g_v), _flip(g_log[:, :, 1]), _flip(g_beta[:, :, 1]), g0_b)
    o_gdn = _rmsnorm(og_f + _flip(og_b), lw['gdn_norm']) * jax.nn.silu(g_z.reshape(B, L, G_HEADS, G_DV))
    o_gdn = o_gdn.reshape(B, L, BRANCH_W)
    state = (d_k, d_v, jnp.stack([sr_f, sr_b], axis=1), jnp.stack([sg_f, sg_b], axis=1))
    return o_ret, o_diff, o_gdn, state


def _peer_jax(h, w_q, sub_keys, exp_u, exp_v):
    hb = h.reshape(N_TOK // TOK_BLOCK, TOK_BLOCK, D_MODEL)

    def block(hx):
        q = jnp.dot(hx, w_q, preferred_element_type=F32).reshape(TOK_BLOCK, PEER_HEADS, 2, PEER_DQ // 2)
        s = jnp.einsum('thpd,pnd->thpn', q, sub_keys)
        s_top, i_top = lax.top_k(s, PEER_TOPK)
        cand_s = (s_top[:, :, 0, :, None] + s_top[:, :, 1, None, :]).reshape(TOK_BLOCK, PEER_HEADS, -1)
        cand_i = (i_top[:, :, 0, :, None] * N_KEYS + i_top[:, :, 1, None, :]).reshape(TOK_BLOCK, PEER_HEADS, -1)
        best_s, pos = lax.top_k(cand_s, PEER_TOPK)
        idx = jnp.take_along_axis(cand_i, pos, axis=-1)
        gate = jax.nn.softmax(best_s, axis=-1)
        hf = hx.astype(F32)
        act = jax.nn.gelu(jnp.einsum('thkd,td->thk', exp_u[idx], hf), approximate=False)
        return jnp.einsum('thk,thkd->td', gate * act, exp_v[idx])

    return lax.map(block, hb).reshape(N_TOK, D_MODEL)


def _per_block(rows):
    idx = np.concatenate([np.zeros(N_CTX // MOD_BLK, np.int32)] +
                         [np.full(DEC_SEQ // MOD_BLK, 1 + b, np.int32) for b in range(DEC_BATCH)])
    return rows[idx][:, None, :]


def kernel(x_prompt, x_sample, cache_diff_k, cache_diff_v, state_ret, state_gdn, c, c_ctx,
           norm1_w, norm2_w, w_mod, b_mod, w_in, ret_decay, ret_norm, diff_lambda, diff_norm,
           gdn_conv, gdn_A_log, gdn_dt_bias, gdn_norm, w_branch, w_out,
           peer_wq, peer_keys, peer_u, peer_v, norm_f_w):
    x = jnp.concatenate([x_prompt.reshape(N_CTX, D_MODEL), x_sample.reshape(N_LAT, D_MODEL)], axis=0)
    cond = jnp.concatenate([c_ctx[None, :], c, jnp.zeros((8 - N_COND, D_MODEL), F32)], axis=0)
    mod_all = _modulation(cond, w_mod, b_mod)

    w_main = jnp.concatenate([w_in[:, :, :AB_OFF], w_in[:, :, AB_OFF + 32:]], axis=-1).astype(BF16)
    w_ab = jnp.pad(w_in[:, :, AB_OFF:AB_OFF + 32], ((0, 0), (0, 0), (0, 96))).astype(BF16)
    wb_bf = w_branch.astype(BF16)
    wo_bf = w_out.astype(BF16)
    wq_bf = peer_wq.astype(BF16)

    ks, vs, srs, sgs = [], [], [], []
    for li in range(DEPTH):
        lw = dict(ret_decay=ret_decay[li], ret_norm=ret_norm[li], diff_lambda=diff_lambda[li],
                  diff_norm=diff_norm[li], gdn_conv=gdn_conv[li], gdn_A_log=gdn_A_log[li],
                  gdn_dt_bias=gdn_dt_bias[li], gdn_norm=gdn_norm[li])
        mod = mod_all[li, :N_COND]
        sh1, sc1, g1, sh2, sc2, g2 = [_per_block(m) for m in jnp.split(mod, 6, axis=-1)]
        h = _normmod(x, norm1_w[li], sc1, sh1, BF16)
        proj = _matmul(h, w_main[li], 512, 1024)
        gab = _matmul(h, w_ab[li], 512, 128)

        oa, sr_l = _retention(proj, ret_decay[li], ret_norm[li], SEQ, BATCH, 0)
        za, _ = _retention(proj, ret_decay[li], ret_norm[li], DEC_SEQ, DEC_BATCH, N_CTX, s0=state_ret[:, li])
        ob = _diff_attn(proj, diff_lambda[li], diff_norm[li], li, SEQ, BATCH, 0)
        zb = _diff_attn(proj, diff_lambda[li], diff_norm[li], li, DEC_SEQ, DEC_BATCH, N_CTX, cache_diff_k, cache_diff_v)
        k_l = proj[:N_CTX, C_DK:C_DV].reshape(BATCH, SEQ, D_HEADS, 2, D_HD)
        v_l = proj[:N_CTX, C_DV:C_GQKV].reshape(BATCH, SEQ, D_HEADS, 2 * D_HD)

        gcol = gab[:, :32].reshape(N_TOK, 2, 2, G_HEADS).transpose(1, 3, 2, 0).reshape(32, N_TOK, 1)
        alog = gdn_A_log[li].T.reshape(2 * G_HEADS, 1, 1)
        dtb = gdn_dt_bias[li].T.reshape(2 * G_HEADS, 1, 1)
        qkv_c = _gdn_prep(proj, gdn_conv[li], SEQ, BATCH, 0)
        oc, sg_l = _gdn(qkv_c, proj, gcol, alog, dtb, gdn_norm[li], None, SEQ, BATCH, 0)
        qkv_z = _gdn_prep(proj, gdn_conv[li], DEC_SEQ, DEC_BATCH, N_CTX)
        zc, _ = _gdn(qkv_z, proj, gcol, alog, dtb, gdn_norm[li], state_gdn[:, li], DEC_SEQ, DEC_BATCH, N_CTX)
        ks.append(k_l); vs.append(v_l); srs.append(sr_l); sgs.append(sg_l)
        cat = lambda a, b: jnp.concatenate([a, b], axis=0)
        o_ret, o_diff, o_gdn = cat(oa, za), cat(ob, zb), cat(oc, zc)

        x, h2 = _merge(o_ret, o_diff, o_gdn, proj, wb_bf[li], wo_bf[li], x, g1, norm2_w[li], sc2, sh2)
        idx, gate = _peer_route(h2, wq_bf[li], peer_keys[li])
        x = _peer_experts(h2, idx, gate, peer_u[li].T.astype(BF16), peer_v[li].astype(BF16), x, g2)

    zeros_blk = jnp.zeros((N_MOD_BLK, 1, D_MODEL), F32)
    y = _normmod(x, norm_f_w, zeros_blk, zeros_blk, F32)
    y_prompt = y[:N_CTX].reshape(BATCH, SEQ, D_MODEL)
    y_sample = y[N_CTX:].reshape(DEC_BATCH, DEC_SEQ, D_MODEL)
    return (y_prompt, y_sample, jnp.stack(ks, axis=1), jnp.stack(vs, axis=1),
            jnp.stack(srs, axis=1), jnp.stack(sgs, axis=1))
```

```python
import functools
import math

import numpy as np
import jax
import jax.numpy as jnp
from jax import lax
from jax.experimental import pallas as pl
from jax.experimental.pallas import tpu as pltpu

F32 = jnp.float32
BF16 = jnp.bfloat16

D_MODEL = 1024
BATCH = 16
SEQ = 256
DEPTH = 2
DEC_BATCH = 2
DEC_SEQ = 1024
PAST_LEN = 256
GRID_W = 64
ROPE_BASE = 10000.0
EPS = 1e-6
CHUNK = 64
R_HEADS, R_DK, R_DV = 4, 128, 256
D_HEADS, D_HD = 8, 64
G_HEADS, G_DK, G_DV = 8, 128, 128
CONV_K = 7
G_QKV = G_HEADS * (2 * G_DK + G_DV)
N_BRANCH = 3
BRANCH_W = 1024
PEER_HEADS = 8
PEER_DQ = 256
N_KEYS = 128
PEER_TOPK = 16
TOK_BLOCK = 128

N_CTX = BATCH * SEQ
N_LAT = DEC_BATCH * DEC_SEQ
N_TOK = N_CTX + N_LAT
N_COND = 1 + DEC_BATCH
MOD_BLK = 256
N_MOD_BLK = N_TOK // MOD_BLK

C_RQ, C_RK, C_RV, C_RG = 0, 512, 1024, 2048
C_DQ, C_DK, C_DV = 3072, 4096, 5120
C_GQKV, C_GZ, C_MG = 6144, 9216, 10240
N_MAIN = 13312
AB_OFF = 10240

VMEM_LIMIT = 56 * 1024 * 1024


def _cparams(sem):
    return pltpu.CompilerParams(dimension_semantics=sem, vmem_limit_bytes=VMEM_LIMIT)


def _mod_kernel(c_ref, w_ref, b_ref, o_ref):
    c = c_ref[...]
    a = c * jax.nn.sigmoid(c)
    o_ref[0] = jnp.dot(a, w_ref[0], preferred_element_type=F32,
                       precision=lax.Precision.HIGHEST) + b_ref[0]


def _modulation(cond_pad, w_mod, b_mod):
    tn = 1536
    return pl.pallas_call(
        _mod_kernel,
        out_shape=jax.ShapeDtypeStruct((DEPTH, 8, 6 * D_MODEL), F32),
        grid=(DEPTH, 6 * D_MODEL // tn),
        in_specs=[pl.BlockSpec((8, D_MODEL), lambda l, j: (0, 0)),
                  pl.BlockSpec((1, D_MODEL, tn), lambda l, j: (l, 0, j)),
                  pl.BlockSpec((1, 1, tn), lambda l, j: (l, 0, j))],
        out_specs=pl.BlockSpec((1, 8, tn), lambda l, j: (l, 0, j)),
        compiler_params=_cparams(("arbitrary", "arbitrary")),
        name="modulation",
    )(cond_pad, w_mod, b_mod.reshape(DEPTH, 1, 6 * D_MODEL))


def _normmod_kernel(x_ref, w_ref, sc_ref, sh_ref, o_ref):
    x = x_ref[...]
    y = x * lax.rsqrt(jnp.mean(x * x, axis=-1, keepdims=True) + EPS)
    y = y * w_ref[...]
    o_ref[...] = (y * (1.0 + sc_ref[0]) + sh_ref[0]).astype(o_ref.dtype)


def _normmod(x, w, sc_blk, sh_blk, out_dtype):
    tm = MOD_BLK
    return pl.pallas_call(
        _normmod_kernel,
        out_shape=jax.ShapeDtypeStruct((N_TOK, D_MODEL), out_dtype),
        grid=(N_TOK // tm,),
        in_specs=[pl.BlockSpec((tm, D_MODEL), lambda i: (i, 0)),
                  pl.BlockSpec((1, D_MODEL), lambda i: (0, 0)),
                  pl.BlockSpec((1, 1, D_MODEL), lambda i: (i, 0, 0)),
                  pl.BlockSpec((1, 1, D_MODEL), lambda i: (i, 0, 0))],
        out_specs=pl.BlockSpec((tm, D_MODEL), lambda i: (i, 0)),
        compiler_params=_cparams(("arbitrary",)),
        name="normmod",
    )(x, w.reshape(1, D_MODEL), sc_blk, sh_blk)


def _matmul_kernel(a_ref, b_ref, o_ref):
    o_ref[...] = jnp.dot(a_ref[...], b_ref[...], preferred_element_type=F32)


def _matmul(a, b, tm, tn):
    m, k = a.shape
    n = b.shape[1]
    return pl.pallas_call(
        _matmul_kernel,
        out_shape=jax.ShapeDtypeStruct((m, n), F32),
        grid=(n // tn, m // tm),
        in_specs=[pl.BlockSpec((tm, k), lambda j, i: (i, 0)),
                  pl.BlockSpec((k, tn), lambda j, i: (0, j))],
        out_specs=pl.BlockSpec((tm, tn), lambda j, i: (i, j)),
        compiler_params=_cparams(("arbitrary", "arbitrary")),
        name="matmul",
    )(a, b)


def _merge_kernel(oa_ref, ob_ref, oc_ref, mga_ref, mgb_ref, mgc_ref, wb_ref, wo_ref, x_ref, g1_ref,
                  n2_ref, sc_ref, sh_ref, xo_ref, h_ref):
    mix = None
    for n, (o_ref, mg_ref) in enumerate(((oa_ref, mga_ref), (ob_ref, mgb_ref), (oc_ref, mgc_ref))):
        merged = jnp.dot(o_ref[...], wb_ref[n], preferred_element_type=F32)
        gate = jax.nn.sigmoid(mg_ref[...])
        mix = gate * merged if mix is None else mix + gate * merged
    m = jnp.dot(mix.astype(BF16), wo_ref[...], preferred_element_type=F32)
    x = x_ref[...] + g1_ref[0] * m
    xo_ref[...] = x
    y = x * lax.rsqrt(jnp.mean(x * x, axis=-1, keepdims=True) + EPS)
    y = y * n2_ref[...]
    h_ref[...] = (y * (1.0 + sc_ref[0]) + sh_ref[0]).astype(h_ref.dtype)


def _merge(o_ret, o_diff, o_gdn, proj, wb, wo, x, g1_blk, n2w, sc2_blk, sh2_blk):
    tm = MOD_BLK
    tok = lambda i: (i, 0)
    blk = lambda i: (i, 0, 0)
    return pl.pallas_call(
        _merge_kernel,
        out_shape=(jax.ShapeDtypeStruct((N_TOK, D_MODEL), F32),
                   jax.ShapeDtypeStruct((N_TOK, D_MODEL), BF16)),
        grid=(N_TOK // tm,),
        in_specs=[pl.BlockSpec((tm, BRANCH_W), tok),
                  pl.BlockSpec((tm, BRANCH_W), tok),
                  pl.BlockSpec((tm, BRANCH_W), tok),
                  pl.BlockSpec((tm, D_MODEL), lambda i: (i, C_MG // D_MODEL)),
                  pl.BlockSpec((tm, D_MODEL), lambda i: (i, C_MG // D_MODEL + 1)),
                  pl.BlockSpec((tm, D_MODEL), lambda i: (i, C_MG // D_MODEL + 2)),
                  pl.BlockSpec((N_BRANCH, BRANCH_W, D_MODEL), lambda i: (0, 0, 0)),
                  pl.BlockSpec((D_MODEL, D_MODEL), lambda i: (0, 0)),
                  pl.BlockSpec((tm, D_MODEL), tok),
                  pl.BlockSpec((1, 1, D_MODEL), blk),
                  pl.BlockSpec((1, D_MODEL), lambda i: (0, 0)),
                  pl.BlockSpec((1, 1, D_MODEL), blk),
                  pl.BlockSpec((1, 1, D_MODEL), blk)],
        out_specs=(pl.BlockSpec((tm, D_MODEL), tok), pl.BlockSpec((tm, D_MODEL), tok)),
        compiler_params=_cparams(("arbitrary",)),
        name="merge",
    )(o_ret, o_diff, o_gdn, proj, proj, proj, wb, wo, x, g1_blk, n2w.reshape(1, D_MODEL), sc2_blk, sh2_blk)


ROUTE_T = 128
ROUTE_HEADS = 2
N_PAIR = PEER_HEADS * PEER_TOPK
N_CAND = PEER_TOPK + (PEER_TOPK // 2) * (PEER_TOPK // 2 - 1) + PEER_TOPK // 2
NEG_INF = float("-inf")


def _split_bf16(x):
    hi = x.astype(BF16)
    return hi, (x - hi.astype(F32)).astype(BF16)


def _topk_rows(scores, rows, n_rows, payloads=None):
    scores = list(scores)
    vals = [[] for _ in scores]
    outs = [[] for _ in scores]
    for _ in range(PEER_TOPK):
        for n, s in enumerate(scores):
            m = jnp.max(s, axis=0, keepdims=True)
            pos = jnp.min(jnp.where(s == m, rows, n_rows), axis=0, keepdims=True)
            sel = rows == pos
            vals[n].append(m)
            if payloads is None:
                outs[n].append(pos)
            else:
                outs[n].append(jnp.sum(jnp.where(sel, payloads[n], 0), axis=0, keepdims=True))
            scores[n] = jnp.where(sel, NEG_INF, s)
    return [(jnp.concatenate(v, axis=0), jnp.concatenate(o, axis=0)) for v, o in zip(vals, outs)]


def _route_kernel(h_ref, wq_ref, khi_ref, klo_ref, idx_ref, gate_ref, q_scr, idx_scr, gate_scr):
    t = h_ref.shape[0]
    q = jnp.dot(h_ref[...], wq_ref[...], preferred_element_type=F32)
    for hp in range(2 * PEER_HEADS):
        q_scr[hp] = q[:, hp * 128:(hp + 1) * 128]
    rows_k = lax.broadcasted_iota(jnp.int32, (N_KEYS, t), 0)
    rows_c = lax.broadcasted_iota(jnp.int32, (N_CAND, t), 0)
    nt = (((1,), (1,)), ((), ()))

    half = PEER_TOPK // 2
    groups = [(slice(0, 1), slice(0, PEER_TOPK)), (slice(1, 2), slice(0, half))]
    groups += [(slice(k, k + 1), slice(0, half)) for k in range(2, half)]
    groups += [(slice(half, PEER_TOPK), slice(0, 1))]

    def heads(hh, carry):
        hs = [ROUTE_HEADS * hh + n for n in range(ROUTE_HEADS)]
        scores = []
        for h in hs:
            for p in range(2):
                qhi, qlo = _split_bf16(q_scr[2 * h + p])
                scores.append(lax.dot_general(khi_ref[p], qhi, nt, preferred_element_type=F32)
                              + lax.dot_general(khi_ref[p], qlo, nt, preferred_element_type=F32)
                              + lax.dot_general(klo_ref[p], qhi, nt, preferred_element_type=F32))
        tops = _topk_rows(scores, rows_k, N_KEYS)
        cands, cand_is = [], []
        for n in range(ROUTE_HEADS):
            (v1, i1), (v2, i2) = tops[2 * n], tops[2 * n + 1]
            cands.append(jnp.concatenate([v1[a] + v2[b] for a, b in groups], axis=0))
            cand_is.append(jnp.concatenate([i1[a] * N_KEYS + i2[b] for a, b in groups], axis=0))
        for h, (best, bidx) in zip(hs, _topk_rows(cands, rows_c, N_CAND, payloads=cand_is)):
            e = jnp.exp(best - best[0:1])
            gate_scr[h] = e / jnp.sum(e, axis=0, keepdims=True)
            idx_scr[h] = bidx
        return carry

    lax.fori_loop(0, PEER_HEADS // ROUTE_HEADS, heads, 0)
    idx_ref[...] = idx_scr[...].reshape(N_PAIR, t).T
    gate_ref[...] = gate_scr[...].reshape(N_PAIR, t).T


def _peer_route(h2, wq, keys):
    khi, klo = _split_bf16(keys)
    t = ROUTE_T
    n_tok = h2.shape[0]
    return pl.pallas_call(
        _route_kernel,
        out_shape=(jax.ShapeDtypeStruct((n_tok, N_PAIR), jnp.int32),
                   jax.ShapeDtypeStruct((n_tok, N_PAIR), F32)),
        grid=(n_tok // t,),
        in_specs=[pl.BlockSpec((t, D_MODEL), lambda i: (i, 0)),
                  pl.BlockSpec((D_MODEL, PEER_HEADS * PEER_DQ), lambda i: (0, 0)),
                  pl.BlockSpec((2, N_KEYS, PEER_DQ // 2), lambda i: (0, 0, 0)),
                  pl.BlockSpec((2, N_KEYS, PEER_DQ // 2), lambda i: (0, 0, 0))],
        out_specs=(pl.BlockSpec((t, N_PAIR), lambda i: (i, 0)),
                   pl.BlockSpec((t, N_PAIR), lambda i: (i, 0))),
        scratch_shapes=[pltpu.VMEM((2 * PEER_HEADS, t, PEER_DQ // 2), F32),
                        pltpu.VMEM((PEER_HEADS, PEER_TOPK, t), jnp.int32),
                        pltpu.VMEM((PEER_HEADS, PEER_TOPK, t), F32)],
        compiler_params=_cparams(("arbitrary",)),
        name="peer_route",
    )(h2, wq, khi, klo)


EXP_TB = 256
EXP_EB = 1024
EXP_SUB = 256
N_EXPERTS = N_KEYS * N_KEYS


def _expert_kernel(h_ref, idx_ref, gate_ref, ut_ref, v_ref, x_ref, g2_ref, o_ref, g_scr, acc_ref):
    j = pl.program_id(1)
    tb = h_ref.shape[0]
    nt = (((1,), (1,)), ((), ()))

    @pl.when(j == 0)
    def _():
        acc_ref[...] = jnp.zeros_like(acc_ref)
        sub = lax.broadcasted_iota(jnp.int32, (N_KEYS, N_PAIR), 0)

        def body(t, carry):
            e = idx_ref[pl.ds(t, 1), :]
            g = gate_ref[pl.ds(t, 1), :]
            ghi = g.astype(BF16).astype(F32)
            glo = g - ghi
            xa = jnp.where(sub == (e >> 7), 1.0, 0.0).astype(BF16)
            mb = sub == (e & (N_KEYS - 1))
            yhi = jnp.where(mb, ghi, 0.0).astype(BF16)
            ylo = jnp.where(mb, glo, 0.0).astype(BF16)
            gt = lax.dot_general(jnp.concatenate([xa, xa], axis=1), jnp.concatenate([yhi, ylo], axis=1), nt,
                                 preferred_element_type=F32)
            g_scr[pl.ds(pl.multiple_of(t * N_KEYS, N_KEYS), N_KEYS), :] = gt
            return carry

        lax.fori_loop(0, tb, body, 0, unroll=8)

    h = h_ref[...]
    nsub = EXP_EB // EXP_SUB
    per = EXP_SUB // N_KEYS
    s = [jnp.dot(h, ut_ref[:, c * EXP_SUB:(c + 1) * EXP_SUB], preferred_element_type=F32) for c in range(nsub)]
    acc = acc_ref[...]
    for c in range(nsub):
        gj = jnp.concatenate([g_scr[pl.ds(j * (nsub * per) + c * per + i, tb, stride=N_KEYS), :]
                              for i in range(per)], axis=1)
        w = 0.5 * s[c] * (1.0 + lax.erf(s[c] * (2.0 ** -0.5))) * gj
        acc = acc + jnp.dot(w.astype(BF16), v_ref[c * EXP_SUB:(c + 1) * EXP_SUB, :], preferred_element_type=F32)
    acc_ref[...] = acc

    @pl.when(j == pl.num_programs(1) - 1)
    def _():
        o_ref[...] = x_ref[...] + g2_ref[0] * acc_ref[...]


def _peer_experts(h2, idx, gate, u_t, v, x, g2_blk):
    tb, eb = EXP_TB, EXP_EB
    n_tok = h2.shape[0]
    return pl.pallas_call(
        _expert_kernel,
        out_shape=jax.ShapeDtypeStruct((n_tok, D_MODEL), F32),
        grid=(n_tok // tb, N_EXPERTS // eb),
        in_specs=[pl.BlockSpec((tb, D_MODEL), lambda i, j: (i, 0)),
                  pl.BlockSpec((tb, N_PAIR), lambda i, j: (i, 0)),
                  pl.BlockSpec((tb, N_PAIR), lambda i, j: (i, 0)),
                  pl.BlockSpec((D_MODEL, eb), lambda i, j: (0, j)),
                  pl.BlockSpec((eb, D_MODEL), lambda i, j: (j, 0)),
                  pl.BlockSpec((tb, D_MODEL), lambda i, j: (i, 0)),
                  pl.BlockSpec((1, 1, D_MODEL), lambda i, j: (i * tb // MOD_BLK, 0, 0))],
        out_specs=pl.BlockSpec((tb, D_MODEL), lambda i, j: (i, 0)),
        scratch_shapes=[pltpu.VMEM((tb * N_KEYS, N_KEYS), F32),
                        pltpu.VMEM((tb, D_MODEL), F32)],
        compiler_params=_cparams(("arbitrary", "arbitrary")),
        name="peer_experts",
    )(h2, idx, gate, u_t, v, x, g2_blk)


CONV_ROWS = 256
CONV_HALO = 8


def _gdn_prep_kernel(x_ref, w_ref, o_ref):
    sec = pl.program_id(1)
    L = x_ref.shape[0]
    half = CONV_K // 2
    for r0 in range(0, L, CONV_ROWS):
        s0, s1 = max(r0 - CONV_HALO, 0), min(r0 + CONV_ROWS + CONV_HALO, L)
        n = s1 - s0
        t_idx = lax.broadcasted_iota(jnp.int32, (n, 128), 0) + s0
        for cb in range(G_HEADS):
            cols = slice(cb * 128, (cb + 1) * 128)
            x = x_ref[s0:s1, cols]
            w = w_ref[0, :, cols]
            acc = x * w[half:half + 1]
            for d in range(-half, half + 1):
                if d == 0:
                    continue
                xs = pltpu.roll(x, (-d) % n, axis=0)
                valid = (t_idx + d >= 0) if d < 0 else (t_idx + d < L)
                acc = acc + jnp.where(valid, xs, 0.0) * w[half + d:half + d + 1]
            y = acc[r0 - s0:r0 - s0 + CONV_ROWS]
            y = y * jax.nn.sigmoid(y)
            inv = lax.rsqrt(jnp.sum(y * y, axis=-1, keepdims=True) + EPS)
            o_ref[r0:r0 + CONV_ROWS, cols] = y * jnp.where(sec < 2, inv, 1.0)


def _gdn_prep(proj, conv_w, seq_len, n_seq, tok_off):
    blk0 = tok_off // seq_len
    sec0 = C_GQKV // 1024
    return pl.pallas_call(
        _gdn_prep_kernel,
        out_shape=jax.ShapeDtypeStruct((n_seq * seq_len, G_QKV), F32),
        grid=(n_seq, 3),
        in_specs=[pl.BlockSpec((seq_len, 1024), lambda s, c: (blk0 + s, sec0 + c)),
                  pl.BlockSpec((1, CONV_K, 1024), lambda s, c: (0, 0, c))],
        out_specs=pl.BlockSpec((seq_len, 1024), lambda s, c: (s, c)),
        compiler_params=_cparams(("arbitrary", "arbitrary")),
        name="gdn_prep",
    )(proj, conv_w.reshape(1, CONV_K, G_QKV))


def _dot(a, b):
    return jnp.dot(a, b, preferred_element_type=F32)


def _dot_nt(a, b):
    return lax.dot_general(a, b, (((1,), (1,)), ((), ())), preferred_element_type=F32)


def _dot_tn(a, b):
    return lax.dot_general(a, b, (((0,), (0,)), ((), ())), preferred_element_type=F32)


def _dot3(a, b):
    ah, al = _split_bf16(a)
    bh, bl = _split_bf16(b)
    return _dot(ah, bh) + _dot(ah, bl) + _dot(al, bh)


def _mask_dot3(m, b):
    m16 = m.astype(BF16)
    bh = b.astype(BF16)
    r1 = b - bh.astype(F32)
    bm = r1.astype(BF16)
    bl = (r1 - bm.astype(F32)).astype(BF16)
    return _dot(m16, bh) + _dot(m16, bm) + _dot(m16, bl)


def _gdn_kernel(*refs, seq_len, has_init, emit_state):
    it = iter(refs)
    q_ref, k_ref, v_ref, z_ref, ga_ref, gb_ref, alog_ref, dtb_ref, nw_ref = [next(it) for _ in range(9)]
    s0_ref = next(it) if has_init else None
    o_ref = next(it)
    so_ref = next(it) if emit_state else None
    u_s, w_s, qg_s, kd_s, qk_s, gl_s, o_s = [next(it) for _ in range(7)]
    C = CHUNK
    nc = seq_len // C
    ri = lax.broadcasted_iota(jnp.int32, (C, C), 0)
    ci = lax.broadcasted_iota(jnp.int32, (C, C), 1)
    eye = jnp.where(ri == ci, 1.0, 0.0)
    ones = jnp.ones((C, C), F32)
    scale = G_DK ** -0.5
    incl = (ri >= ci, ri <= ci)
    strict = (ri > ci, ri < ci)
    levels = []
    for lv in range(6):
        b = 1 << lv
        same = (ri >> (lv + 1)) == (ci >> (lv + 1))
        r_hi, c_hi = (ri & b) != 0, (ci & b) != 0
        levels.append((same & r_hi & jnp.logical_not(c_hi), same & jnp.logical_not(r_hi) & c_hi))

    GROUP = 4

    def prep(grp, carry):
        chunks = [grp * GROUP + j for j in range(GROUP)]
        rows = [pl.ds(pl.multiple_of(c * C, C), C) for c in chunks]
        q = [q_ref[r, :] * scale for r in rows]
        k = [k_ref[r, :] for r in rows]
        v = [v_ref[r, :] for r in rows]
        k16 = [x.astype(BF16) for x in k]
        kk = [_dot_nt(x, x) for x in k16]
        qk = [_dot_nt(a.astype(BF16), b) for a, b in zip(q, k16)]
        probs = [(j, d) for j in range(GROUP) for d in range(2)]
        g = [-jnp.exp(alog_ref[d]) * jax.nn.softplus(ga_ref[d, rows[j], :] + dtb_ref[d]) for j, d in probs]
        beta = [jax.nn.sigmoid(gb_ref[d, rows[j], :]) for j, d in probs]
        tri = [jnp.where(m, 1.0, 0.0) for m in incl]
        gcol = [_mask_dot3(tri[d], jnp.broadcast_to(g[p], (C, 128))) for p, (j, d) in enumerate(probs)]
        grow = [_mask_dot3(ones, jnp.where(incl[1 - d], jnp.broadcast_to(g[p], (C, C)), 0.0))
                for p, (j, d) in enumerate(probs)]
        dec = [jnp.exp(jnp.where(incl[d], gcol[p][:, :C] - grow[p], NEG_INF)) for p, (j, d) in enumerate(probs)]
        a = [jnp.where(strict[d], kk[j] * beta[p] * dec[p], 0.0) for p, (j, d) in enumerate(probs)]
        t = [eye - jnp.where(levels[0][d], a[p], 0.0) for p, (j, d) in enumerate(probs)]
        for lv in range(1, 6):
            ct = [_dot3(jnp.where(levels[lv][d], a[p], 0.0), t[p]) for p, (j, d) in enumerate(probs)]
            t = [t[p] - _dot3(t[p], ct[p]) for p in range(len(probs))]
        for p, (j, d) in enumerate(probs):
            eg = jnp.exp(gcol[p])
            t16 = t[p].astype(BF16)
            glast = gcol[p][C - 1:C, :] if d == 0 else gcol[p][0:1, :]
            u_s[d, rows[j], :] = _dot(t16, (v[j] * beta[p]).astype(BF16))
            w_s[d, rows[j], :] = _dot(t16, (k[j] * beta[p] * eg).astype(BF16))
            qg_s[d, rows[j], :] = q[j] * eg
            kd_s[d, rows[j], :] = k[j] * jnp.exp(glast - gcol[p])
            qk_s[d, rows[j], :] = qk[j] * dec[p]
            gl_s[d * nc + chunks[j]] = jnp.broadcast_to(jnp.exp(glast), (8, 128))
        return carry

    lax.fori_loop(0, nc // GROUP, prep, 0)
    o_s[...] = jnp.zeros_like(o_s)

    def step(i, states):
        cs = (i, nc - 1 - i)
        rows = [pl.ds(pl.multiple_of(c * C, C), C) for c in cs]
        s16 = [s.astype(BF16) for s in states]
        ws = [_dot(w_s[d, rows[d], :].astype(BF16), s16[d]) for d in range(2)]
        un16 = [(u_s[d, rows[d], :] - ws[d]).astype(BF16) for d in range(2)]
        o = [_dot(qg_s[d, rows[d], :].astype(BF16), s16[d]) + _dot(qk_s[d, rows[d], :].astype(BF16), un16[d])
             for d in range(2)]
        new = tuple(states[d] * gl_s[d * nc + cs[d]][0:1, :] + _dot_tn(kd_s[d, rows[d], :].astype(BF16), un16[d])
                    for d in range(2))
        for d in range(2):
            o_s[rows[d], :] += o[d]
        return new

    if has_init:
        init = (s0_ref[0, 0, 0], s0_ref[0, 1, 0])
    else:
        init = (jnp.zeros((G_DK, G_DV), F32),) * 2
    final = lax.fori_loop(0, nc, step, init)
    if emit_state:
        so_ref[0, 0, 0] = final[0]
        so_ref[0, 1, 0] = final[1]
    o = o_s[...]
    y = o * lax.rsqrt(jnp.mean(o * o, axis=-1, keepdims=True) + EPS) * nw_ref[...]
    z = z_ref[...]
    o_ref[...] = (y * (z * jax.nn.sigmoid(z))).astype(o_ref.dtype)


def _gdn(qkv, proj, gcol, alog, dtb, nw, s0, seq_len, n_seq, tok_off):
    has_init = s0 is not None
    emit_state = not has_init
    blk0 = tok_off // seq_len
    tokcol = lambda off: pl.BlockSpec((seq_len, 128), lambda s, h: (s, off + h))
    in_specs = [tokcol(0), tokcol(G_HEADS), tokcol(2 * G_HEADS),
                pl.BlockSpec((seq_len, 128), lambda s, h: (blk0 + s, C_GZ // 128 + h)),
                pl.BlockSpec((2, seq_len, 1), lambda s, h: (h, blk0 + s, 0)),
                pl.BlockSpec((2, seq_len, 1), lambda s, h: (G_HEADS + h, blk0 + s, 0)),
                pl.BlockSpec((2, 1, 1), lambda s, h: (h, 0, 0)),
                pl.BlockSpec((2, 1, 1), lambda s, h: (h, 0, 0)),
                pl.BlockSpec((1, G_DV), lambda s, h: (0, 0))]
    args = [qkv, qkv, qkv, proj, gcol, gcol, alog, dtb, nw.reshape(1, G_DV)]
    if has_init:
        in_specs.append(pl.BlockSpec((1, 2, 1, G_DK, G_DV), lambda s, h: (s, 0, h, 0, 0)))
        args.append(s0)
    out_shape = [jax.ShapeDtypeStruct((n_seq * seq_len, BRANCH_W), BF16)]
    out_specs = [pl.BlockSpec((seq_len, 128), lambda s, h: (s, h))]
    if emit_state:
        out_shape.append(jax.ShapeDtypeStruct((n_seq, 2, G_HEADS, G_DK, G_DV), F32))
        out_specs.append(pl.BlockSpec((1, 2, 1, G_DK, G_DV), lambda s, h: (s, 0, h, 0, 0)))
    nc = seq_len // CHUNK
    res = pl.pallas_call(
        functools.partial(_gdn_kernel, seq_len=seq_len, has_init=has_init, emit_state=emit_state),
        out_shape=tuple(out_shape),
        grid=(n_seq, G_HEADS),
        in_specs=in_specs,
        out_specs=tuple(out_specs),
        scratch_shapes=[pltpu.VMEM((2, seq_len, 128), F32)] * 4
                       + [pltpu.VMEM((2, seq_len, CHUNK), F32), pltpu.VMEM((2 * nc, 8, 128), F32),
                          pltpu.VMEM((seq_len, 128), F32)],
        compiler_params=_cparams(("arbitrary", "arbitrary")),
        name="gdn",
    )(*args)
    return res if emit_state else (res[0], None)


def _rope_tables(seq_len, dim):
    n_rows = seq_len // GRID_W
    row = np.repeat(np.arange(n_rows), GRID_W).astype(np.float32)
    col = np.tile(np.arange(GRID_W), n_rows).astype(np.float32)
    nf = dim // 4
    inv = np.power(np.float32(ROPE_BASE), -np.arange(nf, dtype=np.float32) / np.float32(nf)).astype(np.float32)
    ang_r = row[:, None] * inv
    ang_c = col[:, None] * inv
    ang = np.concatenate([ang_r, ang_r, ang_c, ang_c], axis=-1)
    ang = np.tile(ang, (1, 128 // dim))
    return jnp.asarray(np.cos(ang), F32), jnp.asarray(np.sin(ang), F32)


def _rope(x, cos, sin, dim):
    quarter = dim // 4
    lane = lax.broadcasted_iota(jnp.int32, x.shape, 1)
    first = (lane & (2 * quarter - 1)) < quarter
    rot = jnp.where(first, -pltpu.roll(x, 128 - quarter, axis=1), pltpu.roll(x, quarter, axis=1))
    return x * cos + rot * sin


Q_TILE = 256


def _retention_kernel(*refs, seq_len, latent):
    it = iter(refs)
    q_ref, k_ref, v_ref, rg_ref, lg_ref, nw_ref = [next(it) for _ in range(6)]
    if latent:
        cos_ref, sin_ref, s0_ref = next(it), next(it), next(it)
    o_ref = next(it)
    so_ref = None if latent else next(it)
    L = seq_len
    lg = lg_ref[0]
    lg = jnp.minimum(lg, 0.0) - jnp.log(1.0 + jnp.exp(-jnp.abs(lg)))
    lgf, lgb = lg[0:1, :], lg[1:2, :]
    k = k_ref[...]
    if latent:
        k = _rope(k, cos_ref[...], sin_ref[...], R_DK)
    k16 = k.astype(BF16)
    v16 = v_ref[...].astype(BF16)
    for qt in range(L // Q_TILE):
        rows = slice(qt * Q_TILE, (qt + 1) * Q_TILE)
        q = q_ref[rows, :]
        if latent:
            q = _rope(q, cos_ref[rows, :], sin_ref[rows, :], R_DK)
        q16 = (q * R_DK ** -0.5).astype(BF16)
        a = _dot_nt(q16, k16)
        i = lax.broadcasted_iota(jnp.int32, (Q_TILE, L), 0) + qt * Q_TILE
        j = lax.broadcasted_iota(jnp.int32, (Q_TILE, L), 1)
        dist = (i - j).astype(F32)
        dec = (jnp.exp(jnp.where(dist >= 0, lgf * dist, NEG_INF))
               + jnp.exp(jnp.where(dist <= 0, -lgb * dist, NEG_INF)))
        o = _dot((a * dec).astype(BF16), v16)
        if latent:
            pos = (lax.broadcasted_iota(jnp.int32, (Q_TILE, 1), 0) + qt * Q_TILE).astype(F32)
            o = o + _dot(q16, s0_ref[0, 0, 0].astype(BF16)) * jnp.exp(lgf * (pos + 1.0))
            o = o + _dot(q16, s0_ref[0, 1, 0].astype(BF16)) * jnp.exp(lgb * (L - pos))
        y = o * lax.rsqrt(jnp.mean(o * o, axis=-1, keepdims=True) + EPS) * nw_ref[...]
        g = rg_ref[rows, :]
        o_ref[rows, :] = (y * (g * jax.nn.sigmoid(g))).astype(o_ref.dtype)
    if not latent:
        pos = lax.broadcasted_iota(jnp.int32, (L, 1), 0).astype(F32)
        so_ref[0, 0, 0] = _dot_tn((k * jnp.exp(lgf * (L - 1.0 - pos))).astype(BF16), v16)
        so_ref[0, 1, 0] = _dot_tn((k * jnp.exp(lgb * pos)).astype(BF16), v16)


def _retention(proj, decay, nw, seq_len, n_seq, tok_off, s0=None):
    latent = s0 is not None
    blk0 = tok_off // seq_len
    in_specs = [pl.BlockSpec((seq_len, R_DK), lambda s, h: (blk0 + s, C_RQ // R_DK + h)),
                pl.BlockSpec((seq_len, R_DK), lambda s, h: (blk0 + s, C_RK // R_DK + h)),
                pl.BlockSpec((seq_len, R_DV), lambda s, h: (blk0 + s, C_RV // R_DV + h)),
                pl.BlockSpec((seq_len, R_DV), lambda s, h: (blk0 + s, C_RG // R_DV + h)),
                pl.BlockSpec((1, 2, 1), lambda s, h: (h, 0, 0)),
                pl.BlockSpec((1, R_DV), lambda s, h: (0, 0))]
    args = [proj, proj, proj, proj, decay.T.reshape(R_HEADS, 2, 1), nw.reshape(1, R_DV)]
    out_shape = [jax.ShapeDtypeStruct((n_seq * seq_len, BRANCH_W), BF16)]
    out_specs = [pl.BlockSpec((seq_len, R_DV), lambda s, h: (s, h))]
    if latent:
        cos, sin = _rope_tables(seq_len, R_DK)
        in_specs += [pl.BlockSpec((seq_len, 128), lambda s, h: (0, 0)),
                     pl.BlockSpec((seq_len, 128), lambda s, h: (0, 0)),
                     pl.BlockSpec((1, 2, 1, R_DK, R_DV), lambda s, h: (s, 0, h, 0, 0))]
        args += [cos, sin, s0]
    else:
        out_shape.append(jax.ShapeDtypeStruct((n_seq, 2, R_HEADS, R_DK, R_DV), F32))
        out_specs.append(pl.BlockSpec((1, 2, 1, R_DK, R_DV), lambda s, h: (s, 0, h, 0, 0)))
    res = pl.pallas_call(
        functools.partial(_retention_kernel, seq_len=seq_len, latent=latent),
        out_shape=tuple(out_shape),
        grid=(n_seq, R_HEADS),
        in_specs=in_specs,
        out_specs=tuple(out_specs),
        compiler_params=_cparams(("arbitrary", "arbitrary")),
        name="retention",
    )(*args)
    return (res[0], None) if latent else res


def _diff_attn_kernel(*refs, seq_len, latent, lam_init):
    it = iter(refs)
    q_ref, k_ref, v_ref, lp_ref, nw_ref = [next(it) for _ in range(5)]
    if latent:
        cos_ref, sin_ref, ck_ref, cv_ref = [next(it) for _ in range(4)]
    o_ref = next(it)
    L = seq_len
    lp = lp_ref[...]
    lam = (jnp.exp(jnp.sum(lp[0:1] * lp[1:2], axis=-1, keepdims=True))
           - jnp.exp(jnp.sum(lp[2:3] * lp[3:4], axis=-1, keepdims=True)) + lam_init)
    k = k_ref[...]
    if latent:
        k = _rope(k, cos_ref[...], sin_ref[...], D_HD)
    keys = [k.astype(BF16)]
    vals = [v_ref[...].astype(BF16)]
    if latent:
        keys.append(ck_ref[0, 0].astype(BF16))
        vals.append(cv_ref[0, 0].astype(BF16))
    lane = lax.broadcasted_iota(jnp.int32, (Q_TILE, 2 * D_HD), 1)
    for qt in range(L // Q_TILE):
        rows = slice(qt * Q_TILE, (qt + 1) * Q_TILE)
        q = q_ref[rows, :]
        if latent:
            q = _rope(q, cos_ref[rows, :], sin_ref[rows, :], D_HD)
        q = q * D_HD ** -0.5
        outs = []
        for p in range(2):
            qp = jnp.where((lane >= D_HD) == (p == 1), q, 0.0).astype(BF16)
            s = [_dot_nt(qp, kk) for kk in keys]
            m = s[0].max(axis=-1, keepdims=True)
            for x in s[1:]:
                m = jnp.maximum(m, x.max(axis=-1, keepdims=True))
            e = [jnp.exp(x - m) for x in s]
            z = sum(x.sum(axis=-1, keepdims=True) for x in e)
            pv = sum(_dot(x.astype(BF16), vv) for x, vv in zip(e, vals))
            outs.append(pv / z)
        o = outs[0] - lam * outs[1]
        y = o * lax.rsqrt(jnp.mean(o * o, axis=-1, keepdims=True) + EPS) * nw_ref[...]
        o_ref[rows, :] = (y * (1.0 - lam_init)).astype(o_ref.dtype)


def _diff_attn(proj, lam_p, nw, li, seq_len, n_seq, tok_off, cache_k=None, cache_v=None):
    latent = cache_k is not None
    blk0 = tok_off // seq_len
    w = 2 * D_HD
    in_specs = [pl.BlockSpec((seq_len, w), lambda s, h: (blk0 + s, C_DQ // w + h)),
                pl.BlockSpec((seq_len, w), lambda s, h: (blk0 + s, C_DK // w + h)),
                pl.BlockSpec((seq_len, w), lambda s, h: (blk0 + s, C_DV // w + h)),
                pl.BlockSpec((4, D_HD), lambda s, h: (0, 0)),
                pl.BlockSpec((1, w), lambda s, h: (0, 0))]
    args = [proj, proj, proj, lam_p, nw.reshape(1, w)]
    if latent:
        cos, sin = _rope_tables(seq_len, D_HD)
        in_specs += [pl.BlockSpec((seq_len, 128), lambda s, h: (0, 0)),
                     pl.BlockSpec((seq_len, 128), lambda s, h: (0, 0)),
                     pl.BlockSpec((1, 1, PAST_LEN, w), lambda s, h: (s, li, 0, h)),
                     pl.BlockSpec((1, 1, PAST_LEN, w), lambda s, h: (s, li, 0, h))]
        args += [cos, sin, cache_k.reshape(DEC_BATCH, DEPTH, PAST_LEN, D_HEADS * w),
                 cache_v.reshape(DEC_BATCH, DEPTH, PAST_LEN, D_HEADS * w)]
    lam_init = 0.8 - 0.6 * math.exp(-0.3 * li)
    return pl.pallas_call(
        functools.partial(_diff_attn_kernel, seq_len=seq_len, latent=latent, lam_init=lam_init),
        out_shape=jax.ShapeDtypeStruct((n_seq * seq_len, BRANCH_W), BF16),
        grid=(n_seq, D_HEADS),
        in_specs=in_specs,
        out_specs=pl.BlockSpec((seq_len, w), lambda s, h: (s, h)),
        compiler_params=_cparams(("arbitrary", "arbitrary")),
        name="diff_attn",
    )(*args)


def _rmsnorm(x, w):
    return x * lax.rsqrt(jnp.mean(x * x, axis=-1, keepdims=True) + EPS) * w


def _l2norm(x):
    return x * lax.rsqrt(jnp.sum(x * x, axis=-1, keepdims=True) + EPS)


def _flip(x):
    return jnp.flip(x, axis=1)


def _axial_rope(x):
    L, dim = x.shape[1], x.shape[-1]
    n_rows = L // GRID_W
    row = jnp.repeat(jnp.arange(n_rows), GRID_W).astype(F32)
    col = jnp.tile(jnp.arange(GRID_W), n_rows).astype(F32)
    nf = dim // 4
    inv = jnp.power(ROPE_BASE, -jnp.arange(nf, dtype=F32) / nf)
    ang_r = row[:, None] * inv
    ang_c = col[:, None] * inv
    ang = jnp.concatenate([ang_r, ang_r, ang_c, ang_c], axis=-1)
    shape = (1, L) + (1,) * (x.ndim - 3) + (dim,)
    cos = jnp.cos(ang).reshape(shape)
    sin = jnp.sin(ang).reshape(shape)
    x_r1, x_r2, x_c1, x_c2 = jnp.split(x, 4, axis=-1)
    rot = jnp.concatenate([-x_r2, x_r1, -x_c2, x_c1], axis=-1)
    return x * cos + rot * sin


def _to_chunks(x):
    B, L, H = x.shape[:3]
    x = x.reshape((B, L // CHUNK, CHUNK, H) + x.shape[3:])
    return jnp.moveaxis(x, 3, 1)


def _from_chunks(x):
    x = jnp.moveaxis(x, 1, 3)
    B, N, C, H = x.shape[:4]
    return x.reshape((B, N * C, H) + x.shape[4:])


def _retention_scan(q, k, v, log_gamma, s0):
    q = _to_chunks(q * R_DK ** -0.5)
    k = _to_chunks(k)
    v = _to_chunks(v)
    pos = jnp.arange(CHUNK, dtype=F32)
    lg = log_gamma[:, None]
    dist = pos[:, None] - pos[None, :]
    intra = jnp.exp(jnp.where(dist >= 0, lg[:, :, None] * dist, -jnp.inf))
    q_dec = jnp.exp(lg * (pos + 1.0))[:, :, None]
    k_dec = jnp.exp(lg * (CHUNK - 1.0 - pos))[:, :, None]
    c_dec = jnp.exp(lg * CHUNK)[:, :, None]

    def step(s, xs):
        qc, kc, vc = xs
        a = jnp.einsum('bhqd,bhkd->bhqk', qc, kc) * intra
        o = jnp.einsum('bhqk,bhkv->bhqv', a, vc) + jnp.einsum('bhqd,bhdv->bhqv', qc, s) * q_dec
        s = s * c_dec + jnp.einsum('bhkd,bhkv->bhdv', kc * k_dec, vc)
        return s, o

    xs = (jnp.moveaxis(q, 2, 0), jnp.moveaxis(k, 2, 0), jnp.moveaxis(v, 2, 0))
    s, o = lax.scan(step, s0, xs)
    return _from_chunks(jnp.moveaxis(o, 0, 2)), s


def _gdn_scan(q, k, v, g, beta, s0):
    q = _to_chunks(q * G_DK ** -0.5)
    k = _to_chunks(k)
    v = _to_chunks(v)
    g = jnp.cumsum(_to_chunks(g), axis=-1)
    beta = _to_chunks(beta)
    incl = jnp.tril(jnp.ones((CHUNK, CHUNK), dtype=bool))
    strict = jnp.tril(jnp.ones((CHUNK, CHUNK), dtype=bool), -1)
    decay = jnp.exp(jnp.where(incl, g[..., :, None] - g[..., None, :], -jnp.inf))
    kb = k * beta[..., None]
    a = jnp.where(strict, jnp.einsum('bhnid,bhnjd->bhnij', kb, k) * decay, 0.0)
    eye = jnp.eye(CHUNK, dtype=F32)
    t = lax.linalg.triangular_solve(a + eye, jnp.broadcast_to(eye, a.shape), left_side=True, lower=True)
    u = t @ (v * beta[..., None])
    w = t @ (kb * jnp.exp(g)[..., None])
    qk = jnp.einsum('bhnid,bhnjd->bhnij', q, k) * decay
    qg = q * jnp.exp(g)[..., None]
    kd = k * jnp.exp(g[..., -1:] - g)[..., None]
    gl = jnp.exp(g[..., -1])[..., None, None]

    def step(s, xs):
        qg_c, kd_c, u_c, w_c, qk_c, gl_c = xs
        u_new = u_c - w_c @ s
        o = qg_c @ s + qk_c @ u_new
        s = s * gl_c + jnp.swapaxes(kd_c, -1, -2) @ u_new
        return s, o

    xs = tuple(jnp.moveaxis(arr, 2, 0) for arr in (qg, kd, u, w, qk, gl))
    s, o = lax.scan(step, s0, xs)
    return _from_chunks(jnp.moveaxis(o, 0, 2)), s


def _diff_attention(q, k, v, lam):
    scale = D_HD ** -0.5
    s = jnp.einsum('bqhpd,bkhpd->bhpqk', q, k) * scale
    p = jax.nn.softmax(s, axis=-1)
    wts = p[:, :, 0] - lam * p[:, :, 1]
    return jnp.einsum('bhqk,bkhe->bqhe', wts, v)


def _centred_dwconv(x, w):
    pad = (w.shape[0] - 1) // 2
    return lax.conv_general_dilated(x, w[:, None, :], window_strides=(1,), padding=[(pad, pad)],
                                    dimension_numbers=('NWC', 'WIO', 'NWC'),
                                    feature_group_count=x.shape[-1])


def _mixers_jax(proj, gab, lw, li, ctx, with_gdn=True):
    B, L, _ = proj.shape
    latent = ctx is not None
    r_q = proj[..., C_RQ:C_RK].reshape(B, L, R_HEADS, R_DK)
    r_k = proj[..., C_RK:C_RV].reshape(B, L, R_HEADS, R_DK)
    r_v = proj[..., C_RV:C_RG].reshape(B, L, R_HEADS, R_DV)
    r_g = proj[..., C_RG:C_DQ]
    if latent:
        r_q, r_k = _axial_rope(r_q), _axial_rope(r_k)
        r0_f, r0_b = ctx['ret'][:, 0], ctx['ret'][:, 1]
    else:
        r0_f = r0_b = jnp.zeros((B, R_HEADS, R_DK, R_DV), F32)
    log_gamma = jax.nn.log_sigmoid(lw['ret_decay'])
    or_f, sr_f = _retention_scan(r_q, r_k, r_v, log_gamma[0], r0_f)
    or_b, sr_b = _retention_scan(_flip(r_q), _flip(r_k), _flip(r_v), log_gamma[1], r0_b)
    o_ret = _rmsnorm(or_f + _flip(or_b), lw['ret_norm']).reshape(B, L, BRANCH_W) * jax.nn.silu(r_g)

    d_q = proj[..., C_DQ:C_DK].reshape(B, L, D_HEADS, 2, D_HD)
    d_k = proj[..., C_DK:C_DV].reshape(B, L, D_HEADS, 2, D_HD)
    d_v = proj[..., C_DV:C_GQKV].reshape(B, L, D_HEADS, 2 * D_HD)
    if latent:
        q_att = _axial_rope(d_q)
        k_all = jnp.concatenate([_axial_rope(d_k), ctx['k']], axis=1)
        v_all = jnp.concatenate([d_v, ctx['v']], axis=1)
    else:
        q_att, k_all, v_all = d_q, d_k, d_v
    lam_init = 0.8 - 0.6 * math.exp(-0.3 * li)
    lp = lw['diff_lambda']
    lam = jnp.exp(jnp.sum(lp[0] * lp[1])) - jnp.exp(jnp.sum(lp[2] * lp[3])) + lam_init
    o_diff = _diff_attention(q_att, k_all, v_all, lam)
    o_diff = (_rmsnorm(o_diff, lw['diff_norm']) * (1.0 - lam_init)).reshape(B, L, BRANCH_W)
    if not with_gdn:
        return o_ret, o_diff, None, (d_k, d_v, jnp.stack([sr_f, sr_b], axis=1), None)

    g_qkv = jax.nn.silu(_centred_dwconv(proj[..., C_GQKV:C_GZ], lw['gdn_conv']))
    g_q, g_k, g_v = jnp.split(g_qkv, [G_HEADS * G_DK, 2 * G_HEADS * G_DK], axis=-1)
    g_q = _l2norm(g_q.reshape(B, L, G_HEADS, G_DK))
    g_k = _l2norm(g_k.reshape(B, L, G_HEADS, G_DK))
    g_v = g_v.reshape(B, L, G_HEADS, G_DV)
    g_z = proj[..., C_GZ:C_MG]
    g_a = gab[..., :16].reshape(B, L, 2, G_HEADS)
    g_b = gab[..., 16:32].reshape(B, L, 2, G_HEADS)
    g_log = -jnp.exp(lw['gdn_A_log']) * jax.nn.softplus(g_a + lw['gdn_dt_bias'])
    g_beta = jax.nn.sigmoid(g_b)
    if latent:
        g0_f, g0_b = ctx['gdn'][:, 0], ctx['gdn'][:, 1]
    else:
        g0_f = g0_b = jnp.zeros((B, G_HEADS, G_DK, G_DV), F32)
    og_f, sg_f = _gdn_scan(g_q, g_k, g_v, g_log[:, :, 0], g_beta[:, :, 0], g0_f)
    og_b, sg_b = _gdn_scan(_flip(g_q), _flip(g_k), _flip(g_v), _flip(g_log[:, :, 1]), _flip(g_beta[:, :, 1]), g0_b)
    o_gdn = _rmsnorm(og_f + _flip(og_b), lw['gdn_norm']) * jax.nn.silu(g_z.reshape(B, L, G_HEADS, G_DV))
    o_gdn = o_gdn.reshape(B, L, BRANCH_W)
    state = (d_k, d_v, jnp.stack([sr_f, sr_b], axis=1), jnp.stack([sg_f, sg_b], axis=1))
    return o_ret, o_diff, o_gdn, state


def _peer_jax(h, w_q, sub_keys, exp_u, exp_v):
    hb = h.reshape(N_TOK // TOK_BLOCK, TOK_BLOCK, D_MODEL)

    def block(hx):
        q = jnp.dot(hx, w_q, preferred_element_type=F32).reshape(TOK_BLOCK, PEER_HEADS, 2, PEER_DQ // 2)
        s = jnp.einsum('thpd,pnd->thpn', q, sub_keys)
        s_top, i_top = lax.top_k(s, PEER_TOPK)
        cand_s = (s_top[:, :, 0, :, None] + s_top[:, :, 1, None, :]).reshape(TOK_BLOCK, PEER_HEADS, -1)
        cand_i = (i_top[:, :, 0, :, None] * N_KEYS + i_top[:, :, 1, None, :]).reshape(TOK_BLOCK, PEER_HEADS, -1)
        best_s, pos = lax.top_k(cand_s, PEER_TOPK)
        idx = jnp.take_along_axis(cand_i, pos, axis=-1)
        gate = jax.nn.softmax(best_s, axis=-1)
        hf = hx.astype(F32)
        act = jax.nn.gelu(jnp.einsum('thkd,td->thk', exp_u[idx], hf), approximate=False)
        return jnp.einsum('thk,thkd->td', gate * act, exp_v[idx])

    return lax.map(block, hb).reshape(N_TOK, D_MODEL)


def _per_block(rows):
    idx = np.concatenate([np.zeros(N_CTX // MOD_BLK, np.int32)] +
                         [np.full(DEC_SEQ // MOD_BLK, 1 + b, np.int32) for b in range(DEC_BATCH)])
    return rows[idx][:, None, :]


def kernel(x_prompt, x_sample, cache_diff_k, cache_diff_v, state_ret, state_gdn, c, c_ctx,
           norm1_w, norm2_w, w_mod, b_mod, w_in, ret_decay, ret_norm, diff_lambda, diff_norm,
           gdn_conv, gdn_A_log, gdn_dt_bias, gdn_norm, w_branch, w_out,
           peer_wq, peer_keys, peer_u, peer_v, norm_f_w):
    x = jnp.concatenate([x_prompt.reshape(N_CTX, D_MODEL), x_sample.reshape(N_LAT, D_MODEL)], axis=0)
    cond = jnp.concatenate([c_ctx[None, :], c, jnp.zeros((8 - N_COND, D_MODEL), F32)], axis=0)
    mod_all = _modulation(cond, w_mod, b_mod)

    w_main = jnp.concatenate([w_in[:, :, :AB_OFF], w_in[:, :, AB_OFF + 32:]], axis=-1).astype(BF16)
    w_ab = jnp.pad(w_in[:, :, AB_OFF:AB_OFF + 32], ((0, 0), (0, 0), (0, 96))).astype(BF16)
    wb_bf = w_branch.astype(BF16)
    wo_bf = w_out.astype(BF16)
    wq_bf = peer_wq.astype(BF16)

    ks, vs, srs, sgs = [], [], [], []
    for li in range(DEPTH):
        lw = dict(ret_decay=ret_decay[li], ret_norm=ret_norm[li], diff_lambda=diff_lambda[li],
                  diff_norm=diff_norm[li], gdn_conv=gdn_conv[li], gdn_A_log=gdn_A_log[li],
                  gdn_dt_bias=gdn_dt_bias[li], gdn_norm=gdn_norm[li])
        mod = mod_all[li, :N_COND]
        sh1, sc1, g1, sh2, sc2, g2 = [_per_block(m) for m in jnp.split(mod, 6, axis=-1)]
        h = _normmod(x, norm1_w[li], sc1, sh1, BF16)
        proj = _matmul(h, w_main[li], 512, 1024)
        gab = _matmul(h, w_ab[li], 512, 128)

        oa, sr_l = _retention(proj, ret_decay[li], ret_norm[li], SEQ, BATCH, 0)
        za, _ = _retention(proj, ret_decay[li], ret_norm[li], DEC_SEQ, DEC_BATCH, N_CTX, s0=state_ret[:, li])
        ob = _diff_attn(proj, diff_lambda[li], diff_norm[li], li, SEQ, BATCH, 0)
        zb = _diff_attn(proj, diff_lambda[li], diff_norm[li], li, DEC_SEQ, DEC_BATCH, N_CTX, cache_diff_k, cache_diff_v)
        k_l = proj[:N_CTX, C_DK:C_DV].reshape(BATCH, SEQ, D_HEADS, 2, D_HD)
        v_l = proj[:N_CTX, C_DV:C_GQKV].reshape(BATCH, SEQ, D_HEADS, 2 * D_HD)

        gcol = gab[:, :32].reshape(N_TOK, 2, 2, G_HEADS).transpose(1, 3, 2, 0).reshape(32, N_TOK, 1)
        alog = gdn_A_log[li].T.reshape(2 * G_HEADS, 1, 1)
        dtb = gdn_dt_bias[li].T.reshape(2 * G_HEADS, 1, 1)
        qkv_c = _gdn_prep(proj, gdn_conv[li], SEQ, BATCH, 0)
        oc, sg_l = _gdn(qkv_c, proj, gcol, alog, dtb, gdn_norm[li], None, SEQ, BATCH, 0)
        qkv_z = _gdn_prep(proj, gdn_conv[li], DEC_SEQ, DEC_BATCH, N_CTX)
        zc, _ = _gdn(qkv_z, proj, gcol, alog, dtb, gdn_norm[li], state_gdn[:, li], DEC_SEQ, DEC_BATCH, N_CTX)
        ks.append(k_l); vs.append(v_l); srs.append(sr_l); sgs.append(sg_l)
        cat = lambda a, b: jnp.concatenate([a, b], axis=0)
        o_ret, o_diff, o_gdn = cat(oa, za), cat(ob, zb), cat(oc, zc)

        x, h2 = _merge(o_ret, o_diff, o_gdn, proj, wb_bf[li], wo_bf[li], x, g1, norm2_w[li], sc2, sh2)
        idx, gate = _peer_route(h2, wq_bf[li], peer_keys[li])
        x = _peer_experts(h2, idx, gate, peer_u[li].T.astype(BF16), peer_v[li].astype(BF16), x, g2)

    zeros_blk = jnp.zeros((N_MOD_BLK, 1, D_MODEL), F32)
    y = _normmod(x, norm_f_w, zeros_blk, zeros_blk, F32)
    y_prompt = y[:N_CTX].reshape(BATCH, SEQ, D_MODEL)
    y_sample = y[N_CTX:].reshape(DEC_BATCH, DEC_SEQ, D_MODEL)
    return (y_prompt, y_sample, jnp.stack(ks, axis=1), jnp.stack(vs, axis=1),
            jnp.stack(srs, axis=1), jnp.stack(sgs, axis=1))
```

```python
import functools
import math

import numpy as np
import jax
import jax.numpy as jnp
from jax import lax
from jax.experimental import pallas as pl
from jax.experimental.pallas import tpu as pltpu

F32 = jnp.float32
BF16 = jnp.bfloat16

D_MODEL = 1024
BATCH = 16
SEQ = 256
DEPTH = 2
DEC_BATCH = 2
DEC_SEQ = 1024
PAST_LEN = 256
GRID_W = 64
ROPE_BASE = 10000.0
EPS = 1e-6
CHUNK = 64
R_HEADS, R_DK, R_DV = 4, 128, 256
D_HEADS, D_HD = 8, 64
G_HEADS, G_DK, G_DV = 8, 128, 128
CONV_K = 7
G_QKV = G_HEADS * (2 * G_DK + G_DV)
N_BRANCH = 3
BRANCH_W = 1024
PEER_HEADS = 8
PEER_DQ = 256
N_KEYS = 128
PEER_TOPK = 16
TOK_BLOCK = 128

N_CTX = BATCH * SEQ
N_LAT = DEC_BATCH * DEC_SEQ
N_TOK = N_CTX + N_LAT
N_COND = 1 + DEC_BATCH
MOD_BLK = 256
N_MOD_BLK = N_TOK // MOD_BLK

C_RQ, C_RK, C_RV, C_RG = 0, 512, 1024, 2048
C_DQ, C_DK, C_DV = 3072, 4096, 5120
C_GQKV, C_GZ, C_MG = 6144, 9216, 10240
N_MAIN = 13312
AB_OFF = 10240

VMEM_LIMIT = 56 * 1024 * 1024


def _cparams(sem):
    return pltpu.CompilerParams(dimension_semantics=sem, vmem_limit_bytes=VMEM_LIMIT)


def _mod_kernel(c_ref, w_ref, b_ref, o_ref):
    c = c_ref[...]
    a = c * jax.nn.sigmoid(c)
    o_ref[0] = jnp.dot(a, w_ref[0], preferred_element_type=F32,
                       precision=lax.Precision.HIGHEST) + b_ref[0]


def _modulation(cond_pad, w_mod, b_mod):
    tn = 1536
    return pl.pallas_call(
        _mod_kernel,
        out_shape=jax.ShapeDtypeStruct((DEPTH, 8, 6 * D_MODEL), F32),
        grid=(DEPTH, 6 * D_MODEL // tn),
        in_specs=[pl.BlockSpec((8, D_MODEL), lambda l, j: (0, 0)),
                  pl.BlockSpec((1, D_MODEL, tn), lambda l, j: (l, 0, j)),
                  pl.BlockSpec((1, 1, tn), lambda l, j: (l, 0, j))],
        out_specs=pl.BlockSpec((1, 8, tn), lambda l, j: (l, 0, j)),
        compiler_params=_cparams(("arbitrary", "arbitrary")),
        name="modulation",
    )(cond_pad, w_mod, b_mod.reshape(DEPTH, 1, 6 * D_MODEL))


def _normmod_kernel(x_ref, w_ref, sc_ref, sh_ref, o_ref):
    x = x_ref[...]
    y = x * lax.rsqrt(jnp.mean(x * x, axis=-1, keepdims=True) + EPS)
    y = y * w_ref[...]
    o_ref[...] = (y * (1.0 + sc_ref[0]) + sh_ref[0]).astype(o_ref.dtype)


def _normmod(x, w, sc_blk, sh_blk, out_dtype):
    tm = MOD_BLK
    return pl.pallas_call(
        _normmod_kernel,
        out_shape=jax.ShapeDtypeStruct((N_TOK, D_MODEL), out_dtype),
        grid=(N_TOK // tm,),
        in_specs=[pl.BlockSpec((tm, D_MODEL), lambda i: (i, 0)),
                  pl.BlockSpec((1, D_MODEL), lambda i: (0, 0)),
                  pl.BlockSpec((1, 1, D_MODEL), lambda i: (i, 0, 0)),
                  pl.BlockSpec((1, 1, D_MODEL), lambda i: (i, 0, 0))],
        out_specs=pl.BlockSpec((tm, D_MODEL), lambda i: (i, 0)),
        compiler_params=_cparams(("arbitrary",)),
        name="normmod",
    )(x, w.reshape(1, D_MODEL), sc_blk, sh_blk)


def _matmul_kernel(a_ref, b_ref, o_ref):
    o_ref[...] = jnp.dot(a_ref[...], b_ref[...], preferred_element_type=F32)


def _matmul(a, b, tm, tn):
    m, k = a.shape
    n = b.shape[1]
    return pl.pallas_call(
        _matmul_kernel,
        out_shape=jax.ShapeDtypeStruct((m, n), F32),
        grid=(n // tn, m // tm),
        in_specs=[pl.BlockSpec((tm, k), lambda j, i: (i, 0)),
                  pl.BlockSpec((k, tn), lambda j, i: (0, j))],
        out_specs=pl.BlockSpec((tm, tn), lambda j, i: (i, j)),
        compiler_params=_cparams(("arbitrary", "arbitrary")),
        name="matmul",
    )(a, b)


def _merge_kernel(oa_ref, ob_ref, oc_ref, mga_ref, mgb_ref, mgc_ref, wb_ref, wo_ref, x_ref, g1_ref,
                  n2_ref, sc_ref, sh_ref, xo_ref, h_ref):
    mix = None
    for n, (o_ref, mg_ref) in enumerate(((oa_ref, mga_ref), (ob_ref, mgb_ref), (oc_ref, mgc_ref))):
        merged = jnp.dot(o_ref[...], wb_ref[n], preferred_element_type=F32)
        gate = jax.nn.sigmoid(mg_ref[...])
        mix = gate * merged if mix is None else mix + gate * merged
    m = jnp.dot(mix.astype(BF16), wo_ref[...], preferred_element_type=F32)
    x = x_ref[...] + g1_ref[0] * m
    xo_ref[...] = x
    y = x * lax.rsqrt(jnp.mean(x * x, axis=-1, keepdims=True) + EPS)
    y = y * n2_ref[...]
    h_ref[...] = (y * (1.0 + sc_ref[0]) + sh_ref[0]).astype(h_ref.dtype)


def _merge(o_ret, o_diff, o_gdn, proj, wb, wo, x, g1_blk, n2w, sc2_blk, sh2_blk):
    tm = MOD_BLK
    tok = lambda i: (i, 0)
    blk = lambda i: (i, 0, 0)
    return pl.pallas_call(
        _merge_kernel,
        out_shape=(jax.ShapeDtypeStruct((N_TOK, D_MODEL), F32),
                   jax.ShapeDtypeStruct((N_TOK, D_MODEL), BF16)),
        grid=(N_TOK // tm,),
        in_specs=[pl.BlockSpec((tm, BRANCH_W), tok),
                  pl.BlockSpec((tm, BRANCH_W), tok),
                  pl.BlockSpec((tm, BRANCH_W), tok),
                  pl.BlockSpec((tm, D_MODEL), lambda i: (i, C_MG // D_MODEL)),
                  pl.BlockSpec((tm, D_MODEL), lambda i: (i, C_MG // D_MODEL + 1)),
                  pl.BlockSpec((tm, D_MODEL), lambda i: (i, C_MG // D_MODEL + 2)),
                  pl.BlockSpec((N_BRANCH, BRANCH_W, D_MODEL), lambda i: (0, 0, 0)),
                  pl.BlockSpec((D_MODEL, D_MODEL), lambda i: (0, 0)),
                  pl.BlockSpec((tm, D_MODEL), tok),
                  pl.BlockSpec((1, 1, D_MODEL), blk),
                  pl.BlockSpec((1, D_MODEL), lambda i: (0, 0)),
                  pl.BlockSpec((1, 1, D_MODEL), blk),
                  pl.BlockSpec((1, 1, D_MODEL), blk)],
        out_specs=(pl.BlockSpec((tm, D_MODEL), tok), pl.BlockSpec((tm, D_MODEL), tok)),
        compiler_params=_cparams(("arbitrary",)),
        name="merge",
    )(o_ret, o_diff, o_gdn, proj, proj, proj, wb, wo, x, g1_blk, n2w.reshape(1, D_MODEL), sc2_blk, sh2_blk)


ROUTE_T = 128
ROUTE_HEADS = 2
N_PAIR = PEER_HEADS * PEER_TOPK
N_CAND = PEER_TOPK + (PEER_TOPK // 2) * (PEER_TOPK // 2 - 1) + PEER_TOPK // 2
NEG_INF = float("-inf")


def _split_bf16(x):
    hi = x.astype(BF16)
    return hi, (x - hi.astype(F32)).astype(BF16)


def _topk_rows(scores, rows, n_rows, payloads=None):
    scores = list(scores)
    vals = [[] for _ in scores]
    outs = [[] for _ in scores]
    for _ in range(PEER_TOPK):
        for n, s in enumerate(scores):
            m = jnp.max(s, axis=0, keepdims=True)
            pos = jnp.min(jnp.where(s == m, rows, n_rows), axis=0, keepdims=True)
            sel = rows == pos
            vals[n].append(m)
            if payloads is None:
                outs[n].append(pos)
            else:
                outs[n].append(jnp.sum(jnp.where(sel, payloads[n], 0), axis=0, keepdims=True))
            scores[n] = jnp.where(sel, NEG_INF, s)
    return [(jnp.concatenate(v, axis=0), jnp.concatenate(o, axis=0)) for v, o in zip(vals, outs)]


def _route_kernel(h_ref, wq_ref, khi_ref, klo_ref, idx_ref, gate_ref, q_scr, idx_scr, gate_scr):
    t = h_ref.shape[0]
    q = jnp.dot(h_ref[...], wq_ref[...], preferred_element_type=F32)
    for hp in range(2 * PEER_HEADS):
        q_scr[hp] = q[:, hp * 128:(hp + 1) * 128]
    rows_k = lax.broadcasted_iota(jnp.int32, (N_KEYS, t), 0)
    rows_c = lax.broadcasted_iota(jnp.int32, (N_CAND, t), 0)
    nt = (((1,), (1,)), ((), ()))

    half = PEER_TOPK // 2
    groups = [(slice(0, 1), slice(0, PEER_TOPK)), (slice(1, 2), slice(0, half))]
    groups += [(slice(k, k + 1), slice(0, half)) for k in range(2, half)]
    groups += [(slice(half, PEER_TOPK), slice(0, 1))]

    def heads(hh, carry):
        hs = [ROUTE_HEADS * hh + n for n in range(ROUTE_HEADS)]
        scores = []
        for h in hs:
            for p in range(2):
                qhi, qlo = _split_bf16(q_scr[2 * h + p])
                scores.append(lax.dot_general(khi_ref[p], qhi, nt, preferred_element_type=F32)
                              + lax.dot_general(khi_ref[p], qlo, nt, preferred_element_type=F32)
                              + lax.dot_general(klo_ref[p], qhi, nt, preferred_element_type=F32))
        tops = _topk_rows(scores, rows_k, N_KEYS)
        cands, cand_is = [], []
        for n in range(ROUTE_HEADS):
            (v1, i1), (v2, i2) = tops[2 * n], tops[2 * n + 1]
            cands.append(jnp.concatenate([v1[a] + v2[b] for a, b in groups], axis=0))
            cand_is.append(jnp.concatenate([i1[a] * N_KEYS + i2[b] for a, b in groups], axis=0))
        for h, (best, bidx) in zip(hs, _topk_rows(cands, rows_c, N_CAND, payloads=cand_is)):
            e = jnp.exp(best - best[0:1])
            gate_scr[h] = e / jnp.sum(e, axis=0, keepdims=True)
            idx_scr[h] = bidx
        return carry

    lax.fori_loop(0, PEER_HEADS // ROUTE_HEADS, heads, 0)
    idx_ref[...] = idx_scr[...].reshape(N_PAIR, t).T
    gate_ref[...] = gate_scr[...].reshape(N_PAIR, t).T


def _peer_route(h2, wq, keys):
    khi, klo = _split_bf16(keys)
    t = ROUTE_T
    n_tok = h2.shape[0]
    return pl.pallas_call(
        _route_kernel,
        out_shape=(jax.ShapeDtypeStruct((n_tok, N_PAIR), jnp.int32),
                   jax.ShapeDtypeStruct((n_tok, N_PAIR), F32)),
        grid=(n_tok // t,),
        in_specs=[pl.BlockSpec((t, D_MODEL), lambda i: (i, 0)),
                  pl.BlockSpec((D_MODEL, PEER_HEADS * PEER_DQ), lambda i: (0, 0)),
                  pl.BlockSpec((2, N_KEYS, PEER_DQ // 2), lambda i: (0, 0, 0)),
                  pl.BlockSpec((2, N_KEYS, PEER_DQ // 2), lambda i: (0, 0, 0))],
        out_specs=(pl.BlockSpec((t, N_PAIR), lambda i: (i, 0)),
                   pl.BlockSpec((t, N_PAIR), lambda i: (i, 0))),
        scratch_shapes=[pltpu.VMEM((2 * PEER_HEADS, t, PEER_DQ // 2), F32),
                        pltpu.VMEM((PEER_HEADS, PEER_TOPK, t), jnp.int32),
                        pltpu.VMEM((PEER_HEADS, PEER_TOPK, t), F32)],
        compiler_params=_cparams(("arbitrary",)),
        name="peer_route",
    )(h2, wq, khi, klo)


EXP_TB = 256
EXP_EB = 1024
EXP_SUB = 256
N_EXPERTS = N_KEYS * N_KEYS


def _expert_kernel(h_ref, idx_ref, gate_ref, u_ref, v_ref, x_ref, g2_ref, o_ref, g_scr, acc_ref):
    j = pl.program_id(1)
    tb = h_ref.shape[0]

    @pl.when(j == 0)
    def _():
        acc_ref[...] = jnp.zeros_like(acc_ref)
        sub = lax.broadcasted_iota(jnp.int32, (N_KEYS, N_PAIR), 0)

        def body(t, carry):
            e = idx_ref[pl.ds(t, 1), :]
            g = gate_ref[pl.ds(t, 1), :]
            xa = jnp.where(sub == (e >> 7), 1.0, 0.0).astype(BF16)
            yb = jnp.where(sub == (e & (N_KEYS - 1)), g, 0.0).astype(BF16)
            g_scr[pl.ds(pl.multiple_of(t * N_KEYS, N_KEYS), N_KEYS), :] = _dot_nt(xa, yb)
            return carry

        lax.fori_loop(0, tb, body, 0, unroll=16)

    h = h_ref[...]
    nsub = EXP_EB // EXP_SUB
    per = EXP_SUB // N_KEYS
    s = [_dot_nt(h, u_ref[c * EXP_SUB:(c + 1) * EXP_SUB, :]) for c in range(nsub)]
    acc = acc_ref[...]
    for c in range(nsub):
        gj = jnp.concatenate([g_scr[pl.ds(j * (nsub * per) + c * per + i, tb, stride=N_KEYS), :]
                              for i in range(per)], axis=1)
        w = 0.5 * s[c] * (1.0 + lax.erf(s[c] * (2.0 ** -0.5))) * gj
        acc = acc + jnp.dot(w.astype(BF16), v_ref[c * EXP_SUB:(c + 1) * EXP_SUB, :], preferred_element_type=F32)
    acc_ref[...] = acc

    @pl.when(j == pl.num_programs(1) - 1)
    def _():
        o_ref[...] = x_ref[...] + g2_ref[0] * acc_ref[...]


def _peer_experts(h2, idx, gate, u, v, x, g2_blk):
    tb, eb = EXP_TB, EXP_EB
    n_tok = h2.shape[0]
    return pl.pallas_call(
        _expert_kernel,
        out_shape=jax.ShapeDtypeStruct((n_tok, D_MODEL), F32),
        grid=(n_tok // tb, N_EXPERTS // eb),
        in_specs=[pl.BlockSpec((tb, D_MODEL), lambda i, j: (i, 0)),
                  pl.BlockSpec((tb, N_PAIR), lambda i, j: (i, 0)),
                  pl.BlockSpec((tb, N_PAIR), lambda i, j: (i, 0)),
                  pl.BlockSpec((eb, D_MODEL), lambda i, j: (j, 0)),
                  pl.BlockSpec((eb, D_MODEL), lambda i, j: (j, 0)),
                  pl.BlockSpec((tb, D_MODEL), lambda i, j: (i, 0)),
                  pl.BlockSpec((1, 1, D_MODEL), lambda i, j: (i * tb // MOD_BLK, 0, 0))],
        out_specs=pl.BlockSpec((tb, D_MODEL), lambda i, j: (i, 0)),
        scratch_shapes=[pltpu.VMEM((tb * N_KEYS, N_KEYS), F32),
                        pltpu.VMEM((tb, D_MODEL), F32)],
        compiler_params=_cparams(("arbitrary", "arbitrary")),
        name="peer_experts",
    )(h2, idx, gate, u, v, x, g2_blk)


CONV_ROWS = 256
CONV_HALO = 8


def _gdn_prep_kernel(x_ref, w_ref, o_ref):
    sec = pl.program_id(1)
    L = x_ref.shape[0]
    half = CONV_K // 2
    for r0 in range(0, L, CONV_ROWS):
        s0, s1 = max(r0 - CONV_HALO, 0), min(r0 + CONV_ROWS + CONV_HALO, L)
        n = s1 - s0
        t_idx = lax.broadcasted_iota(jnp.int32, (n, 128), 0) + s0
        for cb in range(G_HEADS):
            cols = slice(cb * 128, (cb + 1) * 128)
            x = x_ref[s0:s1, cols]
            w = w_ref[0, :, cols]
            acc = x * w[half:half + 1]
            for d in range(-half, half + 1):
                if d == 0:
                    continue
                xs = pltpu.roll(x, (-d) % n, axis=0)
                valid = (t_idx + d >= 0) if d < 0 else (t_idx + d < L)
                acc = acc + jnp.where(valid, xs, 0.0) * w[half + d:half + d + 1]
            y = acc[r0 - s0:r0 - s0 + CONV_ROWS]
            y = y * jax.nn.sigmoid(y)
            inv = lax.rsqrt(jnp.sum(y * y, axis=-1, keepdims=True) + EPS)
            o_ref[r0:r0 + CONV_ROWS, cols] = y * jnp.where(sec < 2, inv, 1.0)


def _gdn_prep(proj, conv_w, seq_len, n_seq, tok_off):
    blk0 = tok_off // seq_len
    sec0 = C_GQKV // 1024
    return pl.pallas_call(
        _gdn_prep_kernel,
        out_shape=jax.ShapeDtypeStruct((n_seq * seq_len, G_QKV), F32),
        grid=(n_seq, 3),
        in_specs=[pl.BlockSpec((seq_len, 1024), lambda s, c: (blk0 + s, sec0 + c)),
                  pl.BlockSpec((1, CONV_K, 1024), lambda s, c: (0, 0, c))],
        out_specs=pl.BlockSpec((seq_len, 1024), lambda s, c: (s, c)),
        compiler_params=_cparams(("arbitrary", "arbitrary")),
        name="gdn_prep",
    )(proj, conv_w.reshape(1, CONV_K, G_QKV))


GDN_HB = 2


def _dot(a, b):
    return jnp.dot(a, b, preferred_element_type=F32)


def _dot_nt(a, b):
    return lax.dot_general(a, b, (((1,), (1,)), ((), ())), preferred_element_type=F32)


def _dot_tn(a, b):
    return lax.dot_general(a, b, (((0,), (0,)), ((), ())), preferred_element_type=F32)


def _mask_dot3(m, b):
    m16 = m.astype(BF16)
    bh = b.astype(BF16)
    r1 = b - bh.astype(F32)
    bm = r1.astype(BF16)
    bl = (r1 - bm.astype(F32)).astype(BF16)
    return _dot(m16, bh) + _dot(m16, bm) + _dot(m16, bl)


def _gdn_kernel(*refs, seq_len, has_init, emit_state):
    it = iter(refs)
    q_ref, k_ref, v_ref, z_ref, ga_ref, gb_ref, alog_ref, dtb_ref, nw_ref = [next(it) for _ in range(9)]
    s0_ref = next(it) if has_init else None
    o_ref = next(it)
    so_ref = next(it) if emit_state else None
    u_s, w_s, qg_s, kd_s, qk_s, gl_s, o_s = [next(it) for _ in range(7)]
    C = CHUNK
    nc = seq_len // C
    ri = lax.broadcasted_iota(jnp.int32, (C, C), 0)
    ci = lax.broadcasted_iota(jnp.int32, (C, C), 1)
    eye = jnp.where(ri == ci, 1.0, 0.0)
    ones = jnp.ones((C, C), F32)
    scale = G_DK ** -0.5
    incl = (ri >= ci, ri <= ci)
    strict = (ri > ci, ri < ci)
    levels = []
    for lv in range(6):
        b = 1 << lv
        same = (ri >> (lv + 1)) == (ci >> (lv + 1))
        r_hi, c_hi = (ri & b) != 0, (ci & b) != 0
        levels.append((same & r_hi & jnp.logical_not(c_hi), same & jnp.logical_not(r_hi) & c_hi))

    GROUP = 4
    HB = GDN_HB
    lanes = [slice(hh * 128, (hh + 1) * 128) for hh in range(HB)]
    tri = [jnp.where(m, 1.0, 0.0) for m in incl]

    def prep(grp, carry):
        chunks = [grp * GROUP + j for j in range(GROUP)]
        rows = [pl.ds(pl.multiple_of(c * C, C), C) for c in chunks]
        hj = [(hh, j) for hh in range(HB) for j in range(GROUP)]
        q = {(hh, j): q_ref[rows[j], lanes[hh]] * scale for hh, j in hj}
        k = {(hh, j): k_ref[rows[j], lanes[hh]] for hh, j in hj}
        v = {(hh, j): v_ref[rows[j], lanes[hh]] for hh, j in hj}
        k16 = {key: x.astype(BF16) for key, x in k.items()}
        kk = {key: _dot_nt(x, x) for key, x in k16.items()}
        qk = {key: _dot_nt(q[key].astype(BF16), k16[key]) for key in hj}
        probs = [(hh, j, d) for hh in range(HB) for j in range(GROUP) for d in range(2)]
        g = [-jnp.exp(alog_ref[2 * hh + d]) * jax.nn.softplus(ga_ref[2 * hh + d, rows[j], :] + dtb_ref[2 * hh + d])
             for hh, j, d in probs]
        beta = [jax.nn.sigmoid(gb_ref[2 * hh + d, rows[j], :]) for hh, j, d in probs]
        gcol = [_mask_dot3(tri[d], jnp.broadcast_to(g[p], (C, 128))) for p, (hh, j, d) in enumerate(probs)]
        grow = [_mask_dot3(ones, jnp.where(incl[1 - d], jnp.broadcast_to(g[p], (C, C)), 0.0))
                for p, (hh, j, d) in enumerate(probs)]
        dec = [jnp.exp(jnp.where(incl[d], gcol[p][:, :C] - grow[p], NEG_INF)) for p, (hh, j, d) in enumerate(probs)]
        a = [jnp.where(strict[d], kk[hh, j] * beta[p] * dec[p], 0.0) for p, (hh, j, d) in enumerate(probs)]
        t = [eye - jnp.where(levels[0][d], a[p], 0.0) for p, (hh, j, d) in enumerate(probs)]
        for lv in range(1, 6):
            t16 = [x.astype(BF16) for x in t]
            ct = [_dot(jnp.where(levels[lv][d], a[p], 0.0).astype(BF16), t16[p])
                  for p, (hh, j, d) in enumerate(probs)]
            t = [t[p] - _dot(t16[p], ct[p].astype(BF16)) for p in range(len(probs))]
        for p, (hh, j, d) in enumerate(probs):
            n = 2 * hh + d
            eg = jnp.exp(gcol[p])
            t16 = t[p].astype(BF16)
            glast = gcol[p][C - 1:C, :] if d == 0 else gcol[p][0:1, :]
            u_s[n, rows[j], :] = _dot(t16, (v[hh, j] * beta[p]).astype(BF16))
            w_s[n, rows[j], :] = _dot(t16, (k[hh, j] * beta[p] * eg).astype(BF16))
            qg_s[n, rows[j], :] = q[hh, j] * eg
            kd_s[n, rows[j], :] = k[hh, j] * jnp.exp(glast - gcol[p])
            qk_s[n, rows[j], :] = qk[hh, j] * dec[p]
            gl_s[n * nc + chunks[j]] = jnp.broadcast_to(jnp.exp(glast), (8, 128))
        return carry

    lax.fori_loop(0, nc // GROUP, prep, 0)
    o_s[...] = jnp.zeros_like(o_s)
    chains = [(hh, d) for hh in range(HB) for d in range(2)]

    def step(i, states):
        cs = (i, nc - 1 - i)
        rows = [pl.ds(pl.multiple_of(c * C, C), C) for c in cs]
        s16 = [s.astype(BF16) for s in states]
        ws = [_dot(w_s[n, rows[d], :].astype(BF16), s16[n]) for n, (hh, d) in enumerate(chains)]
        un16 = [(u_s[n, rows[d], :] - ws[n]).astype(BF16) for n, (hh, d) in enumerate(chains)]
        o = [_dot(qg_s[n, rows[d], :].astype(BF16), s16[n]) + _dot(qk_s[n, rows[d], :].astype(BF16), un16[n])
             for n, (hh, d) in enumerate(chains)]
        new = tuple(states[n] * gl_s[n * nc + cs[d]][0:1, :] + _dot_tn(kd_s[n, rows[d], :].astype(BF16), un16[n])
                    for n, (hh, d) in enumerate(chains))
        for n, (hh, d) in enumerate(chains):
            o_s[rows[d], lanes[hh]] += o[n]
        return new

    if has_init:
        init = tuple(s0_ref[0, d, hh] for hh, d in chains)
    else:
        init = (jnp.zeros((G_DK, G_DV), F32),) * len(chains)
    final = lax.fori_loop(0, nc, step, init)
    if emit_state:
        for n, (hh, d) in enumerate(chains):
            so_ref[0, d, hh] = final[n]
    for hh in range(HB):
        o = o_s[:, lanes[hh]]
        y = o * lax.rsqrt(jnp.mean(o * o, axis=-1, keepdims=True) + EPS) * nw_ref[...]
        z = z_ref[:, lanes[hh]]
        o_ref[:, lanes[hh]] = (y * (z * jax.nn.sigmoid(z))).astype(o_ref.dtype)


def _gdn(qkv, proj, gcol, alog, dtb, nw, s0, seq_len, n_seq, tok_off):
    has_init = s0 is not None
    emit_state = not has_init
    blk0 = tok_off // seq_len
    hb = GDN_HB
    wl = hb * 128
    nblk = G_HEADS // hb
    tokcol = lambda off: pl.BlockSpec((seq_len, wl), lambda s, h: (s, off + h))
    in_specs = [tokcol(0), tokcol(nblk), tokcol(2 * nblk),
                pl.BlockSpec((seq_len, wl), lambda s, h: (blk0 + s, C_GZ // wl + h)),
                pl.BlockSpec((2 * hb, seq_len, 1), lambda s, h: (h, blk0 + s, 0)),
                pl.BlockSpec((2 * hb, seq_len, 1), lambda s, h: (nblk + h, blk0 + s, 0)),
                pl.BlockSpec((2 * hb, 1, 1), lambda s, h: (h, 0, 0)),
                pl.BlockSpec((2 * hb, 1, 1), lambda s, h: (h, 0, 0)),
                pl.BlockSpec((1, G_DV), lambda s, h: (0, 0))]
    args = [qkv, qkv, qkv, proj, gcol, gcol, alog, dtb, nw.reshape(1, G_DV)]
    if has_init:
        in_specs.append(pl.BlockSpec((1, 2, hb, G_DK, G_DV), lambda s, h: (s, 0, h, 0, 0)))
        args.append(s0)
    out_shape = [jax.ShapeDtypeStruct((n_seq * seq_len, BRANCH_W), BF16)]
    out_specs = [pl.BlockSpec((seq_len, wl), lambda s, h: (s, h))]
    if emit_state:
        out_shape.append(jax.ShapeDtypeStruct((n_seq, 2, G_HEADS, G_DK, G_DV), F32))
        out_specs.append(pl.BlockSpec((1, 2, hb, G_DK, G_DV), lambda s, h: (s, 0, h, 0, 0)))
    nc = seq_len // CHUNK
    res = pl.pallas_call(
        functools.partial(_gdn_kernel, seq_len=seq_len, has_init=has_init, emit_state=emit_state),
        out_shape=tuple(out_shape),
        grid=(n_seq, nblk),
        in_specs=in_specs,
        out_specs=tuple(out_specs),
        scratch_shapes=[pltpu.VMEM((2 * hb, seq_len, 128), F32)] * 4
                       + [pltpu.VMEM((2 * hb, seq_len, CHUNK), F32), pltpu.VMEM((2 * hb * nc, 8, 128), F32),
                          pltpu.VMEM((seq_len, wl), F32)],
        compiler_params=_cparams(("arbitrary", "arbitrary")),
        name="gdn",
    )(*args)
    return res if emit_state else (res[0], None)


def _rope_tables(seq_len, dim):
    n_rows = seq_len // GRID_W
    row = np.repeat(np.arange(n_rows), GRID_W).astype(np.float32)
    col = np.tile(np.arange(GRID_W), n_rows).astype(np.float32)
    nf = dim // 4
    inv = np.power(np.float32(ROPE_BASE), -np.arange(nf, dtype=np.float32) / np.float32(nf)).astype(np.float32)
    ang_r = row[:, None] * inv
    ang_c = col[:, None] * inv
    ang = np.concatenate([ang_r, ang_r, ang_c, ang_c], axis=-1)
    ang = np.tile(ang, (1, 128 // dim))
    return jnp.asarray(np.cos(ang), F32), jnp.asarray(np.sin(ang), F32)


def _rope(x, cos, sin, dim):
    quarter = dim // 4
    lane = lax.broadcasted_iota(jnp.int32, x.shape, 1)
    first = (lane & (2 * quarter - 1)) < quarter
    rot = jnp.where(first, -pltpu.roll(x, 128 - quarter, axis=1), pltpu.roll(x, quarter, axis=1))
    return x * cos + rot * sin


Q_TILE = 256


def _retention_kernel(*refs, seq_len, latent):
    it = iter(refs)
    q_ref, k_ref, v_ref, rg_ref, lg_ref, nw_ref = [next(it) for _ in range(6)]
    if latent:
        cos_ref, sin_ref, s0_ref = next(it), next(it), next(it)
    o_ref = next(it)
    so_ref = None if latent else next(it)
    L = seq_len
    lg = lg_ref[0]
    lg = jnp.minimum(lg, 0.0) - jnp.log(1.0 + jnp.exp(-jnp.abs(lg)))
    lgf, lgb = lg[0:1, :], lg[1:2, :]
    k = k_ref[...]
    if latent:
        k = _rope(k, cos_ref[...], sin_ref[...], R_DK)
    k16 = k.astype(BF16)
    v16 = v_ref[...].astype(BF16)
    for qt in range(L // Q_TILE):
        rows = slice(qt * Q_TILE, (qt + 1) * Q_TILE)
        q = q_ref[rows, :]
        if latent:
            q = _rope(q, cos_ref[rows, :], sin_ref[rows, :], R_DK)
        q16 = (q * R_DK ** -0.5).astype(BF16)
        a = _dot_nt(q16, k16)
        i = lax.broadcasted_iota(jnp.int32, (Q_TILE, L), 0) + qt * Q_TILE
        j = lax.broadcasted_iota(jnp.int32, (Q_TILE, L), 1)
        dist = (i - j).astype(F32)
        dec = (jnp.exp(jnp.where(dist >= 0, lgf * dist, NEG_INF))
               + jnp.exp(jnp.where(dist <= 0, -lgb * dist, NEG_INF)))
        o = _dot((a * dec).astype(BF16), v16)
        if latent:
            pos = (lax.broadcasted_iota(jnp.int32, (Q_TILE, 1), 0) + qt * Q_TILE).astype(F32)
            o = o + _dot(q16, s0_ref[0, 0, 0].astype(BF16)) * jnp.exp(lgf * (pos + 1.0))
            o = o + _dot(q16, s0_ref[0, 1, 0].astype(BF16)) * jnp.exp(lgb * (L - pos))
        y = o * lax.rsqrt(jnp.mean(o * o, axis=-1, keepdims=True) + EPS) * nw_ref[...]
        g = rg_ref[rows, :]
        o_ref[rows, :] = (y * (g * jax.nn.sigmoid(g))).astype(o_ref.dtype)
    if not latent:
        pos = lax.broadcasted_iota(jnp.int32, (L, 1), 0).astype(F32)
        so_ref[0, 0, 0] = _dot_tn((k * jnp.exp(lgf * (L - 1.0 - pos))).astype(BF16), v16)
        so_ref[0, 1, 0] = _dot_tn((k * jnp.exp(lgb * pos)).astype(BF16), v16)


def _retention(proj, decay, nw, seq_len, n_seq, tok_off, s0=None):
    latent = s0 is not None
    blk0 = tok_off // seq_len
    in_specs = [pl.BlockSpec((seq_len, R_DK), lambda s, h: (blk0 + s, C_RQ // R_DK + h)),
                pl.BlockSpec((seq_len, R_DK), lambda s, h: (blk0 + s, C_RK // R_DK + h)),
                pl.BlockSpec((seq_len, R_DV), lambda s, h: (blk0 + s, C_RV // R_DV + h)),
                pl.BlockSpec((seq_len, R_DV), lambda s, h: (blk0 + s, C_RG // R_DV + h)),
                pl.BlockSpec((1, 2, 1), lambda s, h: (h, 0, 0)),
                pl.BlockSpec((1, R_DV), lambda s, h: (0, 0))]
    args = [proj, proj, proj, proj, decay.T.reshape(R_HEADS, 2, 1), nw.reshape(1, R_DV)]
    out_shape = [jax.ShapeDtypeStruct((n_seq * seq_len, BRANCH_W), BF16)]
    out_specs = [pl.BlockSpec((seq_len, R_DV), lambda s, h: (s, h))]
    if latent:
        cos, sin = _rope_tables(seq_len, R_DK)
        in_specs += [pl.BlockSpec((seq_len, 128), lambda s, h: (0, 0)),
                     pl.BlockSpec((seq_len, 128), lambda s, h: (0, 0)),
                     pl.BlockSpec((1, 2, 1, R_DK, R_DV), lambda s, h: (s, 0, h, 0, 0))]
        args += [cos, sin, s0]
    else:
        out_shape.append(jax.ShapeDtypeStruct((n_seq, 2, R_HEADS, R_DK, R_DV), F32))
        out_specs.append(pl.BlockSpec((1, 2, 1, R_DK, R_DV), lambda s, h: (s, 0, h, 0, 0)))
    res = pl.pallas_call(
        functools.partial(_retention_kernel, seq_len=seq_len, latent=latent),
        out_shape=tuple(out_shape),
        grid=(n_seq, R_HEADS),
        in_specs=in_specs,
        out_specs=tuple(out_specs),
        compiler_params=_cparams(("arbitrary", "arbitrary")),
        name="retention",
    )(*args)
    return (res[0], None) if latent else res


def _diff_attn_kernel(*refs, seq_len, latent, lam_init):
    it = iter(refs)
    q_ref, k_ref, v_ref, lp_ref, nw_ref = [next(it) for _ in range(5)]
    if latent:
        cos_ref, sin_ref, ck_ref, cv_ref = [next(it) for _ in range(4)]
    o_ref = next(it)
    L = seq_len
    lp = lp_ref[...]
    lam = (jnp.exp(jnp.sum(lp[0:1] * lp[1:2], axis=-1, keepdims=True))
           - jnp.exp(jnp.sum(lp[2:3] * lp[3:4], axis=-1, keepdims=True)) + lam_init)
    k = k_ref[...]
    if latent:
        k = _rope(k, cos_ref[...], sin_ref[...], D_HD)
    keys = [k.astype(BF16)]
    vals = [v_ref[...].astype(BF16)]
    if latent:
        keys.append(ck_ref[0, 0].astype(BF16))
        vals.append(cv_ref[0, 0].astype(BF16))
    lane = lax.broadcasted_iota(jnp.int32, (Q_TILE, 2 * D_HD), 1)
    for qt in range(L // Q_TILE):
        rows = slice(qt * Q_TILE, (qt + 1) * Q_TILE)
        q = q_ref[rows, :]
        if latent:
            q = _rope(q, cos_ref[rows, :], sin_ref[rows, :], D_HD)
        q = q * D_HD ** -0.5
        outs = []
        for p in range(2):
            qp = jnp.where((lane >= D_HD) == (p == 1), q, 0.0).astype(BF16)
            s = [_dot_nt(qp, kk) for kk in keys]
            m = s[0].max(axis=-1, keepdims=True)
            for x in s[1:]:
                m = jnp.maximum(m, x.max(axis=-1, keepdims=True))
            e = [jnp.exp(x - m) for x in s]
            z = sum(x.sum(axis=-1, keepdims=True) for x in e)
            pv = sum(_dot(x.astype(BF16), vv) for x, vv in zip(e, vals))
            outs.append(pv / z)
        o = outs[0] - lam * outs[1]
        y = o * lax.rsqrt(jnp.mean(o * o, axis=-1, keepdims=True) + EPS) * nw_ref[...]
        o_ref[rows, :] = (y * (1.0 - lam_init)).astype(o_ref.dtype)


def _diff_attn(proj, lam_p, nw, li, seq_len, n_seq, tok_off, cache_k=None, cache_v=None):
    latent = cache_k is not None
    blk0 = tok_off // seq_len
    w = 2 * D_HD
    in_specs = [pl.BlockSpec((seq_len, w), lambda s, h: (blk0 + s, C_DQ // w + h)),
                pl.BlockSpec((seq_len, w), lambda s, h: (blk0 + s, C_DK // w + h)),
                pl.BlockSpec((seq_len, w), lambda s, h: (blk0 + s, C_DV // w + h)),
                pl.BlockSpec((4, D_HD), lambda s, h: (0, 0)),
                pl.BlockSpec((1, w), lambda s, h: (0, 0))]
    args = [proj, proj, proj, lam_p, nw.reshape(1, w)]
    if latent:
        cos, sin = _rope_tables(seq_len, D_HD)
        in_specs += [pl.BlockSpec((seq_len, 128), lambda s, h: (0, 0)),
                     pl.BlockSpec((seq_len, 128), lambda s, h: (0, 0)),
                     pl.BlockSpec((1, 1, PAST_LEN, w), lambda s, h: (s, li, 0, h)),
                     pl.BlockSpec((1, 1, PAST_LEN, w), lambda s, h: (s, li, 0, h))]
        args += [cos, sin, cache_k.reshape(DEC_BATCH, DEPTH, PAST_LEN, D_HEADS * w),
                 cache_v.reshape(DEC_BATCH, DEPTH, PAST_LEN, D_HEADS * w)]
    lam_init = 0.8 - 0.6 * math.exp(-0.3 * li)
    return pl.pallas_call(
        functools.partial(_diff_attn_kernel, seq_len=seq_len, latent=latent, lam_init=lam_init),
        out_shape=jax.ShapeDtypeStruct((n_seq * seq_len, BRANCH_W), BF16),
        grid=(n_seq, D_HEADS),
        in_specs=in_specs,
        out_specs=pl.BlockSpec((seq_len, w), lambda s, h: (s, h)),
        compiler_params=_cparams(("arbitrary", "arbitrary")),
        name="diff_attn",
    )(*args)


def _rmsnorm(x, w):
    return x * lax.rsqrt(jnp.mean(x * x, axis=-1, keepdims=True) + EPS) * w


def _l2norm(x):
    return x * lax.rsqrt(jnp.sum(x * x, axis=-1, keepdims=True) + EPS)


def _flip(x):
    return jnp.flip(x, axis=1)


def _axial_rope(x):
    L, dim = x.shape[1], x.shape[-1]
    n_rows = L // GRID_W
    row = jnp.repeat(jnp.arange(n_rows), GRID_W).astype(F32)
    col = jnp.tile(jnp.arange(GRID_W), n_rows).astype(F32)
    nf = dim // 4
    inv = jnp.power(ROPE_BASE, -jnp.arange(nf, dtype=F32) / nf)
    ang_r = row[:, None] * inv
    ang_c = col[:, None] * inv
    ang = jnp.concatenate([ang_r, ang_r, ang_c, ang_c], axis=-1)
    shape = (1, L) + (1,) * (x.ndim - 3) + (dim,)
    cos = jnp.cos(ang).reshape(shape)
    sin = jnp.sin(ang).reshape(shape)
    x_r1, x_r2, x_c1, x_c2 = jnp.split(x, 4, axis=-1)
    rot = jnp.concatenate([-x_r2, x_r1, -x_c2, x_c1], axis=-1)
    return x * cos + rot * sin


def _to_chunks(x):
    B, L, H = x.shape[:3]
    x = x.reshape((B, L // CHUNK, CHUNK, H) + x.shape[3:])
    return jnp.moveaxis(x, 3, 1)


def _from_chunks(x):
    x = jnp.moveaxis(x, 1, 3)
    B, N, C, H = x.shape[:4]
    return x.reshape((B, N * C, H) + x.shape[4:])


def _retention_scan(q, k, v, log_gamma, s0):
    q = _to_chunks(q * R_DK ** -0.5)
    k = _to_chunks(k)
    v = _to_chunks(v)
    pos = jnp.arange(CHUNK, dtype=F32)
    lg = log_gamma[:, None]
    dist = pos[:, None] - pos[None, :]
    intra = jnp.exp(jnp.where(dist >= 0, lg[:, :, None] * dist, -jnp.inf))
    q_dec = jnp.exp(lg * (pos + 1.0))[:, :, None]
    k_dec = jnp.exp(lg * (CHUNK - 1.0 - pos))[:, :, None]
    c_dec = jnp.exp(lg * CHUNK)[:, :, None]

    def step(s, xs):
        qc, kc, vc = xs
        a = jnp.einsum('bhqd,bhkd->bhqk', qc, kc) * intra
        o = jnp.einsum('bhqk,bhkv->bhqv', a, vc) + jnp.einsum('bhqd,bhdv->bhqv', qc, s) * q_dec
        s = s * c_dec + jnp.einsum('bhkd,bhkv->bhdv', kc * k_dec, vc)
        return s, o

    xs = (jnp.moveaxis(q, 2, 0), jnp.moveaxis(k, 2, 0), jnp.moveaxis(v, 2, 0))
    s, o = lax.scan(step, s0, xs)
    return _from_chunks(jnp.moveaxis(o, 0, 2)), s


def _gdn_scan(q, k, v, g, beta, s0):
    q = _to_chunks(q * G_DK ** -0.5)
    k = _to_chunks(k)
    v = _to_chunks(v)
    g = jnp.cumsum(_to_chunks(g), axis=-1)
    beta = _to_chunks(beta)
    incl = jnp.tril(jnp.ones((CHUNK, CHUNK), dtype=bool))
    strict = jnp.tril(jnp.ones((CHUNK, CHUNK), dtype=bool), -1)
    decay = jnp.exp(jnp.where(incl, g[..., :, None] - g[..., None, :], -jnp.inf))
    kb = k * beta[..., None]
    a = jnp.where(strict, jnp.einsum('bhnid,bhnjd->bhnij', kb, k) * decay, 0.0)
    eye = jnp.eye(CHUNK, dtype=F32)
    t = lax.linalg.triangular_solve(a + eye, jnp.broadcast_to(eye, a.shape), left_side=True, lower=True)
    u = t @ (v * beta[..., None])
    w = t @ (kb * jnp.exp(g)[..., None])
    qk = jnp.einsum('bhnid,bhnjd->bhnij', q, k) * decay
    qg = q * jnp.exp(g)[..., None]
    kd = k * jnp.exp(g[..., -1:] - g)[..., None]
    gl = jnp.exp(g[..., -1])[..., None, None]

    def step(s, xs):
        qg_c, kd_c, u_c, w_c, qk_c, gl_c = xs
        u_new = u_c - w_c @ s
        o = qg_c @ s + qk_c @ u_new
        s = s * gl_c + jnp.swapaxes(kd_c, -1, -2) @ u_new
        return s, o

    xs = tuple(jnp.moveaxis(arr, 2, 0) for arr in (qg, kd, u, w, qk, gl))
    s, o = lax.scan(step, s0, xs)
    return _from_chunks(jnp.moveaxis(o, 0, 2)), s


def _diff_attention(q, k, v, lam):
    scale = D_HD ** -0.5
    s = jnp.einsum('bqhpd,bkhpd->bhpqk', q, k) * scale
    p = jax.nn.softmax(s, axis=-1)
    wts = p[:, :, 0] - lam * p[:, :, 1]
    return jnp.einsum('bhqk,bkhe->bqhe', wts, v)


def _centred_dwconv(x, w):
    pad = (w.shape[0] - 1) // 2
    return lax.conv_general_dilated(x, w[:, None, :], window_strides=(1,), padding=[(pad, pad)],
                                    dimension_numbers=('NWC', 'WIO', 'NWC'),
                                    feature_group_count=x.shape[-1])


def _mixers_jax(proj, gab, lw, li, ctx, with_gdn=True):
    B, L, _ = proj.shape
    latent = ctx is not None
    r_q = proj[..., C_RQ:C_RK].reshape(B, L, R_HEADS, R_DK)
    r_k = proj[..., C_RK:C_RV].reshape(B, L, R_HEADS, R_DK)
    r_v = proj[..., C_RV:C_RG].reshape(B, L, R_HEADS, R_DV)
    r_g = proj[..., C_RG:C_DQ]
    if latent:
        r_q, r_k = _axial_rope(r_q), _axial_rope(r_k)
        r0_f, r0_b = ctx['ret'][:, 0], ctx['ret'][:, 1]
    else:
        r0_f = r0_b = jnp.zeros((B, R_HEADS, R_DK, R_DV), F32)
    log_gamma = jax.nn.log_sigmoid(lw['ret_decay'])
    or_f, sr_f = _retention_scan(r_q, r_k, r_v, log_gamma[0], r0_f)
    or_b, sr_b = _retention_scan(_flip(r_q), _flip(r_k), _flip(r_v), log_gamma[1], r0_b)
    o_ret = _rmsnorm(or_f + _flip(or_b), lw['ret_norm']).reshape(B, L, BRANCH_W) * jax.nn.silu(r_g)

    d_q = proj[..., C_DQ:C_DK].reshape(B, L, D_HEADS, 2, D_HD)
    d_k = proj[..., C_DK:C_DV].reshape(B, L, D_HEADS, 2, D_HD)
    d_v = proj[..., C_DV:C_GQKV].reshape(B, L, D_HEADS, 2 * D_HD)
    if latent:
        q_att = _axial_rope(d_q)
        k_all = jnp.concatenate([_axial_rope(d_k), ctx['k']], axis=1)
        v_all = jnp.concatenate([d_v, ctx['v']], axis=1)
    else:
        q_att, k_all, v_all = d_q, d_k, d_v
    lam_init = 0.8 - 0.6 * math.exp(-0.3 * li)
    lp = lw['diff_lambda']
    lam = jnp.exp(jnp.sum(lp[0] * lp[1])) - jnp.exp(jnp.sum(lp[2] * lp[3])) + lam_init
    o_diff = _diff_attention(q_att, k_all, v_all, lam)
    o_diff = (_rmsnorm(o_diff, lw['diff_norm']) * (1.0 - lam_init)).reshape(B, L, BRANCH_W)
    if not with_gdn:
        return o_ret, o_diff, None, (d_k, d_v, jnp.stack([sr_f, sr_b], axis=1), None)

    g_qkv = jax.nn.silu(_centred_dwconv(proj[..., C_GQKV:C_GZ], lw['gdn_conv']))
    g_q, g_k, g_v = jnp.split(g_qkv, [G_HEADS * G_DK, 2 * G_HEADS * G_DK], axis=-1)
    g_q = _l2norm(g_q.reshape(B, L, G_HEADS, G_DK))
    g_k = _l2norm(g_k.reshape(B, L, G_HEADS, G_DK))
    g_v = g_v.reshape(B, L, G_HEADS, G_DV)
    g_z = proj[..., C_GZ:C_MG]
    g_a = gab[..., :16].reshape(B, L, 2, G_HEADS)
    g_b = gab[..., 16:32].reshape(B, L, 2, G_HEADS)
    g_log = -jnp.exp(lw['gdn_A_log']) * jax.nn.softplus(g_a + lw['gdn_dt_bias'])
    g_beta = jax.nn.sigmoid(g_b)
    if latent:
        g0_f, g0_b = ctx['gdn'][:, 0], ctx['gdn'][:, 1]
    else:
        g0_f = g0_b = jnp.zeros((B, G_HEADS, G_DK, G_DV), F32)
    og_f, sg_f = _gdn_scan(g_q, g_k, g_v, g_log[:, :, 0], g_beta[:, :, 0], g0_f)
    og_b, sg_b = _gdn_scan(_flip(g_q), _flip(g_k), _flip(g_v), _flip(g_log[:, :, 1]), _flip(g_beta[:, :, 1]), g0_b)
    o_gdn = _rmsnorm(og_f + _flip(og_b), lw['gdn_norm']) * jax.nn.silu(g_z.reshape(B, L, G_HEADS, G_DV))
    o_gdn = o_gdn.reshape(B, L, BRANCH_W)
    state = (d_k, d_v, jnp.stack([sr_f, sr_b], axis=1), jnp.stack([sg_f, sg_b], axis=1))
    return o_ret, o_diff, o_gdn, state


def _peer_jax(h, w_q, sub_keys, exp_u, exp_v):
    hb = h.reshape(N_TOK // TOK_BLOCK, TOK_BLOCK, D_MODEL)

    def block(hx):
        q = jnp.dot(hx, w_q, preferred_element_type=F32).reshape(TOK_BLOCK, PEER_HEADS, 2, PEER_DQ // 2)
        s = jnp.einsum('thpd,pnd->thpn', q, sub_keys)
        s_top, i_top = lax.top_k(s, PEER_TOPK)
        cand_s = (s_top[:, :, 0, :, None] + s_top[:, :, 1, None, :]).reshape(TOK_BLOCK, PEER_HEADS, -1)
        cand_i = (i_top[:, :, 0, :, None] * N_KEYS + i_top[:, :, 1, None, :]).reshape(TOK_BLOCK, PEER_HEADS, -1)
        best_s, pos = lax.top_k(cand_s, PEER_TOPK)
        idx = jnp.take_along_axis(cand_i, pos, axis=-1)
        gate = jax.nn.softmax(best_s, axis=-1)
        hf = hx.astype(F32)
        act = jax.nn.gelu(jnp.einsum('thkd,td->thk', exp_u[idx], hf), approximate=False)
        return jnp.einsum('thk,thkd->td', gate * act, exp_v[idx])

    return lax.map(block, hb).reshape(N_TOK, D_MODEL)


def _per_block(rows):
    idx = np.concatenate([np.zeros(N_CTX // MOD_BLK, np.int32)] +
                         [np.full(DEC_SEQ // MOD_BLK, 1 + b, np.int32) for b in range(DEC_BATCH)])
    return rows[idx][:, None, :]


def kernel(x_prompt, x_sample, cache_diff_k, cache_diff_v, state_ret, state_gdn, c, c_ctx,
           norm1_w, norm2_w, w_mod, b_mod, w_in, ret_decay, ret_norm, diff_lambda, diff_norm,
           gdn_conv, gdn_A_log, gdn_dt_bias, gdn_norm, w_branch, w_out,
           peer_wq, peer_keys, peer_u, peer_v, norm_f_w):
    x = jnp.concatenate([x_prompt.reshape(N_CTX, D_MODEL), x_sample.reshape(N_LAT, D_MODEL)], axis=0)
    cond = jnp.concatenate([c_ctx[None, :], c, jnp.zeros((8 - N_COND, D_MODEL), F32)], axis=0)
    mod_all = _modulation(cond, w_mod, b_mod)

    w_main = jnp.concatenate([w_in[:, :, :AB_OFF], w_in[:, :, AB_OFF + 32:]], axis=-1).astype(BF16)
    w_ab = jnp.pad(w_in[:, :, AB_OFF:AB_OFF + 32], ((0, 0), (0, 0), (0, 96))).astype(BF16)
    wb_bf = w_branch.astype(BF16)
    wo_bf = w_out.astype(BF16)
    wq_bf = peer_wq.astype(BF16)

    ks, vs, srs, sgs = [], [], [], []
    for li in range(DEPTH):
        lw = dict(ret_decay=ret_decay[li], ret_norm=ret_norm[li], diff_lambda=diff_lambda[li],
                  diff_norm=diff_norm[li], gdn_conv=gdn_conv[li], gdn_A_log=gdn_A_log[li],
                  gdn_dt_bias=gdn_dt_bias[li], gdn_norm=gdn_norm[li])
        mod = mod_all[li, :N_COND]
        sh1, sc1, g1, sh2, sc2, g2 = [_per_block(m) for m in jnp.split(mod, 6, axis=-1)]
        h = _normmod(x, norm1_w[li], sc1, sh1, BF16)
        proj = _matmul(h, w_main[li], 512, 1024)
        gab = _matmul(h, w_ab[li], 512, 128)

        oa, sr_l = _retention(proj, ret_decay[li], ret_norm[li], SEQ, BATCH, 0)
        za, _ = _retention(proj, ret_decay[li], ret_norm[li], DEC_SEQ, DEC_BATCH, N_CTX, s0=state_ret[:, li])
        ob = _diff_attn(proj, diff_lambda[li], diff_norm[li], li, SEQ, BATCH, 0)
        zb = _diff_attn(proj, diff_lambda[li], diff_norm[li], li, DEC_SEQ, DEC_BATCH, N_CTX, cache_diff_k, cache_diff_v)
        k_l = proj[:N_CTX, C_DK:C_DV].reshape(BATCH, SEQ, D_HEADS, 2, D_HD)
        v_l = proj[:N_CTX, C_DV:C_GQKV].reshape(BATCH, SEQ, D_HEADS, 2 * D_HD)

        gcol = gab[:, :32].reshape(N_TOK, 2, 2, G_HEADS).transpose(1, 3, 2, 0).reshape(32, N_TOK, 1)
        alog = gdn_A_log[li].T.reshape(2 * G_HEADS, 1, 1)
        dtb = gdn_dt_bias[li].T.reshape(2 * G_HEADS, 1, 1)
        qkv_c = _gdn_prep(proj, gdn_conv[li], SEQ, BATCH, 0)
        oc, sg_l = _gdn(qkv_c, proj, gcol, alog, dtb, gdn_norm[li], None, SEQ, BATCH, 0)
        qkv_z = _gdn_prep(proj, gdn_conv[li], DEC_SEQ, DEC_BATCH, N_CTX)
        zc, _ = _gdn(qkv_z, proj, gcol, alog, dtb, gdn_norm[li], state_gdn[:, li], DEC_SEQ, DEC_BATCH, N_CTX)
        ks.append(k_l); vs.append(v_l); srs.append(sr_l); sgs.append(sg_l)
        cat = lambda a, b: jnp.concatenate([a, b], axis=0)
        o_ret, o_diff, o_gdn = cat(oa, za), cat(ob, zb), cat(oc, zc)

        x, h2 = _merge(o_ret, o_diff, o_gdn, proj, wb_bf[li], wo_bf[li], x, g1, norm2_w[li], sc2, sh2)
        idx, gate = _peer_route(h2, wq_bf[li], peer_keys[li])
        x = _peer_experts(h2, idx, gate, peer_u[li].astype(BF16), peer_v[li].astype(BF16), x, g2)

    zeros_blk = jnp.zeros((N_MOD_BLK, 1, D_MODEL), F32)
    y = _normmod(x, norm_f_w, zeros_blk, zeros_blk, F32)
    y_prompt = y[:N_CTX].reshape(BATCH, SEQ, D_MODEL)
    y_sample = y[N_CTX:].reshape(DEC_BATCH, DEC_SEQ, D_MODEL)
    return (y_prompt, y_sample, jnp.stack(ks, axis=1), jnp.stack(vs, axis=1),
            jnp.stack(srs, axis=1), jnp.stack(sgs, axis=1))
```

```python
import functools
import math

import numpy as np
import jax
import jax.numpy as jnp
from jax import lax
from jax.experimental import pallas as pl
from jax.experimental.pallas import tpu as pltpu

F32 = jnp.float32
BF16 = jnp.bfloat16

D_MODEL = 1024
BATCH = 16
SEQ = 256
DEPTH = 2
DEC_BATCH = 2
DEC_SEQ = 1024
PAST_LEN = 256
GRID_W = 64
ROPE_BASE = 10000.0
EPS = 1e-6
CHUNK = 64
R_HEADS, R_DK, R_DV = 4, 128, 256
D_HEADS, D_HD = 8, 64
G_HEADS, G_DK, G_DV = 8, 128, 128
CONV_K = 7
G_QKV = G_HEADS * (2 * G_DK + G_DV)
N_BRANCH = 3
BRANCH_W = 1024
PEER_HEADS = 8
PEER_DQ = 256
N_KEYS = 128
PEER_TOPK = 16
TOK_BLOCK = 128

N_CTX = BATCH * SEQ
N_LAT = DEC_BATCH * DEC_SEQ
N_TOK = N_CTX + N_LAT
N_COND = 1 + DEC_BATCH
MOD_BLK = 256
N_MOD_BLK = N_TOK // MOD_BLK

C_RQ, C_RK, C_RV, C_RG = 0, 512, 1024, 2048
C_DQ, C_DK, C_DV = 3072, 4096, 5120
C_GQKV, C_GZ, C_MG = 6144, 9216, 10240
N_MAIN = 13312
AB_OFF = 10240

VMEM_LIMIT = 56 * 1024 * 1024


def _cparams(sem):
    return pltpu.CompilerParams(dimension_semantics=sem, vmem_limit_bytes=VMEM_LIMIT)


def _mod_kernel(c_ref, w_ref, b_ref, o_ref):
    c = c_ref[...]
    a = c * jax.nn.sigmoid(c)
    o_ref[0] = jnp.dot(a, w_ref[0], preferred_element_type=F32,
                       precision=lax.Precision.HIGHEST) + b_ref[0]


def _modulation(cond_pad, w_mod, b_mod):
    tn = 1536
    return pl.pallas_call(
        _mod_kernel,
        out_shape=jax.ShapeDtypeStruct((DEPTH, 8, 6 * D_MODEL), F32),
        grid=(DEPTH, 6 * D_MODEL // tn),
        in_specs=[pl.BlockSpec((8, D_MODEL), lambda l, j: (0, 0)),
                  pl.BlockSpec((1, D_MODEL, tn), lambda l, j: (l, 0, j)),
                  pl.BlockSpec((1, 1, tn), lambda l, j: (l, 0, j))],
        out_specs=pl.BlockSpec((1, 8, tn), lambda l, j: (l, 0, j)),
        compiler_params=_cparams(("arbitrary", "arbitrary")),
        name="modulation",
    )(cond_pad, w_mod, b_mod.reshape(DEPTH, 1, 6 * D_MODEL))


def _normmod_kernel(x_ref, w_ref, sc_ref, sh_ref, o_ref):
    x = x_ref[...]
    y = x * lax.rsqrt(jnp.mean(x * x, axis=-1, keepdims=True) + EPS)
    y = y * w_ref[...]
    o_ref[...] = (y * (1.0 + sc_ref[0]) + sh_ref[0]).astype(o_ref.dtype)


def _normmod(x, w, sc_blk, sh_blk, out_dtype):
    tm = MOD_BLK
    return pl.pallas_call(
        _normmod_kernel,
        out_shape=jax.ShapeDtypeStruct((N_TOK, D_MODEL), out_dtype),
        grid=(N_TOK // tm,),
        in_specs=[pl.BlockSpec((tm, D_MODEL), lambda i: (i, 0)),
                  pl.BlockSpec((1, D_MODEL), lambda i: (0, 0)),
                  pl.BlockSpec((1, 1, D_MODEL), lambda i: (i, 0, 0)),
                  pl.BlockSpec((1, 1, D_MODEL), lambda i: (i, 0, 0))],
        out_specs=pl.BlockSpec((tm, D_MODEL), lambda i: (i, 0)),
        compiler_params=_cparams(("arbitrary",)),
        name="normmod",
    )(x, w.reshape(1, D_MODEL), sc_blk, sh_blk)


def _matmul_kernel(a_ref, b_ref, o_ref):
    o_ref[...] = jnp.dot(a_ref[...], b_ref[...], preferred_element_type=F32)


def _matmul(a, b, tm, tn):
    m, k = a.shape
    n = b.shape[1]
    return pl.pallas_call(
        _matmul_kernel,
        out_shape=jax.ShapeDtypeStruct((m, n), F32),
        grid=(n // tn, m // tm),
        in_specs=[pl.BlockSpec((tm, k), lambda j, i: (i, 0)),
                  pl.BlockSpec((k, tn), lambda j, i: (0, j))],
        out_specs=pl.BlockSpec((tm, tn), lambda j, i: (i, j)),
        compiler_params=_cparams(("arbitrary", "arbitrary")),
        name="matmul",
    )(a, b)


def _merge_kernel(oa_ref, ob_ref, oc_ref, mga_ref, mgb_ref, mgc_ref, wb_ref, wo_ref, x_ref, g1_ref,
                  n2_ref, sc_ref, sh_ref, xo_ref, h_ref):
    mix = None
    for n, (o_ref, mg_ref) in enumerate(((oa_ref, mga_ref), (ob_ref, mgb_ref), (oc_ref, mgc_ref))):
        merged = jnp.dot(o_ref[...], wb_ref[n], preferred_element_type=F32)
        gate = jax.nn.sigmoid(mg_ref[...])
        mix = gate * merged if mix is None else mix + gate * merged
    m = jnp.dot(mix.astype(BF16), wo_ref[...], preferred_element_type=F32)
    x = x_ref[...] + g1_ref[0] * m
    xo_ref[...] = x
    y = x * lax.rsqrt(jnp.mean(x * x, axis=-1, keepdims=True) + EPS)
    y = y * n2_ref[...]
    h_ref[...] = (y * (1.0 + sc_ref[0]) + sh_ref[0]).astype(h_ref.dtype)


def _merge(o_ret, o_diff, o_gdn, proj, wb, wo, x, g1_blk, n2w, sc2_blk, sh2_blk):
    tm = MOD_BLK
    tok = lambda i: (i, 0)
    blk = lambda i: (i, 0, 0)
    return pl.pallas_call(
        _merge_kernel,
        out_shape=(jax.ShapeDtypeStruct((N_TOK, D_MODEL), F32),
                   jax.ShapeDtypeStruct((N_TOK, D_MODEL), BF16)),
        grid=(N_TOK // tm,),
        in_specs=[pl.BlockSpec((tm, BRANCH_W), tok),
                  pl.BlockSpec((tm, BRANCH_W), tok),
                  pl.BlockSpec((tm, BRANCH_W), tok),
                  pl.BlockSpec((tm, D_MODEL), lambda i: (i, C_MG // D_MODEL)),
                  pl.BlockSpec((tm, D_MODEL), lambda i: (i, C_MG // D_MODEL + 1)),
                  pl.BlockSpec((tm, D_MODEL), lambda i: (i, C_MG // D_MODEL + 2)),
                  pl.BlockSpec((N_BRANCH, BRANCH_W, D_MODEL), lambda i: (0, 0, 0)),
                  pl.BlockSpec((D_MODEL, D_MODEL), lambda i: (0, 0)),
                  pl.BlockSpec((tm, D_MODEL), tok),
                  pl.BlockSpec((1, 1, D_MODEL), blk),
                  pl.BlockSpec((1, D_MODEL), lambda i: (0, 0)),
                  pl.BlockSpec((1, 1, D_MODEL), blk),
                  pl.BlockSpec((1, 1, D_MODEL), blk)],
        out_specs=(pl.BlockSpec((tm, D_MODEL), tok), pl.BlockSpec((tm, D_MODEL), tok)),
        compiler_params=_cparams(("arbitrary",)),
        name="merge",
    )(o_ret, o_diff, o_gdn, proj, proj, proj, wb, wo, x, g1_blk, n2w.reshape(1, D_MODEL), sc2_blk, sh2_blk)


ROUTE_T = 128
ROUTE_HEADS = 2
N_PAIR = PEER_HEADS * PEER_TOPK
N_CAND = PEER_TOPK + (PEER_TOPK // 2) * (PEER_TOPK // 2 - 1) + PEER_TOPK // 2
NEG_INF = float("-inf")


def _split_bf16(x):
    hi = x.astype(BF16)
    return hi, (x - hi.astype(F32)).astype(BF16)


def _topk_rows(scores, rows, n_rows, payloads=None):
    scores = list(scores)
    vals = [[] for _ in scores]
    outs = [[] for _ in scores]
    for _ in range(PEER_TOPK):
        for n, s in enumerate(scores):
            m = jnp.max(s, axis=0, keepdims=True)
            pos = jnp.min(jnp.where(s == m, rows, n_rows), axis=0, keepdims=True)
            sel = rows == pos
            vals[n].append(m)
            if payloads is None:
                outs[n].append(pos)
            else:
                outs[n].append(jnp.sum(jnp.where(sel, payloads[n], 0), axis=0, keepdims=True))
            scores[n] = jnp.where(sel, NEG_INF, s)
    return [(jnp.concatenate(v, axis=0), jnp.concatenate(o, axis=0)) for v, o in zip(vals, outs)]


def _route_kernel(h_ref, wq_ref, khi_ref, klo_ref, idx_ref, gate_ref, q_scr, idx_scr, gate_scr):
    t = h_ref.shape[0]
    q = jnp.dot(h_ref[...], wq_ref[...], preferred_element_type=F32)
    for hp in range(2 * PEER_HEADS):
        q_scr[hp] = q[:, hp * 128:(hp + 1) * 128]
    rows_k = lax.broadcasted_iota(jnp.int32, (N_KEYS, t), 0)
    rows_c = lax.broadcasted_iota(jnp.int32, (N_CAND, t), 0)
    nt = (((1,), (1,)), ((), ()))

    half = PEER_TOPK // 2
    groups = [(slice(0, 1), slice(0, PEER_TOPK)), (slice(1, 2), slice(0, half))]
    groups += [(slice(k, k + 1), slice(0, half)) for k in range(2, half)]
    groups += [(slice(half, PEER_TOPK), slice(0, 1))]

    def heads(hh, carry):
        hs = [ROUTE_HEADS * hh + n for n in range(ROUTE_HEADS)]
        scores = []
        for h in hs:
            for p in range(2):
                qhi, qlo = _split_bf16(q_scr[2 * h + p])
                scores.append(lax.dot_general(khi_ref[p], qhi, nt, preferred_element_type=F32)
                              + lax.dot_general(khi_ref[p], qlo, nt, preferred_element_type=F32)
                              + lax.dot_general(klo_ref[p], qhi, nt, preferred_element_type=F32))
        tops = _topk_rows(scores, rows_k, N_KEYS)
        cands, cand_is = [], []
        for n in range(ROUTE_HEADS):
            (v1, i1), (v2, i2) = tops[2 * n], tops[2 * n + 1]
            cands.append(jnp.concatenate([v1[a] + v2[b] for a, b in groups], axis=0))
            cand_is.append(jnp.concatenate([i1[a] * N_KEYS + i2[b] for a, b in groups], axis=0))
        for h, (best, bidx) in zip(hs, _topk_rows(cands, rows_c, N_CAND, payloads=cand_is)):
            e = jnp.exp(best - best[0:1])
            gate_scr[h] = e / jnp.sum(e, axis=0, keepdims=True)
            idx_scr[h] = bidx
        return carry

    lax.fori_loop(0, PEER_HEADS // ROUTE_HEADS, heads, 0)
    idx_ref[...] = idx_scr[...].reshape(N_PAIR, t).T
    gate_ref[...] = gate_scr[...].reshape(N_PAIR, t).T


def _peer_route(h2, wq, keys):
    khi, klo = _split_bf16(keys)
    t = ROUTE_T
    n_tok = h2.shape[0]
    return pl.pallas_call(
        _route_kernel,
        out_shape=(jax.ShapeDtypeStruct((n_tok, N_PAIR), jnp.int32),
                   jax.ShapeDtypeStruct((n_tok, N_PAIR), F32)),
        grid=(n_tok // t,),
        in_specs=[pl.BlockSpec((t, D_MODEL), lambda i: (i, 0)),
                  pl.BlockSpec((D_MODEL, PEER_HEADS * PEER_DQ), lambda i: (0, 0)),
                  pl.BlockSpec((2, N_KEYS, PEER_DQ // 2), lambda i: (0, 0, 0)),
                  pl.BlockSpec((2, N_KEYS, PEER_DQ // 2), lambda i: (0, 0, 0))],
        out_specs=(pl.BlockSpec((t, N_PAIR), lambda i: (i, 0)),
                   pl.BlockSpec((t, N_PAIR), lambda i: (i, 0))),
        scratch_shapes=[pltpu.VMEM((2 * PEER_HEADS, t, PEER_DQ // 2), F32),
                        pltpu.VMEM((PEER_HEADS, PEER_TOPK, t), jnp.int32),
                        pltpu.VMEM((PEER_HEADS, PEER_TOPK, t), F32)],
        compiler_params=_cparams(("arbitrary",)),
        name="peer_route",
    )(h2, wq, khi, klo)


EXP_TB = 256
EXP_EB = 1024
EXP_SUB = 256
N_EXPERTS = N_KEYS * N_KEYS


def _expert_kernel(h_ref, idx_ref, gate_ref, u_ref, v_ref, x_ref, g2_ref, o_ref, g_scr, acc_ref):
    j = pl.program_id(1)
    tb = h_ref.shape[0]

    @pl.when(j == 0)
    def _():
        acc_ref[...] = jnp.zeros_like(acc_ref)
        sub = lax.broadcasted_iota(jnp.int32, (N_KEYS, N_PAIR), 0)

        def body(t, carry):
            e = idx_ref[pl.ds(t, 1), :]
            g = gate_ref[pl.ds(t, 1), :]
            xa = jnp.where(sub == (e >> 7), 1.0, 0.0).astype(BF16)
            yb = jnp.where(sub == (e & (N_KEYS - 1)), g, 0.0).astype(BF16)
            g_scr[pl.ds(pl.multiple_of(t * N_KEYS, N_KEYS), N_KEYS), :] = _dot_nt(xa, yb)
            return carry

        lax.fori_loop(0, tb, body, 0, unroll=16)

    h = h_ref[...]
    nsub = EXP_EB // EXP_SUB
    per = EXP_SUB // N_KEYS
    s = [_dot_nt(h, u_ref[c * EXP_SUB:(c + 1) * EXP_SUB, :]) for c in range(nsub)]
    acc = acc_ref[...]
    for c in range(nsub):
        gj = jnp.concatenate([g_scr[pl.ds(j * (nsub * per) + c * per + i, tb, stride=N_KEYS), :]
                              for i in range(per)], axis=1)
        w = 0.5 * s[c] * (1.0 + lax.erf(s[c] * (2.0 ** -0.5))) * gj
        acc = acc + jnp.dot(w.astype(BF16), v_ref[c * EXP_SUB:(c + 1) * EXP_SUB, :], preferred_element_type=F32)
    acc_ref[...] = acc

    @pl.when(j == pl.num_programs(1) - 1)
    def _():
        o_ref[...] = x_ref[...] + g2_ref[0] * acc_ref[...]


def _peer_experts(h2, idx, gate, u, v, x, g2_blk):
    tb, eb = EXP_TB, EXP_EB
    n_tok = h2.shape[0]
    return pl.pallas_call(
        _expert_kernel,
        out_shape=jax.ShapeDtypeStruct((n_tok, D_MODEL), F32),
        grid=(n_tok // tb, N_EXPERTS // eb),
        in_specs=[pl.BlockSpec((tb, D_MODEL), lambda i, j: (i, 0)),
                  pl.BlockSpec((tb, N_PAIR), lambda i, j: (i, 0)),
                  pl.BlockSpec((tb, N_PAIR), lambda i, j: (i, 0)),
                  pl.BlockSpec((eb, D_MODEL), lambda i, j: (j, 0)),
                  pl.BlockSpec((eb, D_MODEL), lambda i, j: (j, 0)),
                  pl.BlockSpec((tb, D_MODEL), lambda i, j: (i, 0)),
                  pl.BlockSpec((1, 1, D_MODEL), lambda i, j: (i * tb // MOD_BLK, 0, 0))],
        out_specs=pl.BlockSpec((tb, D_MODEL), lambda i, j: (i, 0)),
        scratch_shapes=[pltpu.VMEM((tb * N_KEYS, N_KEYS), F32),
                        pltpu.VMEM((tb, D_MODEL), F32)],
        compiler_params=_cparams(("arbitrary", "arbitrary")),
        name="peer_experts",
    )(h2, idx, gate, u, v, x, g2_blk)


CONV_ROWS = 256
CONV_HALO = 8


def _gdn_prep_kernel(x_ref, w_ref, o_ref):
    sec = pl.program_id(1)
    L = x_ref.shape[0]
    half = CONV_K // 2
    for r0 in range(0, L, CONV_ROWS):
        s0, s1 = max(r0 - CONV_HALO, 0), min(r0 + CONV_ROWS + CONV_HALO, L)
        n = s1 - s0
        t_idx = lax.broadcasted_iota(jnp.int32, (n, 128), 0) + s0
        for cb in range(G_HEADS):
            cols = slice(cb * 128, (cb + 1) * 128)
            x = x_ref[s0:s1, cols]
            w = w_ref[0, :, cols]
            acc = x * w[half:half + 1]
            for d in range(-half, half + 1):
                if d == 0:
                    continue
                xs = pltpu.roll(x, (-d) % n, axis=0)
                valid = (t_idx + d >= 0) if d < 0 else (t_idx + d < L)
                acc = acc + jnp.where(valid, xs, 0.0) * w[half + d:half + d + 1]
            y = acc[r0 - s0:r0 - s0 + CONV_ROWS]
            y = y * jax.nn.sigmoid(y)
            inv = lax.rsqrt(jnp.sum(y * y, axis=-1, keepdims=True) + EPS)
            o_ref[r0:r0 + CONV_ROWS, cols] = y * jnp.where(sec < 2, inv, 1.0)


def _gdn_prep(proj, conv_w, seq_len, n_seq, tok_off):
    blk0 = tok_off // seq_len
    sec0 = C_GQKV // 1024
    return pl.pallas_call(
        _gdn_prep_kernel,
        out_shape=jax.ShapeDtypeStruct((n_seq * seq_len, G_QKV), F32),
        grid=(n_seq, 3),
        in_specs=[pl.BlockSpec((seq_len, 1024), lambda s, c: (blk0 + s, sec0 + c)),
                  pl.BlockSpec((1, CONV_K, 1024), lambda s, c: (0, 0, c))],
        out_specs=pl.BlockSpec((seq_len, 1024), lambda s, c: (s, c)),
        compiler_params=_cparams(("arbitrary", "arbitrary")),
        name="gdn_prep",
    )(proj, conv_w.reshape(1, CONV_K, G_QKV))


GDN_HB = 2


def _dot(a, b):
    return jnp.dot(a, b, preferred_element_type=F32)


def _dot_nt(a, b):
    return lax.dot_general(a, b, (((1,), (1,)), ((), ())), preferred_element_type=F32)


def _dot_tn(a, b):
    return lax.dot_general(a, b, (((0,), (0,)), ((), ())), preferred_element_type=F32)


def _mask_dot3(m, b):
    m16 = m.astype(BF16)
    bh = b.astype(BF16)
    r1 = b - bh.astype(F32)
    bm = r1.astype(BF16)
    bl = (r1 - bm.astype(F32)).astype(BF16)
    return _dot(m16, bh) + _dot(m16, bm) + _dot(m16, bl)


def _gdn_kernel(*refs, seq_len, has_init, emit_state):
    it = iter(refs)
    q_ref, k_ref, v_ref, z_ref, gab_ref, alog_ref, dtb_ref, nw_ref = [next(it) for _ in range(8)]
    s0_ref = next(it) if has_init else None
    o_ref = next(it)
    so_ref = next(it) if emit_state else None
    u_s, w_s, qg_s, kd_s, qk_s, gl_s, o_s = [next(it) for _ in range(7)]
    C = CHUNK
    nc = seq_len // C
    ri = lax.broadcasted_iota(jnp.int32, (C, C), 0)
    ci = lax.broadcasted_iota(jnp.int32, (C, C), 1)
    eye = jnp.where(ri == ci, 1.0, 0.0)
    ones = jnp.ones((C, C), F32)
    scale = G_DK ** -0.5
    incl = (ri >= ci, ri <= ci)
    strict = (ri > ci, ri < ci)
    levels = []
    for lv in range(6):
        b = 1 << lv
        same = (ri >> (lv + 1)) == (ci >> (lv + 1))
        r_hi, c_hi = (ri & b) != 0, (ci & b) != 0
        levels.append((same & r_hi & jnp.logical_not(c_hi), same & jnp.logical_not(r_hi) & c_hi))

    GROUP = 4
    HB = GDN_HB
    lanes = [slice(hh * 128, (hh + 1) * 128) for hh in range(HB)]
    tri = [jnp.where(m, 1.0, 0.0) for m in incl]

    def prep(grp, carry):
        chunks = [grp * GROUP + j for j in range(GROUP)]
        rows = [pl.ds(pl.multiple_of(c * C, C), C) for c in chunks]
        hj = [(hh, j) for hh in range(HB) for j in range(GROUP)]
        q = {(hh, j): q_ref[rows[j], lanes[hh]] * scale for hh, j in hj}
        k = {(hh, j): k_ref[rows[j], lanes[hh]] for hh, j in hj}
        v = {(hh, j): v_ref[rows[j], lanes[hh]] for hh, j in hj}
        k16 = {key: x.astype(BF16) for key, x in k.items()}
        kk = {key: _dot_nt(x, x) for key, x in k16.items()}
        qk = {key: _dot_nt(q[key].astype(BF16), k16[key]) for key in hj}
        probs = [(hh, j, d) for hh in range(HB) for j in range(GROUP) for d in range(2)]
        gab = [gab_ref[r, :] for r in rows]
        ga = lambda hh, j, d: gab[j][:, 2 * hh + d:2 * hh + d + 1]
        gb = lambda hh, j, d: gab[j][:, 2 * HB + 2 * hh + d:2 * HB + 2 * hh + d + 1]
        g = [-jnp.exp(alog_ref[2 * hh + d]) * jax.nn.softplus(ga(hh, j, d) + dtb_ref[2 * hh + d])
             for hh, j, d in probs]
        beta = [jax.nn.sigmoid(gb(hh, j, d)) for hh, j, d in probs]
        gcol = [_mask_dot3(tri[d], jnp.broadcast_to(g[p], (C, 128))) for p, (hh, j, d) in enumerate(probs)]
        grow = [_mask_dot3(ones, jnp.where(incl[1 - d], jnp.broadcast_to(g[p], (C, C)), 0.0))
                for p, (hh, j, d) in enumerate(probs)]
        dec = [jnp.exp(jnp.where(incl[d], gcol[p][:, :C] - grow[p], NEG_INF)) for p, (hh, j, d) in enumerate(probs)]
        a = [jnp.where(strict[d], kk[hh, j] * beta[p] * dec[p], 0.0) for p, (hh, j, d) in enumerate(probs)]
        t = [eye - jnp.where(levels[0][d], a[p], 0.0) for p, (hh, j, d) in enumerate(probs)]
        for lv in range(1, 6):
            t16 = [x.astype(BF16) for x in t]
            ct = [_dot(jnp.where(levels[lv][d], a[p], 0.0).astype(BF16), t16[p])
                  for p, (hh, j, d) in enumerate(probs)]
            t = [t[p] - _dot(t16[p], ct[p].astype(BF16)) for p in range(len(probs))]
        for p, (hh, j, d) in enumerate(probs):
            n = 2 * hh + d
            eg = jnp.exp(gcol[p])
            t16 = t[p].astype(BF16)
            glast = gcol[p][C - 1:C, :] if d == 0 else gcol[p][0:1, :]
            u_s[n, rows[j], :] = _dot(t16, (v[hh, j] * beta[p]).astype(BF16))
            w_s[n, rows[j], :] = _dot(t16, (k[hh, j] * beta[p] * eg).astype(BF16))
            qg_s[n, rows[j], :] = q[hh, j] * eg
            kd_s[n, rows[j], :] = k[hh, j] * jnp.exp(glast - gcol[p])
            qk_s[n, rows[j], :] = qk[hh, j] * dec[p]
            gl_s[n * nc + chunks[j]] = jnp.broadcast_to(jnp.exp(glast), (8, 128))
        return carry

    lax.fori_loop(0, nc // GROUP, prep, 0)
    o_s[...] = jnp.zeros_like(o_s)
    chains = [(hh, d) for hh in range(HB) for d in range(2)]

    def step(i, states):
        cs = (i, nc - 1 - i)
        rows = [pl.ds(pl.multiple_of(c * C, C), C) for c in cs]
        s16 = [s.astype(BF16) for s in states]
        ws = [_dot(w_s[n, rows[d], :].astype(BF16), s16[n]) for n, (hh, d) in enumerate(chains)]
        un16 = [(u_s[n, rows[d], :] - ws[n]).astype(BF16) for n, (hh, d) in enumerate(chains)]
        o = [_dot(qg_s[n, rows[d], :].astype(BF16), s16[n]) + _dot(qk_s[n, rows[d], :].astype(BF16), un16[n])
             for n, (hh, d) in enumerate(chains)]
        new = tuple(states[n] * gl_s[n * nc + cs[d]][0:1, :] + _dot_tn(kd_s[n, rows[d], :].astype(BF16), un16[n])
                    for n, (hh, d) in enumerate(chains))
        for n, (hh, d) in enumerate(chains):
            o_s[rows[d], lanes[hh]] += o[n]
        return new

    if has_init:
        init = tuple(s0_ref[0, d, hh] for hh, d in chains)
    else:
        init = (jnp.zeros((G_DK, G_DV), F32),) * len(chains)
    final = lax.fori_loop(0, nc, step, init)
    if emit_state:
        for n, (hh, d) in enumerate(chains):
            so_ref[0, d, hh] = final[n]
    for hh in range(HB):
        o = o_s[:, lanes[hh]]
        y = o * lax.rsqrt(jnp.mean(o * o, axis=-1, keepdims=True) + EPS) * nw_ref[...]
        z = z_ref[:, lanes[hh]]
        o_ref[:, lanes[hh]] = (y * (z * jax.nn.sigmoid(z))).astype(o_ref.dtype)


def _gdn_gate_cols():
    cols = []
    for blk in range(G_HEADS // GDN_HB):
        heads = range(blk * GDN_HB, (blk + 1) * GDN_HB)
        cols.append([kind * 2 * G_HEADS + d * G_HEADS + h for kind in range(2) for h in heads for d in range(2)])
    return np.asarray(cols, np.int32)


def _gdn(qkv, proj, gab, alog, dtb, nw, s0, seq_len, n_seq, tok_off):
    has_init = s0 is not None
    emit_state = not has_init
    blk0 = tok_off // seq_len
    hb = GDN_HB
    wl = hb * 128
    nblk = G_HEADS // hb
    tokcol = lambda off: pl.BlockSpec((seq_len, wl), lambda s, h: (s, off + h))
    in_specs = [tokcol(0), tokcol(nblk), tokcol(2 * nblk),
                pl.BlockSpec((seq_len, wl), lambda s, h: (blk0 + s, C_GZ // wl + h)),
                pl.BlockSpec((seq_len, 128), lambda s, h: (blk0 + s, h)),
                pl.BlockSpec((2 * hb, 1, 1), lambda s, h: (h, 0, 0)),
                pl.BlockSpec((2 * hb, 1, 1), lambda s, h: (h, 0, 0)),
                pl.BlockSpec((1, G_DV), lambda s, h: (0, 0))]
    args = [qkv, qkv, qkv, proj, gab, alog, dtb, nw.reshape(1, G_DV)]
    if has_init:
        in_specs.append(pl.BlockSpec((1, 2, hb, G_DK, G_DV), lambda s, h: (s, 0, h, 0, 0)))
        args.append(s0)
    out_shape = [jax.ShapeDtypeStruct((n_seq * seq_len, BRANCH_W), BF16)]
    out_specs = [pl.BlockSpec((seq_len, wl), lambda s, h: (s, h))]
    if emit_state:
        out_shape.append(jax.ShapeDtypeStruct((n_seq, 2, G_HEADS, G_DK, G_DV), F32))
        out_specs.append(pl.BlockSpec((1, 2, hb, G_DK, G_DV), lambda s, h: (s, 0, h, 0, 0)))
    nc = seq_len // CHUNK
    res = pl.pallas_call(
        functools.partial(_gdn_kernel, seq_len=seq_len, has_init=has_init, emit_state=emit_state),
        out_shape=tuple(out_shape),
        grid=(n_seq, nblk),
        in_specs=in_specs,
        out_specs=tuple(out_specs),
        scratch_shapes=[pltpu.VMEM((2 * hb, seq_len, 128), F32)] * 4
                       + [pltpu.VMEM((2 * hb, seq_len, CHUNK), F32), pltpu.VMEM((2 * hb * nc, 8, 128), F32),
                          pltpu.VMEM((seq_len, wl), F32)],
        compiler_params=_cparams(("arbitrary", "arbitrary")),
        name="gdn",
    )(*args)
    return res if emit_state else (res[0], None)


def _rope_tables(seq_len, dim):
    n_rows = seq_len // GRID_W
    row = np.repeat(np.arange(n_rows), GRID_W).astype(np.float32)
    col = np.tile(np.arange(GRID_W), n_rows).astype(np.float32)
    nf = dim // 4
    inv = np.power(np.float32(ROPE_BASE), -np.arange(nf, dtype=np.float32) / np.float32(nf)).astype(np.float32)
    ang_r = row[:, None] * inv
    ang_c = col[:, None] * inv
    ang = np.concatenate([ang_r, ang_r, ang_c, ang_c], axis=-1)
    ang = np.tile(ang, (1, 128 // dim))
    return jnp.asarray(np.cos(ang), F32), jnp.asarray(np.sin(ang), F32)


def _rope(x, cos, sin, dim):
    quarter = dim // 4
    lane = lax.broadcasted_iota(jnp.int32, x.shape, 1)
    first = (lane & (2 * quarter - 1)) < quarter
    rot = jnp.where(first, -pltpu.roll(x, 128 - quarter, axis=1), pltpu.roll(x, quarter, axis=1))
    return x * cos + rot * sin


Q_TILE = 256


def _retention_kernel(*refs, seq_len, latent):
    it = iter(refs)
    q_ref, k_ref, v_ref, rg_ref, lg_ref, nw_ref = [next(it) for _ in range(6)]
    if latent:
        cos_ref, sin_ref, s0_ref = next(it), next(it), next(it)
    o_ref = next(it)
    so_ref = None if latent else next(it)
    L = seq_len
    lg = lg_ref[0]
    lg = jnp.minimum(lg, 0.0) - jnp.log(1.0 + jnp.exp(-jnp.abs(lg)))
    lgf, lgb = lg[0:1, :], lg[1:2, :]
    k = k_ref[...]
    if latent:
        k = _rope(k, cos_ref[...], sin_ref[...], R_DK)
    k16 = k.astype(BF16)
    v16 = v_ref[...].astype(BF16)
    for qt in range(L // Q_TILE):
        rows = slice(qt * Q_TILE, (qt + 1) * Q_TILE)
        q = q_ref[rows, :]
        if latent:
            q = _rope(q, cos_ref[rows, :], sin_ref[rows, :], R_DK)
        q16 = (q * R_DK ** -0.5).astype(BF16)
        a = _dot_nt(q16, k16)
        i = lax.broadcasted_iota(jnp.int32, (Q_TILE, L), 0) + qt * Q_TILE
        j = lax.broadcasted_iota(jnp.int32, (Q_TILE, L), 1)
        dist = (i - j).astype(F32)
        dec = (jnp.exp(jnp.where(dist >= 0, lgf * dist, NEG_INF))
               + jnp.exp(jnp.where(dist <= 0, -lgb * dist, NEG_INF)))
        o = _dot((a * dec).astype(BF16), v16)
        if latent:
            pos = (lax.broadcasted_iota(jnp.int32, (Q_TILE, 1), 0) + qt * Q_TILE).astype(F32)
            o = o + _dot(q16, s0_ref[0, 0, 0].astype(BF16)) * jnp.exp(lgf * (pos + 1.0))
            o = o + _dot(q16, s0_ref[0, 1, 0].astype(BF16)) * jnp.exp(lgb * (L - pos))
        y = o * lax.rsqrt(jnp.mean(o * o, axis=-1, keepdims=True) + EPS) * nw_ref[...]
        g = rg_ref[rows, :]
        o_ref[rows, :] = (y * (g * jax.nn.sigmoid(g))).astype(o_ref.dtype)
    if not latent:
        pos = lax.broadcasted_iota(jnp.int32, (L, 1), 0).astype(F32)
        so_ref[0, 0, 0] = _dot_tn((k * jnp.exp(lgf * (L - 1.0 - pos))).astype(BF16), v16)
        so_ref[0, 1, 0] = _dot_tn((k * jnp.exp(lgb * pos)).astype(BF16), v16)


def _retention(proj, decay, nw, seq_len, n_seq, tok_off, s0=None):
    latent = s0 is not None
    blk0 = tok_off // seq_len
    in_specs = [pl.BlockSpec((seq_len, R_DK), lambda s, h: (blk0 + s, C_RQ // R_DK + h)),
                pl.BlockSpec((seq_len, R_DK), lambda s, h: (blk0 + s, C_RK // R_DK + h)),
                pl.BlockSpec((seq_len, R_DV), lambda s, h: (blk0 + s, C_RV // R_DV + h)),
                pl.BlockSpec((seq_len, R_DV), lambda s, h: (blk0 + s, C_RG // R_DV + h)),
                pl.BlockSpec((1, 2, 1), lambda s, h: (h, 0, 0)),
                pl.BlockSpec((1, R_DV), lambda s, h: (0, 0))]
    args = [proj, proj, proj, proj, decay.T.reshape(R_HEADS, 2, 1), nw.reshape(1, R_DV)]
    out_shape = [jax.ShapeDtypeStruct((n_seq * seq_len, BRANCH_W), BF16)]
    out_specs = [pl.BlockSpec((seq_len, R_DV), lambda s, h: (s, h))]
    if latent:
        cos, sin = _rope_tables(seq_len, R_DK)
        in_specs += [pl.BlockSpec((seq_len, 128), lambda s, h: (0, 0)),
                     pl.BlockSpec((seq_len, 128), lambda s, h: (0, 0)),
                     pl.BlockSpec((1, 2, 1, R_DK, R_DV), lambda s, h: (s, 0, h, 0, 0))]
        args += [cos, sin, s0]
    else:
        out_shape.append(jax.ShapeDtypeStruct((n_seq, 2, R_HEADS, R_DK, R_DV), F32))
        out_specs.append(pl.BlockSpec((1, 2, 1, R_DK, R_DV), lambda s, h: (s, 0, h, 0, 0)))
    res = pl.pallas_call(
        functools.partial(_retention_kernel, seq_len=seq_len, latent=latent),
        out_shape=tuple(out_shape),
        grid=(n_seq, R_HEADS),
        in_specs=in_specs,
        out_specs=tuple(out_specs),
        compiler_params=_cparams(("arbitrary", "arbitrary")),
        name="retention",
    )(*args)
    return (res[0], None) if latent else res


DA_HB = 4


def _diff_attn_kernel(*refs, seq_len, latent, lam_init):
    it = iter(refs)
    q_ref, k_ref, v_ref, lp_ref, nw_ref = [next(it) for _ in range(5)]
    if latent:
        cos_ref, sin_ref, ck_ref, cv_ref = [next(it) for _ in range(4)]
    o_ref = next(it)
    L = seq_len
    lp = lp_ref[...]
    lam = (jnp.exp(jnp.sum(lp[0:1] * lp[1:2], axis=-1, keepdims=True))
           - jnp.exp(jnp.sum(lp[2:3] * lp[3:4], axis=-1, keepdims=True)) + lam_init)
    w = 2 * D_HD
    lanes = [slice(hh * w, (hh + 1) * w) for hh in range(DA_HB)]
    keys, vals = [], []
    for hh in range(DA_HB):
        k = k_ref[:, lanes[hh]]
        if latent:
            k = _rope(k, cos_ref[...], sin_ref[...], D_HD)
        keys.append([k.astype(BF16)] + ([ck_ref[0, 0, :, lanes[hh]].astype(BF16)] if latent else []))
        vals.append([v_ref[:, lanes[hh]].astype(BF16)] + ([cv_ref[0, 0, :, lanes[hh]].astype(BF16)] if latent else []))
    lane = lax.broadcasted_iota(jnp.int32, (Q_TILE, w), 1)
    for qt in range(L // Q_TILE):
        rows = slice(qt * Q_TILE, (qt + 1) * Q_TILE)
        hp = [(hh, p) for hh in range(DA_HB) for p in range(2)]
        qs = []
        for hh in range(DA_HB):
            q = q_ref[rows, lanes[hh]]
            if latent:
                q = _rope(q, cos_ref[rows, :], sin_ref[rows, :], D_HD)
            qs.append(q * D_HD ** -0.5)
        s = [[_dot_nt(jnp.where((lane >= D_HD) == (p == 1), qs[hh], 0.0).astype(BF16), kk) for kk in keys[hh]]
             for hh, p in hp]
        outs = []
        for n, (hh, p) in enumerate(hp):
            m = s[n][0].max(axis=-1, keepdims=True)
            for x in s[n][1:]:
                m = jnp.maximum(m, x.max(axis=-1, keepdims=True))
            e = [jnp.exp(x - m) for x in s[n]]
            z = sum(x.sum(axis=-1, keepdims=True) for x in e)
            pv = sum(_dot(x.astype(BF16), vv) for x, vv in zip(e, vals[hh]))
            outs.append(pv / z)
        for hh in range(DA_HB):
            o = outs[2 * hh] - lam * outs[2 * hh + 1]
            y = o * lax.rsqrt(jnp.mean(o * o, axis=-1, keepdims=True) + EPS) * nw_ref[...]
            o_ref[rows, lanes[hh]] = (y * (1.0 - lam_init)).astype(o_ref.dtype)


def _diff_attn(proj, lam_p, nw, li, seq_len, n_seq, tok_off, cache_k=None, cache_v=None):
    latent = cache_k is not None
    blk0 = tok_off // seq_len
    w = 2 * D_HD
    wl = DA_HB * w
    in_specs = [pl.BlockSpec((seq_len, wl), lambda s, h: (blk0 + s, C_DQ // wl + h)),
                pl.BlockSpec((seq_len, wl), lambda s, h: (blk0 + s, C_DK // wl + h)),
                pl.BlockSpec((seq_len, wl), lambda s, h: (blk0 + s, C_DV // wl + h)),
                pl.BlockSpec((4, D_HD), lambda s, h: (0, 0)),
                pl.BlockSpec((1, w), lambda s, h: (0, 0))]
    args = [proj, proj, proj, lam_p, nw.reshape(1, w)]
    if latent:
        cos, sin = _rope_tables(seq_len, D_HD)
        in_specs += [pl.BlockSpec((seq_len, 128), lambda s, h: (0, 0)),
                     pl.BlockSpec((seq_len, 128), lambda s, h: (0, 0)),
                     pl.BlockSpec((1, 1, PAST_LEN, wl), lambda s, h: (s, li, 0, h)),
                     pl.BlockSpec((1, 1, PAST_LEN, wl), lambda s, h: (s, li, 0, h))]
        args += [cos, sin, cache_k.reshape(DEC_BATCH, DEPTH, PAST_LEN, D_HEADS * w),
                 cache_v.reshape(DEC_BATCH, DEPTH, PAST_LEN, D_HEADS * w)]
    lam_init = 0.8 - 0.6 * math.exp(-0.3 * li)
    return pl.pallas_call(
        functools.partial(_diff_attn_kernel, seq_len=seq_len, latent=latent, lam_init=lam_init),
        out_shape=jax.ShapeDtypeStruct((n_seq * seq_len, BRANCH_W), BF16),
        grid=(n_seq, D_HEADS // DA_HB),
        in_specs=in_specs,
        out_specs=pl.BlockSpec((seq_len, wl), lambda s, h: (s, h)),
        compiler_params=_cparams(("arbitrary", "arbitrary")),
        name="diff_attn",
    )(*args)


def _rmsnorm(x, w):
    return x * lax.rsqrt(jnp.mean(x * x, axis=-1, keepdims=True) + EPS) * w


def _l2norm(x):
    return x * lax.rsqrt(jnp.sum(x * x, axis=-1, keepdims=True) + EPS)


def _flip(x):
    return jnp.flip(x, axis=1)


def _axial_rope(x):
    L, dim = x.shape[1], x.shape[-1]
    n_rows = L // GRID_W
    row = jnp.repeat(jnp.arange(n_rows), GRID_W).astype(F32)
    col = jnp.tile(jnp.arange(GRID_W), n_rows).astype(F32)
    nf = dim // 4
    inv = jnp.power(ROPE_BASE, -jnp.arange(nf, dtype=F32) / nf)
    ang_r = row[:, None] * inv
    ang_c = col[:, None] * inv
    ang = jnp.concatenate([ang_r, ang_r, ang_c, ang_c], axis=-1)
    shape = (1, L) + (1,) * (x.ndim - 3) + (dim,)
    cos = jnp.cos(ang).reshape(shape)
    sin = jnp.sin(ang).reshape(shape)
    x_r1, x_r2, x_c1, x_c2 = jnp.split(x, 4, axis=-1)
    rot = jnp.concatenate([-x_r2, x_r1, -x_c2, x_c1], axis=-1)
    return x * cos + rot * sin


def _to_chunks(x):
    B, L, H = x.shape[:3]
    x = x.reshape((B, L // CHUNK, CHUNK, H) + x.shape[3:])
    return jnp.moveaxis(x, 3, 1)


def _from_chunks(x):
    x = jnp.moveaxis(x, 1, 3)
    B, N, C, H = x.shape[:4]
    return x.reshape((B, N * C, H) + x.shape[4:])


def _retention_scan(q, k, v, log_gamma, s0):
    q = _to_chunks(q * R_DK ** -0.5)
    k = _to_chunks(k)
    v = _to_chunks(v)
    pos = jnp.arange(CHUNK, dtype=F32)
    lg = log_gamma[:, None]
    dist = pos[:, None] - pos[None, :]
    intra = jnp.exp(jnp.where(dist >= 0, lg[:, :, None] * dist, -jnp.inf))
    q_dec = jnp.exp(lg * (pos + 1.0))[:, :, None]
    k_dec = jnp.exp(lg * (CHUNK - 1.0 - pos))[:, :, None]
    c_dec = jnp.exp(lg * CHUNK)[:, :, None]

    def step(s, xs):
        qc, kc, vc = xs
        a = jnp.einsum('bhqd,bhkd->bhqk', qc, kc) * intra
        o = jnp.einsum('bhqk,bhkv->bhqv', a, vc) + jnp.einsum('bhqd,bhdv->bhqv', qc, s) * q_dec
        s = s * c_dec + jnp.einsum('bhkd,bhkv->bhdv', kc * k_dec, vc)
        return s, o

    xs = (jnp.moveaxis(q, 2, 0), jnp.moveaxis(k, 2, 0), jnp.moveaxis(v, 2, 0))
    s, o = lax.scan(step, s0, xs)
    return _from_chunks(jnp.moveaxis(o, 0, 2)), s


def _gdn_scan(q, k, v, g, beta, s0):
    q = _to_chunks(q * G_DK ** -0.5)
    k = _to_chunks(k)
    v = _to_chunks(v)
    g = jnp.cumsum(_to_chunks(g), axis=-1)
    beta = _to_chunks(beta)
    incl = jnp.tril(jnp.ones((CHUNK, CHUNK), dtype=bool))
    strict = jnp.tril(jnp.ones((CHUNK, CHUNK), dtype=bool), -1)
    decay = jnp.exp(jnp.where(incl, g[..., :, None] - g[..., None, :], -jnp.inf))
    kb = k * beta[..., None]
    a = jnp.where(strict, jnp.einsum('bhnid,bhnjd->bhnij', kb, k) * decay, 0.0)
    eye = jnp.eye(CHUNK, dtype=F32)
    t = lax.linalg.triangular_solve(a + eye, jnp.broadcast_to(eye, a.shape), left_side=True, lower=True)
    u = t @ (v * beta[..., None])
    w = t @ (kb * jnp.exp(g)[..., None])
    qk = jnp.einsum('bhnid,bhnjd->bhnij', q, k) * decay
    qg = q * jnp.exp(g)[..., None]
    kd = k * jnp.exp(g[..., -1:] - g)[..., None]
    gl = jnp.exp(g[..., -1])[..., None, None]

    def step(s, xs):
        qg_c, kd_c, u_c, w_c, qk_c, gl_c = xs
        u_new = u_c - w_c @ s
        o = qg_c @ s + qk_c @ u_new
        s = s * gl_c + jnp.swapaxes(kd_c, -1, -2) @ u_new
        return s, o

    xs = tuple(jnp.moveaxis(arr, 2, 0) for arr in (qg, kd, u, w, qk, gl))
    s, o = lax.scan(step, s0, xs)
    return _from_chunks(jnp.moveaxis(o, 0, 2)), s


def _diff_attention(q, k, v, lam):
    scale = D_HD ** -0.5
    s = jnp.einsum('bqhpd,bkhpd->bhpqk', q, k) * scale
    p = jax.nn.softmax(s, axis=-1)
    wts = p[:, :, 0] - lam * p[:, :, 1]
    return jnp.einsum('bhqk,bkhe->bqhe', wts, v)


def _centred_dwconv(x, w):
    pad = (w.shape[0] - 1) // 2
    return lax.conv_general_dilated(x, w[:, None, :], window_strides=(1,), padding=[(pad, pad)],
                                    dimension_numbers=('NWC', 'WIO', 'NWC'),
                                    feature_group_count=x.shape[-1])


def _mixers_jax(proj, gab, lw, li, ctx, with_gdn=True):
    B, L, _ = proj.shape
    latent = ctx is not None
    r_q = proj[..., C_RQ:C_RK].reshape(B, L, R_HEADS, R_DK)
    r_k = proj[..., C_RK:C_RV].reshape(B, L, R_HEADS, R_DK)
    r_v = proj[..., C_RV:C_RG].reshape(B, L, R_HEADS, R_DV)
    r_g = proj[..., C_RG:C_DQ]
    if latent:
        r_q, r_k = _axial_rope(r_q), _axial_rope(r_k)
        r0_f, r0_b = ctx['ret'][:, 0], ctx['ret'][:, 1]
    else:
        r0_f = r0_b = jnp.zeros((B, R_HEADS, R_DK, R_DV), F32)
    log_gamma = jax.nn.log_sigmoid(lw['ret_decay'])
    or_f, sr_f = _retention_scan(r_q, r_k, r_v, log_gamma[0], r0_f)
    or_b, sr_b = _retention_scan(_flip(r_q), _flip(r_k), _flip(r_v), log_gamma[1], r0_b)
    o_ret = _rmsnorm(or_f + _flip(or_b), lw['ret_norm']).reshape(B, L, BRANCH_W) * jax.nn.silu(r_g)

    d_q = proj[..., C_DQ:C_DK].reshape(B, L, D_HEADS, 2, D_HD)
    d_k = proj[..., C_DK:C_DV].reshape(B, L, D_HEADS, 2, D_HD)
    d_v = proj[..., C_DV:C_GQKV].reshape(B, L, D_HEADS, 2 * D_HD)
    if latent:
        q_att = _axial_rope(d_q)
        k_all = jnp.concatenate([_axial_rope(d_k), ctx['k']], axis=1)
        v_all = jnp.concatenate([d_v, ctx['v']], axis=1)
    else:
        q_att, k_all, v_all = d_q, d_k, d_v
    lam_init = 0.8 - 0.6 * math.exp(-0.3 * li)
    lp = lw['diff_lambda']
    lam = jnp.exp(jnp.sum(lp[0] * lp[1])) - jnp.exp(jnp.sum(lp[2] * lp[3])) + lam_init
    o_diff = _diff_attention(q_att, k_all, v_all, lam)
    o_diff = (_rmsnorm(o_diff, lw['diff_norm']) * (1.0 - lam_init)).reshape(B, L, BRANCH_W)
    if not with_gdn:
        return o_ret, o_diff, None, (d_k, d_v, jnp.stack([sr_f, sr_b], axis=1), None)

    g_qkv = jax.nn.silu(_centred_dwconv(proj[..., C_GQKV:C_GZ], lw['gdn_conv']))
    g_q, g_k, g_v = jnp.split(g_qkv, [G_HEADS * G_DK, 2 * G_HEADS * G_DK], axis=-1)
    g_q = _l2norm(g_q.reshape(B, L, G_HEADS, G_DK))
    g_k = _l2norm(g_k.reshape(B, L, G_HEADS, G_DK))
    g_v = g_v.reshape(B, L, G_HEADS, G_DV)
    g_z = proj[..., C_GZ:C_MG]
    g_a = gab[..., :16].reshape(B, L, 2, G_HEADS)
    g_b = gab[..., 16:32].reshape(B, L, 2, G_HEADS)
    g_log = -jnp.exp(lw['gdn_A_log']) * jax.nn.softplus(g_a + lw['gdn_dt_bias'])
    g_beta = jax.nn.sigmoid(g_b)
    if latent:
        g0_f, g0_b = ctx['gdn'][:, 0], ctx['gdn'][:, 1]
    else:
        g0_f = g0_b = jnp.zeros((B, G_HEADS, G_DK, G_DV), F32)
    og_f, sg_f = _gdn_scan(g_q, g_k, g_v, g_log[:, :, 0], g_beta[:, :, 0], g0_f)
    og_b, sg_b = _gdn_scan(_flip(g_q), _flip(g_k), _flip(g_v), _flip(g_log[:, :, 1]), _flip(g_beta[:, :, 1]), g0_b)
    o_gdn = _rmsnorm(og_f + _flip(og_b), lw['gdn_norm']) * jax.nn.silu(g_z.reshape(B, L, G_HEADS, G_DV))
    o_gdn = o_gdn.reshape(B, L, BRANCH_W)
    state = (d_k, d_v, jnp.stack([sr_f, sr_b], axis=1), jnp.stack([sg_f, sg_b], axis=1))
    return o_ret, o_diff, o_gdn, state


def _peer_jax(h, w_q, sub_keys, exp_u, exp_v):
    hb = h.reshape(N_TOK // TOK_BLOCK, TOK_BLOCK, D_MODEL)

    def block(hx):
        q = jnp.dot(hx, w_q, preferred_element_type=F32).reshape(TOK_BLOCK, PEER_HEADS, 2, PEER_DQ // 2)
        s = jnp.einsum('thpd,pnd->thpn', q, sub_keys)
        s_top, i_top = lax.top_k(s, PEER_TOPK)
        cand_s = (s_top[:, :, 0, :, None] + s_top[:, :, 1, None, :]).reshape(TOK_BLOCK, PEER_HEADS, -1)
        cand_i = (i_top[:, :, 0, :, None] * N_KEYS + i_top[:, :, 1, None, :]).reshape(TOK_BLOCK, PEER_HEADS, -1)
        best_s, pos = lax.top_k(cand_s, PEER_TOPK)
        idx = jnp.take_along_axis(cand_i, pos, axis=-1)
        gate = jax.nn.softmax(best_s, axis=-1)
        hf = hx.astype(F32)
        act = jax.nn.gelu(jnp.einsum('thkd,td->thk', exp_u[idx], hf), approximate=False)
        return jnp.einsum('thk,thkd->td', gate * act, exp_v[idx])

    return lax.map(block, hb).reshape(N_TOK, D_MODEL)


def _per_block(rows):
    idx = np.concatenate([np.zeros(N_CTX // MOD_BLK, np.int32)] +
                         [np.full(DEC_SEQ // MOD_BLK, 1 + b, np.int32) for b in range(DEC_BATCH)])
    return rows[idx][:, None, :]


def kernel(x_prompt, x_sample, cache_diff_k, cache_diff_v, state_ret, state_gdn, c, c_ctx,
           norm1_w, norm2_w, w_mod, b_mod, w_in, ret_decay, ret_norm, diff_lambda, diff_norm,
           gdn_conv, gdn_A_log, gdn_dt_bias, gdn_norm, w_branch, w_out,
           peer_wq, peer_keys, peer_u, peer_v, norm_f_w):
    x = jnp.concatenate([x_prompt.reshape(N_CTX, D_MODEL), x_sample.reshape(N_LAT, D_MODEL)], axis=0)
    cond = jnp.concatenate([c_ctx[None, :], c, jnp.zeros((8 - N_COND, D_MODEL), F32)], axis=0)
    mod_all = _modulation(cond, w_mod, b_mod)

    w_main = jnp.concatenate([w_in[:, :, :AB_OFF], w_in[:, :, AB_OFF + 32:]], axis=-1).astype(BF16)
    gate_cols = _gdn_gate_cols()
    w_ab = w_in[:, :, AB_OFF:AB_OFF + 32][:, :, gate_cols]
    w_ab = jnp.pad(w_ab, ((0, 0), (0, 0), (0, 0), (0, 128 - gate_cols.shape[1])))
    w_ab = w_ab.reshape(DEPTH, D_MODEL, gate_cols.shape[0] * 128).astype(BF16)
    wb_bf = w_branch.astype(BF16)
    wo_bf = w_out.astype(BF16)
    wq_bf = peer_wq.astype(BF16)

    ks, vs, srs, sgs = [], [], [], []
    for li in range(DEPTH):
        lw = dict(ret_decay=ret_decay[li], ret_norm=ret_norm[li], diff_lambda=diff_lambda[li],
                  diff_norm=diff_norm[li], gdn_conv=gdn_conv[li], gdn_A_log=gdn_A_log[li],
                  gdn_dt_bias=gdn_dt_bias[li], gdn_norm=gdn_norm[li])
        mod = mod_all[li, :N_COND]
        sh1, sc1, g1, sh2, sc2, g2 = [_per_block(m) for m in jnp.split(mod, 6, axis=-1)]
        h = _normmod(x, norm1_w[li], sc1, sh1, BF16)
        proj = _matmul(h, w_main[li], 512, 1024)
        gab = _matmul(h, w_ab[li], 512, 512)

        oa, sr_l = _retention(proj, ret_decay[li], ret_norm[li], SEQ, BATCH, 0)
        za, _ = _retention(proj, ret_decay[li], ret_norm[li], DEC_SEQ, DEC_BATCH, N_CTX, s0=state_ret[:, li])
        ob = _diff_attn(proj, diff_lambda[li], diff_norm[li], li, SEQ, BATCH, 0)
        zb = _diff_attn(proj, diff_lambda[li], diff_norm[li], li, DEC_SEQ, DEC_BATCH, N_CTX, cache_diff_k, cache_diff_v)
        k_l = proj[:N_CTX, C_DK:C_DV].reshape(BATCH, SEQ, D_HEADS, 2, D_HD)
        v_l = proj[:N_CTX, C_DV:C_GQKV].reshape(BATCH, SEQ, D_HEADS, 2 * D_HD)

        alog = gdn_A_log[li].T.reshape(2 * G_HEADS, 1, 1)
        dtb = gdn_dt_bias[li].T.reshape(2 * G_HEADS, 1, 1)
        qkv_c = _gdn_prep(proj, gdn_conv[li], SEQ, BATCH, 0)
        oc, sg_l = _gdn(qkv_c, proj, gab, alog, dtb, gdn_norm[li], None, SEQ, BATCH, 0)
        qkv_z = _gdn_prep(proj, gdn_conv[li], DEC_SEQ, DEC_BATCH, N_CTX)
        zc, _ = _gdn(qkv_z, proj, gab, alog, dtb, gdn_norm[li], state_gdn[:, li], DEC_SEQ, DEC_BATCH, N_CTX)
        ks.append(k_l); vs.append(v_l); srs.append(sr_l); sgs.append(sg_l)
        cat = lambda a, b: jnp.concatenate([a, b], axis=0)
        o_ret, o_diff, o_gdn = cat(oa, za), cat(ob, zb), cat(oc, zc)

        x, h2 = _merge(o_ret, o_diff, o_gdn, proj, wb_bf[li], wo_bf[li], x, g1, norm2_w[li], sc2, sh2)
        idx, gate = _peer_route(h2, wq_bf[li], peer_keys[li])
        x = _peer_experts(h2, idx, gate, peer_u[li].astype(BF16), peer_v[li].astype(BF16), x, g2)

    zeros_blk = jnp.zeros((N_MOD_BLK, 1, D_MODEL), F32)
    y = _normmod(x, norm_f_w, zeros_blk, zeros_blk, F32)
    y_prompt = y[:N_CTX].reshape(BATCH, SEQ, D_MODEL)
    y_sample = y[N_CTX:].reshape(DEC_BATCH, DEC_SEQ, D_MODEL)
    return (y_prompt, y_sample, jnp.stack(ks, axis=1), jnp.stack(vs, axis=1),
            jnp.stack(srs, axis=1), jnp.stack(sgs, axis=1))
```

```python
import functools
import math

import numpy as np
import jax
import jax.numpy as jnp
from jax import lax
from jax.experimental import pallas as pl
from jax.experimental.pallas import tpu as pltpu

F32 = jnp.float32
BF16 = jnp.bfloat16

D_MODEL = 1024
BATCH = 16
SEQ = 256
DEPTH = 2
DEC_BATCH = 2
DEC_SEQ = 1024
PAST_LEN = 256
GRID_W = 64
ROPE_BASE = 10000.0
EPS = 1e-6
CHUNK = 64
R_HEADS, R_DK, R_DV = 4, 128, 256
D_HEADS, D_HD = 8, 64
G_HEADS, G_DK, G_DV = 8, 128, 128
CONV_K = 7
G_QKV = G_HEADS * (2 * G_DK + G_DV)
N_BRANCH = 3
BRANCH_W = 1024
PEER_HEADS = 8
PEER_DQ = 256
N_KEYS = 128
PEER_TOPK = 16
TOK_BLOCK = 128

N_CTX = BATCH * SEQ
N_LAT = DEC_BATCH * DEC_SEQ
N_TOK = N_CTX + N_LAT
N_COND = 1 + DEC_BATCH
MOD_BLK = 256
N_MOD_BLK = N_TOK // MOD_BLK

C_RQ, C_RK, C_RV, C_RG = 0, 512, 1024, 2048
C_DQ, C_DK, C_DV = 3072, 4096, 5120
C_GQKV, C_GZ, C_MG = 6144, 9216, 10240
N_MAIN = 13312
AB_OFF = 10240

VMEM_LIMIT = 56 * 1024 * 1024


def _cparams(sem):
    return pltpu.CompilerParams(dimension_semantics=sem, vmem_limit_bytes=VMEM_LIMIT)


def _mod_kernel(c_ref, w_ref, b_ref, o_ref):
    c = c_ref[...]
    a = c * jax.nn.sigmoid(c)
    o_ref[0] = jnp.dot(a, w_ref[0], preferred_element_type=F32,
                       precision=lax.Precision.HIGHEST) + b_ref[0]


def _modulation(cond_pad, w_mod, b_mod):
    tn = 1536
    return pl.pallas_call(
        _mod_kernel,
        out_shape=jax.ShapeDtypeStruct((DEPTH, 8, 6 * D_MODEL), F32),
        grid=(DEPTH, 6 * D_MODEL // tn),
        in_specs=[pl.BlockSpec((8, D_MODEL), lambda l, j: (0, 0)),
                  pl.BlockSpec((1, D_MODEL, tn), lambda l, j: (l, 0, j)),
                  pl.BlockSpec((1, 1, tn), lambda l, j: (l, 0, j))],
        out_specs=pl.BlockSpec((1, 8, tn), lambda l, j: (l, 0, j)),
        compiler_params=_cparams(("arbitrary", "arbitrary")),
        name="modulation",
    )(cond_pad, w_mod, b_mod.reshape(DEPTH, 1, 6 * D_MODEL))


def _normmod_kernel(x_ref, w_ref, sc_ref, sh_ref, o_ref):
    x = x_ref[...]
    y = x * lax.rsqrt(jnp.mean(x * x, axis=-1, keepdims=True) + EPS)
    y = y * w_ref[...]
    o_ref[...] = (y * (1.0 + sc_ref[0]) + sh_ref[0]).astype(o_ref.dtype)


def _normmod(x, w, sc_blk, sh_blk, out_dtype):
    tm = MOD_BLK
    return pl.pallas_call(
        _normmod_kernel,
        out_shape=jax.ShapeDtypeStruct((N_TOK, D_MODEL), out_dtype),
        grid=(N_TOK // tm,),
        in_specs=[pl.BlockSpec((tm, D_MODEL), lambda i: (i, 0)),
                  pl.BlockSpec((1, D_MODEL), lambda i: (0, 0)),
                  pl.BlockSpec((1, 1, D_MODEL), lambda i: (i, 0, 0)),
                  pl.BlockSpec((1, 1, D_MODEL), lambda i: (i, 0, 0))],
        out_specs=pl.BlockSpec((tm, D_MODEL), lambda i: (i, 0)),
        compiler_params=_cparams(("arbitrary",)),
        name="normmod",
    )(x, w.reshape(1, D_MODEL), sc_blk, sh_blk)


def _matmul_kernel(a_ref, b_ref, o_ref):
    o_ref[...] = jnp.dot(a_ref[...], b_ref[...], preferred_element_type=F32)


def _matmul(a, b, tm, tn):
    m, k = a.shape
    n = b.shape[1]
    return pl.pallas_call(
        _matmul_kernel,
        out_shape=jax.ShapeDtypeStruct((m, n), F32),
        grid=(n // tn, m // tm),
        in_specs=[pl.BlockSpec((tm, k), lambda j, i: (i, 0)),
                  pl.BlockSpec((k, tn), lambda j, i: (0, j))],
        out_specs=pl.BlockSpec((tm, tn), lambda j, i: (i, j)),
        compiler_params=_cparams(("arbitrary", "arbitrary")),
        name="matmul",
    )(a, b)


def _merge_kernel(oa_ref, ob_ref, oc_ref, mga_ref, mgb_ref, mgc_ref, wb_ref, wo_ref, x_ref, g1_ref,
                  n2_ref, sc_ref, sh_ref, xo_ref, h_ref):
    mix = None
    for n, (o_ref, mg_ref) in enumerate(((oa_ref, mga_ref), (ob_ref, mgb_ref), (oc_ref, mgc_ref))):
        merged = jnp.dot(o_ref[...], wb_ref[n], preferred_element_type=F32)
        gate = jax.nn.sigmoid(mg_ref[...])
        mix = gate * merged if mix is None else mix + gate * merged
    m = jnp.dot(mix.astype(BF16), wo_ref[...], preferred_element_type=F32)
    x = x_ref[...] + g1_ref[0] * m
    xo_ref[...] = x
    y = x * lax.rsqrt(jnp.mean(x * x, axis=-1, keepdims=True) + EPS)
    y = y * n2_ref[...]
    h_ref[...] = (y * (1.0 + sc_ref[0]) + sh_ref[0]).astype(h_ref.dtype)


def _merge(o_ret, o_diff, o_gdn, proj, wb, wo, x, g1_blk, n2w, sc2_blk, sh2_blk):
    tm = MOD_BLK
    tok = lambda i: (i, 0)
    blk = lambda i: (i, 0, 0)
    return pl.pallas_call(
        _merge_kernel,
        out_shape=(jax.ShapeDtypeStruct((N_TOK, D_MODEL), F32),
                   jax.ShapeDtypeStruct((N_TOK, D_MODEL), BF16)),
        grid=(N_TOK // tm,),
        in_specs=[pl.BlockSpec((tm, BRANCH_W), tok),
                  pl.BlockSpec((tm, BRANCH_W), tok),
                  pl.BlockSpec((tm, BRANCH_W), tok),
                  pl.BlockSpec((tm, D_MODEL), lambda i: (i, C_MG // D_MODEL)),
                  pl.BlockSpec((tm, D_MODEL), lambda i: (i, C_MG // D_MODEL + 1)),
                  pl.BlockSpec((tm, D_MODEL), lambda i: (i, C_MG // D_MODEL + 2)),
                  pl.BlockSpec((N_BRANCH, BRANCH_W, D_MODEL), lambda i: (0, 0, 0)),
                  pl.BlockSpec((D_MODEL, D_MODEL), lambda i: (0, 0)),
                  pl.BlockSpec((tm, D_MODEL), tok),
                  pl.BlockSpec((1, 1, D_MODEL), blk),
                  pl.BlockSpec((1, D_MODEL), lambda i: (0, 0)),
                  pl.BlockSpec((1, 1, D_MODEL), blk),
                  pl.BlockSpec((1, 1, D_MODEL), blk)],
        out_specs=(pl.BlockSpec((tm, D_MODEL), tok), pl.BlockSpec((tm, D_MODEL), tok)),
        compiler_params=_cparams(("arbitrary",)),
        name="merge",
    )(o_ret, o_diff, o_gdn, proj, proj, proj, wb, wo, x, g1_blk, n2w.reshape(1, D_MODEL), sc2_blk, sh2_blk)


ROUTE_T = 128
ROUTE_HEADS = 2
N_PAIR = PEER_HEADS * PEER_TOPK
N_CAND = PEER_TOPK + (PEER_TOPK // 2) * (PEER_TOPK // 2 - 1) + PEER_TOPK // 2
NEG_INF = float("-inf")


def _split_bf16(x):
    hi = x.astype(BF16)
    return hi, (x - hi.astype(F32)).astype(BF16)


def _topk_rows(scores, rows, n_rows, payloads=None):
    scores = list(scores)
    vals = [[] for _ in scores]
    outs = [[] for _ in scores]
    for _ in range(PEER_TOPK):
        for n, s in enumerate(scores):
            m = jnp.max(s, axis=0, keepdims=True)
            pos = jnp.min(jnp.where(s == m, rows, n_rows), axis=0, keepdims=True)
            sel = rows == pos
            vals[n].append(m)
            if payloads is None:
                outs[n].append(pos)
            else:
                outs[n].append(jnp.sum(jnp.where(sel, payloads[n], 0), axis=0, keepdims=True))
            scores[n] = jnp.where(sel, NEG_INF, s)
    return [(jnp.concatenate(v, axis=0), jnp.concatenate(o, axis=0)) for v, o in zip(vals, outs)]


def _route_kernel(h_ref, wq_ref, khi_ref, klo_ref, idx_ref, gate_ref, q_scr, idx_scr, gate_scr):
    t = h_ref.shape[0]
    q = jnp.dot(h_ref[...], wq_ref[...], preferred_element_type=F32)
    for hp in range(2 * PEER_HEADS):
        q_scr[hp] = q[:, hp * 128:(hp + 1) * 128]
    rows_k = lax.broadcasted_iota(jnp.int32, (N_KEYS, t), 0)
    rows_c = lax.broadcasted_iota(jnp.int32, (N_CAND, t), 0)
    nt = (((1,), (1,)), ((), ()))

    half = PEER_TOPK // 2
    groups = [(slice(0, 1), slice(0, PEER_TOPK)), (slice(1, 2), slice(0, half))]
    groups += [(slice(k, k + 1), slice(0, half)) for k in range(2, half)]
    groups += [(slice(half, PEER_TOPK), slice(0, 1))]

    def heads(hh, carry):
        hs = [ROUTE_HEADS * hh + n for n in range(ROUTE_HEADS)]
        scores = []
        for h in hs:
            for p in range(2):
                qhi, qlo = _split_bf16(q_scr[2 * h + p])
                scores.append(lax.dot_general(khi_ref[p], qhi, nt, preferred_element_type=F32)
                              + lax.dot_general(khi_ref[p], qlo, nt, preferred_element_type=F32)
                              + lax.dot_general(klo_ref[p], qhi, nt, preferred_element_type=F32))
        tops = _topk_rows(scores, rows_k, N_KEYS)
        cands, cand_is = [], []
        for n in range(ROUTE_HEADS):
            (v1, i1), (v2, i2) = tops[2 * n], tops[2 * n + 1]
            cands.append(jnp.concatenate([v1[a] + v2[b] for a, b in groups], axis=0))
            cand_is.append(jnp.concatenate([i1[a] * N_KEYS + i2[b] for a, b in groups], axis=0))
        for h, (best, bidx) in zip(hs, _topk_rows(cands, rows_c, N_CAND, payloads=cand_is)):
            e = jnp.exp(best - best[0:1])
            gate_scr[h] = e / jnp.sum(e, axis=0, keepdims=True)
            idx_scr[h] = bidx
        return carry

    lax.fori_loop(0, PEER_HEADS // ROUTE_HEADS, heads, 0)
    idx_ref[...] = idx_scr[...].reshape(N_PAIR, t).T
    gate_ref[...] = gate_scr[...].reshape(N_PAIR, t).T


def _peer_route(h2, wq, keys):
    khi, klo = _split_bf16(keys)
    t = ROUTE_T
    n_tok = h2.shape[0]
    return pl.pallas_call(
        _route_kernel,
        out_shape=(jax.ShapeDtypeStruct((n_tok, N_PAIR), jnp.int32),
                   jax.ShapeDtypeStruct((n_tok, N_PAIR), F32)),
        grid=(n_tok // t,),
        in_specs=[pl.BlockSpec((t, D_MODEL), lambda i: (i, 0)),
                  pl.BlockSpec((D_MODEL, PEER_HEADS * PEER_DQ), lambda i: (0, 0)),
                  pl.BlockSpec((2, N_KEYS, PEER_DQ // 2), lambda i: (0, 0, 0)),
                  pl.BlockSpec((2, N_KEYS, PEER_DQ // 2), lambda i: (0, 0, 0))],
        out_specs=(pl.BlockSpec((t, N_PAIR), lambda i: (i, 0)),
                   pl.BlockSpec((t, N_PAIR), lambda i: (i, 0))),
        scratch_shapes=[pltpu.VMEM((2 * PEER_HEADS, t, PEER_DQ // 2), F32),
                        pltpu.VMEM((PEER_HEADS, PEER_TOPK, t), jnp.int32),
                        pltpu.VMEM((PEER_HEADS, PEER_TOPK, t), F32)],
        compiler_params=_cparams(("arbitrary",)),
        name="peer_route",
    )(h2, wq, khi, klo)


EXP_TB = 256
EXP_EB = 2048
EXP_SUB = 256
EXP_AHEAD = 2
N_EXPERTS = N_KEYS * N_KEYS


def _expert_kernel(h_ref, idx_ref, gate_ref, u_ref, v_ref, x_ref, g2_ref, o_ref, g_scr, acc_ref):
    j = pl.program_id(1)
    tb = h_ref.shape[0]

    @pl.when(j == 0)
    def _():
        acc_ref[...] = jnp.zeros_like(acc_ref)
        sub = lax.broadcasted_iota(jnp.int32, (N_KEYS, N_PAIR), 0)

        def body(t, carry):
            e = idx_ref[pl.ds(t, 1), :]
            g = gate_ref[pl.ds(t, 1), :]
            xa = jnp.where(sub == (e >> 7), 1.0, 0.0).astype(BF16)
            yb = jnp.where(sub == (e & (N_KEYS - 1)), g, 0.0).astype(BF16)
            g_scr[pl.ds(pl.multiple_of(t * N_KEYS, N_KEYS), N_KEYS), :] = _dot_nt(xa, yb)
            return carry

        lax.fori_loop(0, tb, body, 0, unroll=16)

    h = h_ref[...]
    nsub = EXP_EB // EXP_SUB
    per = EXP_SUB // N_KEYS
    score = lambda c: _dot_nt(h, u_ref[c * EXP_SUB:(c + 1) * EXP_SUB, :])
    s = [score(c) for c in range(EXP_AHEAD)]
    acc = acc_ref[...]
    for c in range(nsub):
        gj = jnp.concatenate([g_scr[pl.ds(j * (nsub * per) + c * per + i, tb, stride=N_KEYS), :]
                              for i in range(per)], axis=1)
        w = 0.5 * s[c] * (1.0 + lax.erf(s[c] * (2.0 ** -0.5))) * gj
        acc = acc + jnp.dot(w.astype(BF16), v_ref[c * EXP_SUB:(c + 1) * EXP_SUB, :], preferred_element_type=F32)
        if c + EXP_AHEAD < nsub:
            s.append(score(c + EXP_AHEAD))
    acc_ref[...] = acc

    @pl.when(j == pl.num_programs(1) - 1)
    def _():
        o_ref[...] = x_ref[...] + g2_ref[0] * acc_ref[...]


def _peer_experts(h2, idx, gate, u, v, x, g2_blk):
    tb, eb = EXP_TB, EXP_EB
    n_tok = h2.shape[0]
    return pl.pallas_call(
        _expert_kernel,
        out_shape=jax.ShapeDtypeStruct((n_tok, D_MODEL), F32),
        grid=(n_tok // tb, N_EXPERTS // eb),
        in_specs=[pl.BlockSpec((tb, D_MODEL), lambda i, j: (i, 0)),
                  pl.BlockSpec((tb, N_PAIR), lambda i, j: (i, 0)),
                  pl.BlockSpec((tb, N_PAIR), lambda i, j: (i, 0)),
                  pl.BlockSpec((eb, D_MODEL), lambda i, j: (j, 0)),
                  pl.BlockSpec((eb, D_MODEL), lambda i, j: (j, 0)),
                  pl.BlockSpec((tb, D_MODEL), lambda i, j: (i, 0)),
                  pl.BlockSpec((1, 1, D_MODEL), lambda i, j: (i * tb // MOD_BLK, 0, 0))],
        out_specs=pl.BlockSpec((tb, D_MODEL), lambda i, j: (i, 0)),
        scratch_shapes=[pltpu.VMEM((tb * N_KEYS, N_KEYS), F32),
                        pltpu.VMEM((tb, D_MODEL), F32)],
        compiler_params=_cparams(("arbitrary", "arbitrary")),
        name="peer_experts",
    )(h2, idx, gate, u, v, x, g2_blk)


def _drop_ref(fn, pos, *refs):
    return fn(*refs[:pos], *refs[pos + 1:])


def _branch_call(kernel_fn, dst, in_specs, args, out_shape, out_specs, **kw):
    if dst is not None:
        pos = len(in_specs)
        in_specs = list(in_specs) + [pl.BlockSpec(memory_space=pl.ANY)]
        args = list(args) + [dst]
        kernel_fn = functools.partial(_drop_ref, kernel_fn, pos)
        kw["input_output_aliases"] = {pos: 0}
    return pl.pallas_call(kernel_fn, out_shape=tuple(out_shape), in_specs=in_specs, out_specs=tuple(out_specs),
                          **kw)(*args)


CONV_ROWS = 256
CONV_HALO = 8


def _gdn_prep_kernel(x_ref, w_ref, o_ref):
    sec = pl.program_id(1)
    L = x_ref.shape[0]
    half = CONV_K // 2
    for r0 in range(0, L, CONV_ROWS):
        s0, s1 = max(r0 - CONV_HALO, 0), min(r0 + CONV_ROWS + CONV_HALO, L)
        n = s1 - s0
        t_idx = lax.broadcasted_iota(jnp.int32, (n, 128), 0) + s0
        for cb in range(G_HEADS):
            cols = slice(cb * 128, (cb + 1) * 128)
            x = x_ref[s0:s1, cols]
            w = w_ref[0, :, cols]
            acc = x * w[half:half + 1]
            for d in range(-half, half + 1):
                if d == 0:
                    continue
                xs = pltpu.roll(x, (-d) % n, axis=0)
                valid = (t_idx + d >= 0) if d < 0 else (t_idx + d < L)
                acc = acc + jnp.where(valid, xs, 0.0) * w[half + d:half + d + 1]
            y = acc[r0 - s0:r0 - s0 + CONV_ROWS]
            y = y * jax.nn.sigmoid(y)
            inv = lax.rsqrt(jnp.sum(y * y, axis=-1, keepdims=True) + EPS)
            o_ref[r0:r0 + CONV_ROWS, cols] = y * jnp.where(sec < 2, inv, 1.0)


def _gdn_prep(proj, conv_w, seq_len, n_seq, tok_off):
    blk0 = tok_off // seq_len
    sec0 = C_GQKV // 1024
    return pl.pallas_call(
        _gdn_prep_kernel,
        out_shape=jax.ShapeDtypeStruct((n_seq * seq_len, G_QKV), F32),
        grid=(n_seq, 3),
        in_specs=[pl.BlockSpec((seq_len, 1024), lambda s, c: (blk0 + s, sec0 + c)),
                  pl.BlockSpec((1, CONV_K, 1024), lambda s, c: (0, 0, c))],
        out_specs=pl.BlockSpec((seq_len, 1024), lambda s, c: (s, c)),
        compiler_params=_cparams(("arbitrary", "arbitrary")),
        name="gdn_prep",
    )(proj, conv_w.reshape(1, CONV_K, G_QKV))


GDN_HB = 2


def _dot(a, b):
    return jnp.dot(a, b, preferred_element_type=F32)


def _dot_nt(a, b):
    return lax.dot_general(a, b, (((1,), (1,)), ((), ())), preferred_element_type=F32)


def _dot_tn(a, b):
    return lax.dot_general(a, b, (((0,), (0,)), ((), ())), preferred_element_type=F32)


def _mask_dot3(m, b):
    m16 = m.astype(BF16)
    bh = b.astype(BF16)
    r1 = b - bh.astype(F32)
    bm = r1.astype(BF16)
    bl = (r1 - bm.astype(F32)).astype(BF16)
    return _dot(m16, bh) + _dot(m16, bm) + _dot(m16, bl)


def _gdn_kernel(*refs, seq_len, has_init, emit_state):
    it = iter(refs)
    q_ref, k_ref, v_ref, z_ref, gab_ref, alog_ref, dtb_ref, nw_ref = [next(it) for _ in range(8)]
    s0_ref = next(it) if has_init else None
    o_ref = next(it)
    so_ref = next(it) if emit_state else None
    u_s, w_s, qg_s, kd_s, qk_s, gl_s, o_s = [next(it) for _ in range(7)]
    C = CHUNK
    nc = seq_len // C
    ri = lax.broadcasted_iota(jnp.int32, (C, C), 0)
    ci = lax.broadcasted_iota(jnp.int32, (C, C), 1)
    eye = jnp.where(ri == ci, 1.0, 0.0)
    ones = jnp.ones((C, C), F32)
    scale = G_DK ** -0.5
    incl = (ri >= ci, ri <= ci)
    strict = (ri > ci, ri < ci)
    levels = []
    for lv in range(6):
        b = 1 << lv
        same = (ri >> (lv + 1)) == (ci >> (lv + 1))
        r_hi, c_hi = (ri & b) != 0, (ci & b) != 0
        levels.append((same & r_hi & jnp.logical_not(c_hi), same & jnp.logical_not(r_hi) & c_hi))

    GROUP = 4
    HB = GDN_HB
    lanes = [slice(hh * 128, (hh + 1) * 128) for hh in range(HB)]
    tri = [jnp.where(m, 1.0, 0.0) for m in incl]

    def prep(grp, carry):
        chunks = [grp * GROUP + j for j in range(GROUP)]
        rows = [pl.ds(pl.multiple_of(c * C, C), C) for c in chunks]
        hj = [(hh, j) for hh in range(HB) for j in range(GROUP)]
        q = {(hh, j): q_ref[rows[j], lanes[hh]] * scale for hh, j in hj}
        k = {(hh, j): k_ref[rows[j], lanes[hh]] for hh, j in hj}
        v = {(hh, j): v_ref[rows[j], lanes[hh]] for hh, j in hj}
        k16 = {key: x.astype(BF16) for key, x in k.items()}
        kk = {key: _dot_nt(x, x) for key, x in k16.items()}
        qk = {key: _dot_nt(q[key].astype(BF16), k16[key]) for key in hj}
        probs = [(hh, j, d) for hh in range(HB) for j in range(GROUP) for d in range(2)]
        gab = [gab_ref[r, :] for r in rows]
        ga = lambda hh, j, d: gab[j][:, 2 * hh + d:2 * hh + d + 1]
        gb = lambda hh, j, d: gab[j][:, 2 * HB + 2 * hh + d:2 * HB + 2 * hh + d + 1]
        g = [-jnp.exp(alog_ref[2 * hh + d]) * jax.nn.softplus(ga(hh, j, d) + dtb_ref[2 * hh + d])
             for hh, j, d in probs]
        beta = [jax.nn.sigmoid(gb(hh, j, d)) for hh, j, d in probs]
        gcol = [_mask_dot3(tri[d], jnp.broadcast_to(g[p], (C, 128))) for p, (hh, j, d) in enumerate(probs)]
        grow = [_mask_dot3(ones, jnp.where(incl[1 - d], jnp.broadcast_to(g[p], (C, C)), 0.0))
                for p, (hh, j, d) in enumerate(probs)]
        dec = [jnp.exp(jnp.where(incl[d], gcol[p][:, :C] - grow[p], NEG_INF)) for p, (hh, j, d) in enumerate(probs)]
        a = [jnp.where(strict[d], kk[hh, j] * beta[p] * dec[p], 0.0) for p, (hh, j, d) in enumerate(probs)]
        t = [eye - jnp.where(levels[0][d], a[p], 0.0) for p, (hh, j, d) in enumerate(probs)]
        for lv in range(1, 6):
            t16 = [x.astype(BF16) for x in t]
            ct = [_dot(jnp.where(levels[lv][d], a[p], 0.0).astype(BF16), t16[p])
                  for p, (hh, j, d) in enumerate(probs)]
            t = [t[p] - _dot(t16[p], ct[p].astype(BF16)) for p in range(len(probs))]
        for p, (hh, j, d) in enumerate(probs):
            n = 2 * hh + d
            eg = jnp.exp(gcol[p])
            t16 = t[p].astype(BF16)
            glast = gcol[p][C - 1:C, :] if d == 0 else gcol[p][0:1, :]
            u_s[n, rows[j], :] = _dot(t16, (v[hh, j] * beta[p]).astype(BF16))
            w_s[n, rows[j], :] = _dot(t16, (k[hh, j] * beta[p] * eg).astype(BF16))
            qg_s[n, rows[j], :] = q[hh, j] * eg
            kd_s[n, rows[j], :] = k[hh, j] * jnp.exp(glast - gcol[p])
            qk_s[n, rows[j], :] = qk[hh, j] * dec[p]
            gl_s[n * nc + chunks[j]] = jnp.broadcast_to(jnp.exp(glast), (8, 128))
        return carry

    lax.fori_loop(0, nc // GROUP, prep, 0)
    o_s[...] = jnp.zeros_like(o_s)
    chains = [(hh, d) for hh in range(HB) for d in range(2)]

    def step(i, states):
        cs = (i, nc - 1 - i)
        rows = [pl.ds(pl.multiple_of(c * C, C), C) for c in cs]
        s16 = [s.astype(BF16) for s in states]
        ws = [_dot(w_s[n, rows[d], :].astype(BF16), s16[n]) for n, (hh, d) in enumerate(chains)]
        un16 = [(u_s[n, rows[d], :] - ws[n]).astype(BF16) for n, (hh, d) in enumerate(chains)]
        o = [_dot(qg_s[n, rows[d], :].astype(BF16), s16[n]) + _dot(qk_s[n, rows[d], :].astype(BF16), un16[n])
             for n, (hh, d) in enumerate(chains)]
        new = tuple(states[n] * gl_s[n * nc + cs[d]][0:1, :] + _dot_tn(kd_s[n, rows[d], :].astype(BF16), un16[n])
                    for n, (hh, d) in enumerate(chains))
        for n, (hh, d) in enumerate(chains):
            o_s[rows[d], lanes[hh]] += o[n]
        return new

    if has_init:
        init = tuple(s0_ref[0, d, hh] for hh, d in chains)
    else:
        init = (jnp.zeros((G_DK, G_DV), F32),) * len(chains)
    final = lax.fori_loop(0, nc, step, init)
    if emit_state:
        for n, (hh, d) in enumerate(chains):
            so_ref[0, d, hh] = final[n]
    for hh in range(HB):
        o = o_s[:, lanes[hh]]
        y = o * lax.rsqrt(jnp.mean(o * o, axis=-1, keepdims=True) + EPS) * nw_ref[...]
        z = z_ref[:, lanes[hh]]
        o_ref[:, lanes[hh]] = (y * (z * jax.nn.sigmoid(z))).astype(o_ref.dtype)


def _gdn_gate_cols():
    cols = []
    for blk in range(G_HEADS // GDN_HB):
        heads = range(blk * GDN_HB, (blk + 1) * GDN_HB)
        cols.append([kind * 2 * G_HEADS + d * G_HEADS + h for kind in range(2) for h in heads for d in range(2)])
    return np.asarray(cols, np.int32)


def _gdn(qkv, proj, gab, alog, dtb, nw, s0, seq_len, n_seq, tok_off, dst=None):
    has_init = s0 is not None
    emit_state = not has_init
    blk0 = tok_off // seq_len
    hb = GDN_HB
    wl = hb * 128
    nblk = G_HEADS // hb
    tokcol = lambda off: pl.BlockSpec((seq_len, wl), lambda s, h: (s, off + h))
    in_specs = [tokcol(0), tokcol(nblk), tokcol(2 * nblk),
                pl.BlockSpec((seq_len, wl), lambda s, h: (blk0 + s, C_GZ // wl + h)),
                pl.BlockSpec((seq_len, 128), lambda s, h: (blk0 + s, h)),
                pl.BlockSpec((2 * hb, 1, 1), lambda s, h: (h, 0, 0)),
                pl.BlockSpec((2 * hb, 1, 1), lambda s, h: (h, 0, 0)),
                pl.BlockSpec((1, G_DV), lambda s, h: (0, 0))]
    args = [qkv, qkv, qkv, proj, gab, alog, dtb, nw.reshape(1, G_DV)]
    if has_init:
        in_specs.append(pl.BlockSpec((1, 2, hb, G_DK, G_DV), lambda s, h: (s, 0, h, 0, 0)))
        args.append(s0)
    out_shape = [jax.ShapeDtypeStruct((N_TOK, BRANCH_W), BF16)]
    out_specs = [pl.BlockSpec((seq_len, wl), lambda s, h: (blk0 + s, h))]
    if emit_state:
        out_shape.append(jax.ShapeDtypeStruct((n_seq, 2, G_HEADS, G_DK, G_DV), F32))
        out_specs.append(pl.BlockSpec((1, 2, hb, G_DK, G_DV), lambda s, h: (s, 0, h, 0, 0)))
    nc = seq_len // CHUNK
    res = _branch_call(
        functools.partial(_gdn_kernel, seq_len=seq_len, has_init=has_init, emit_state=emit_state),
        dst, in_specs, args, out_shape, out_specs,
        grid=(n_seq, nblk),
        scratch_shapes=[pltpu.VMEM((2 * hb, seq_len, 128), F32)] * 4
                       + [pltpu.VMEM((2 * hb, seq_len, CHUNK), F32), pltpu.VMEM((2 * hb * nc, 8, 128), F32),
                          pltpu.VMEM((seq_len, wl), F32)],
        compiler_params=_cparams(("arbitrary", "arbitrary")),
        name="gdn",
    )
    return res if emit_state else (res[0], None)


def _rope_tables(seq_len, dim):
    n_rows = seq_len // GRID_W
    row = np.repeat(np.arange(n_rows), GRID_W).astype(np.float32)
    col = np.tile(np.arange(GRID_W), n_rows).astype(np.float32)
    nf = dim // 4
    inv = np.power(np.float32(ROPE_BASE), -np.arange(nf, dtype=np.float32) / np.float32(nf)).astype(np.float32)
    ang_r = row[:, None] * inv
    ang_c = col[:, None] * inv
    ang = np.concatenate([ang_r, ang_r, ang_c, ang_c], axis=-1)
    ang = np.tile(ang, (1, 128 // dim))
    return jnp.asarray(np.cos(ang), F32), jnp.asarray(np.sin(ang), F32)


def _rope(x, cos, sin, dim):
    quarter = dim // 4
    lane = lax.broadcasted_iota(jnp.int32, x.shape, 1)
    first = (lane & (2 * quarter - 1)) < quarter
    rot = jnp.where(first, -pltpu.roll(x, 128 - quarter, axis=1), pltpu.roll(x, quarter, axis=1))
    return x * cos + rot * sin


Q_TILE = 256


def _retention_kernel(*refs, seq_len, latent):
    it = iter(refs)
    q_ref, k_ref, v_ref, rg_ref, lg_ref, nw_ref = [next(it) for _ in range(6)]
    if latent:
        cos_ref, sin_ref, s0_ref = next(it), next(it), next(it)
    o_ref = next(it)
    so_ref = None if latent else next(it)
    L = seq_len
    lg = lg_ref[0]
    lg = jnp.minimum(lg, 0.0) - jnp.log(1.0 + jnp.exp(-jnp.abs(lg)))
    lgf, lgb = lg[0:1, :], lg[1:2, :]
    k = k_ref[...]
    if latent:
        k = _rope(k, cos_ref[...], sin_ref[...], R_DK)
    k16 = k.astype(BF16)
    v16 = v_ref[...].astype(BF16)
    for qt in range(L // Q_TILE):
        rows = slice(qt * Q_TILE, (qt + 1) * Q_TILE)
        q = q_ref[rows, :]
        if latent:
            q = _rope(q, cos_ref[rows, :], sin_ref[rows, :], R_DK)
        q16 = (q * R_DK ** -0.5).astype(BF16)
        a = _dot_nt(q16, k16)
        i = lax.broadcasted_iota(jnp.int32, (Q_TILE, L), 0) + qt * Q_TILE
        j = lax.broadcasted_iota(jnp.int32, (Q_TILE, L), 1)
        dist = (i - j).astype(F32)
        dec = (jnp.exp(jnp.where(dist >= 0, lgf * dist, NEG_INF))
               + jnp.exp(jnp.where(dist <= 0, -lgb * dist, NEG_INF)))
        o = _dot((a * dec).astype(BF16), v16)
        if latent:
            pos = (lax.broadcasted_iota(jnp.int32, (Q_TILE, 1), 0) + qt * Q_TILE).astype(F32)
            o = o + _dot(q16, s0_ref[0, 0, 0].astype(BF16)) * jnp.exp(lgf * (pos + 1.0))
            o = o + _dot(q16, s0_ref[0, 1, 0].astype(BF16)) * jnp.exp(lgb * (L - pos))
        y = o * lax.rsqrt(jnp.mean(o * o, axis=-1, keepdims=True) + EPS) * nw_ref[...]
        g = rg_ref[rows, :]
        o_ref[rows, :] = (y * (g * jax.nn.sigmoid(g))).astype(o_ref.dtype)
    if not latent:
        pos = lax.broadcasted_iota(jnp.int32, (L, 1), 0).astype(F32)
        so_ref[0, 0, 0] = _dot_tn((k * jnp.exp(lgf * (L - 1.0 - pos))).astype(BF16), v16)
        so_ref[0, 1, 0] = _dot_tn((k * jnp.exp(lgb * pos)).astype(BF16), v16)


def _retention(proj, decay, nw, seq_len, n_seq, tok_off, s0=None, dst=None):
    latent = s0 is not None
    blk0 = tok_off // seq_len
    in_specs = [pl.BlockSpec((seq_len, R_DK), lambda s, h: (blk0 + s, C_RQ // R_DK + h)),
                pl.BlockSpec((seq_len, R_DK), lambda s, h: (blk0 + s, C_RK // R_DK + h)),
                pl.BlockSpec((seq_len, R_DV), lambda s, h: (blk0 + s, C_RV // R_DV + h)),
                pl.BlockSpec((seq_len, R_DV), lambda s, h: (blk0 + s, C_RG // R_DV + h)),
                pl.BlockSpec((1, 2, 1), lambda s, h: (h, 0, 0)),
                pl.BlockSpec((1, R_DV), lambda s, h: (0, 0))]
    args = [proj, proj, proj, proj, decay.T.reshape(R_HEADS, 2, 1), nw.reshape(1, R_DV)]
    out_shape = [jax.ShapeDtypeStruct((N_TOK, BRANCH_W), BF16)]
    out_specs = [pl.BlockSpec((seq_len, R_DV), lambda s, h: (blk0 + s, h))]
    if latent:
        cos, sin = _rope_tables(seq_len, R_DK)
        in_specs += [pl.BlockSpec((seq_len, 128), lambda s, h: (0, 0)),
                     pl.BlockSpec((seq_len, 128), lambda s, h: (0, 0)),
                     pl.BlockSpec((1, 2, 1, R_DK, R_DV), lambda s, h: (s, 0, h, 0, 0))]
        args += [cos, sin, s0]
    else:
        out_shape.append(jax.ShapeDtypeStruct((n_seq, 2, R_HEADS, R_DK, R_DV), F32))
        out_specs.append(pl.BlockSpec((1, 2, 1, R_DK, R_DV), lambda s, h: (s, 0, h, 0, 0)))
    res = _branch_call(
        functools.partial(_retention_kernel, seq_len=seq_len, latent=latent),
        dst, in_specs, args, out_shape, out_specs,
        grid=(n_seq, R_HEADS),
        compiler_params=_cparams(("arbitrary", "arbitrary")),
        name="retention",
    )
    return (res[0], None) if latent else res


DA_HB = 4


def _diff_attn_kernel(*refs, seq_len, latent, lam_init):
    it = iter(refs)
    q_ref, k_ref, v_ref, lp_ref, nw_ref = [next(it) for _ in range(5)]
    if latent:
        cos_ref, sin_ref, ck_ref, cv_ref = [next(it) for _ in range(4)]
    o_ref = next(it)
    L = seq_len
    lp = lp_ref[...]
    lam = (jnp.exp(jnp.sum(lp[0:1] * lp[1:2], axis=-1, keepdims=True))
           - jnp.exp(jnp.sum(lp[2:3] * lp[3:4], axis=-1, keepdims=True)) + lam_init)
    w = 2 * D_HD
    lanes = [slice(hh * w, (hh + 1) * w) for hh in range(DA_HB)]
    keys, vals = [], []
    for hh in range(DA_HB):
        k = k_ref[:, lanes[hh]]
        if latent:
            k = _rope(k, cos_ref[...], sin_ref[...], D_HD)
        keys.append([k.astype(BF16)] + ([ck_ref[0, 0, :, lanes[hh]].astype(BF16)] if latent else []))
        vals.append([v_ref[:, lanes[hh]].astype(BF16)] + ([cv_ref[0, 0, :, lanes[hh]].astype(BF16)] if latent else []))
    lane = lax.broadcasted_iota(jnp.int32, (Q_TILE, w), 1)
    for qt in range(L // Q_TILE):
        rows = slice(qt * Q_TILE, (qt + 1) * Q_TILE)
        hp = [(hh, p) for hh in range(DA_HB) for p in range(2)]
        qs = []
        for hh in range(DA_HB):
            q = q_ref[rows, lanes[hh]]
            if latent:
                q = _rope(q, cos_ref[rows, :], sin_ref[rows, :], D_HD)
            qs.append(q * D_HD ** -0.5)
        s = [[_dot_nt(jnp.where((lane >= D_HD) == (p == 1), qs[hh], 0.0).astype(BF16), kk) for kk in keys[hh]]
             for hh, p in hp]
        outs = []
        for n, (hh, p) in enumerate(hp):
            m = s[n][0].max(axis=-1, keepdims=True)
            for x in s[n][1:]:
                m = jnp.maximum(m, x.max(axis=-1, keepdims=True))
            e = [jnp.exp(x - m) for x in s[n]]
            z = sum(x.sum(axis=-1, keepdims=True) for x in e)
            pv = sum(_dot(x.astype(BF16), vv) for x, vv in zip(e, vals[hh]))
            outs.append(pv / z)
        for hh in range(DA_HB):
            o = outs[2 * hh] - lam * outs[2 * hh + 1]
            y = o * lax.rsqrt(jnp.mean(o * o, axis=-1, keepdims=True) + EPS) * nw_ref[...]
            o_ref[rows, lanes[hh]] = (y * (1.0 - lam_init)).astype(o_ref.dtype)


def _diff_attn(proj, lam_p, nw, li, seq_len, n_seq, tok_off, cache_k=None, cache_v=None, dst=None):
    latent = cache_k is not None
    blk0 = tok_off // seq_len
    w = 2 * D_HD
    wl = DA_HB * w
    in_specs = [pl.BlockSpec((seq_len, wl), lambda s, h: (blk0 + s, C_DQ // wl + h)),
                pl.BlockSpec((seq_len, wl), lambda s, h: (blk0 + s, C_DK // wl + h)),
                pl.BlockSpec((seq_len, wl), lambda s, h: (blk0 + s, C_DV // wl + h)),
                pl.BlockSpec((4, D_HD), lambda s, h: (0, 0)),
                pl.BlockSpec((1, w), lambda s, h: (0, 0))]
    args = [proj, proj, proj, lam_p, nw.reshape(1, w)]
    if latent:
        cos, sin = _rope_tables(seq_len, D_HD)
        in_specs += [pl.BlockSpec((seq_len, 128), lambda s, h: (0, 0)),
                     pl.BlockSpec((seq_len, 128), lambda s, h: (0, 0)),
                     pl.BlockSpec((1, 1, PAST_LEN, wl), lambda s, h: (s, li, 0, h)),
                     pl.BlockSpec((1, 1, PAST_LEN, wl), lambda s, h: (s, li, 0, h))]
        args += [cos, sin, cache_k.reshape(DEC_BATCH, DEPTH, PAST_LEN, D_HEADS * w),
                 cache_v.reshape(DEC_BATCH, DEPTH, PAST_LEN, D_HEADS * w)]
    lam_init = 0.8 - 0.6 * math.exp(-0.3 * li)
    return _branch_call(
        functools.partial(_diff_attn_kernel, seq_len=seq_len, latent=latent, lam_init=lam_init),
        dst, in_specs, args,
        [jax.ShapeDtypeStruct((N_TOK, BRANCH_W), BF16)],
        [pl.BlockSpec((seq_len, wl), lambda s, h: (blk0 + s, h))],
        grid=(n_seq, D_HEADS // DA_HB),
        compiler_params=_cparams(("arbitrary", "arbitrary")),
        name="diff_attn",
    )[0]


def _rmsnorm(x, w):
    return x * lax.rsqrt(jnp.mean(x * x, axis=-1, keepdims=True) + EPS) * w


def _l2norm(x):
    return x * lax.rsqrt(jnp.sum(x * x, axis=-1, keepdims=True) + EPS)


def _flip(x):
    return jnp.flip(x, axis=1)


def _axial_rope(x):
    L, dim = x.shape[1], x.shape[-1]
    n_rows = L // GRID_W
    row = jnp.repeat(jnp.arange(n_rows), GRID_W).astype(F32)
    col = jnp.tile(jnp.arange(GRID_W), n_rows).astype(F32)
    nf = dim // 4
    inv = jnp.power(ROPE_BASE, -jnp.arange(nf, dtype=F32) / nf)
    ang_r = row[:, None] * inv
    ang_c = col[:, None] * inv
    ang = jnp.concatenate([ang_r, ang_r, ang_c, ang_c], axis=-1)
    shape = (1, L) + (1,) * (x.ndim - 3) + (dim,)
    cos = jnp.cos(ang).reshape(shape)
    sin = jnp.sin(ang).reshape(shape)
    x_r1, x_r2, x_c1, x_c2 = jnp.split(x, 4, axis=-1)
    rot = jnp.concatenate([-x_r2, x_r1, -x_c2, x_c1], axis=-1)
    return x * cos + rot * sin


def _to_chunks(x):
    B, L, H = x.shape[:3]
    x = x.reshape((B, L // CHUNK, CHUNK, H) + x.shape[3:])
    return jnp.moveaxis(x, 3, 1)


def _from_chunks(x):
    x = jnp.moveaxis(x, 1, 3)
    B, N, C, H = x.shape[:4]
    return x.reshape((B, N * C, H) + x.shape[4:])


def _retention_scan(q, k, v, log_gamma, s0):
    q = _to_chunks(q * R_DK ** -0.5)
    k = _to_chunks(k)
    v = _to_chunks(v)
    pos = jnp.arange(CHUNK, dtype=F32)
    lg = log_gamma[:, None]
    dist = pos[:, None] - pos[None, :]
    intra = jnp.exp(jnp.where(dist >= 0, lg[:, :, None] * dist, -jnp.inf))
    q_dec = jnp.exp(lg * (pos + 1.0))[:, :, None]
    k_dec = jnp.exp(lg * (CHUNK - 1.0 - pos))[:, :, None]
    c_dec = jnp.exp(lg * CHUNK)[:, :, None]

    def step(s, xs):
        qc, kc, vc = xs
        a = jnp.einsum('bhqd,bhkd->bhqk', qc, kc) * intra
        o = jnp.einsum('bhqk,bhkv->bhqv', a, vc) + jnp.einsum('bhqd,bhdv->bhqv', qc, s) * q_dec
        s = s * c_dec + jnp.einsum('bhkd,bhkv->bhdv', kc * k_dec, vc)
        return s, o

    xs = (jnp.moveaxis(q, 2, 0), jnp.moveaxis(k, 2, 0), jnp.moveaxis(v, 2, 0))
    s, o = lax.scan(step, s0, xs)
    return _from_chunks(jnp.moveaxis(o, 0, 2)), s


def _gdn_scan(q, k, v, g, beta, s0):
    q = _to_chunks(q * G_DK ** -0.5)
    k = _to_chunks(k)
    v = _to_chunks(v)
    g = jnp.cumsum(_to_chunks(g), axis=-1)
    beta = _to_chunks(beta)
    incl = jnp.tril(jnp.ones((CHUNK, CHUNK), dtype=bool))
    strict = jnp.tril(jnp.ones((CHUNK, CHUNK), dtype=bool), -1)
    decay = jnp.exp(jnp.where(incl, g[..., :, None] - g[..., None, :], -jnp.inf))
    kb = k * beta[..., None]
    a = jnp.where(strict, jnp.einsum('bhnid,bhnjd->bhnij', kb, k) * decay, 0.0)
    eye = jnp.eye(CHUNK, dtype=F32)
    t = lax.linalg.triangular_solve(a + eye, jnp.broadcast_to(eye, a.shape), left_side=True, lower=True)
    u = t @ (v * beta[..., None])
    w = t @ (kb * jnp.exp(g)[..., None])
    qk = jnp.einsum('bhnid,bhnjd->bhnij', q, k) * decay
    qg = q * jnp.exp(g)[..., None]
    kd = k * jnp.exp(g[..., -1:] - g)[..., None]
    gl = jnp.exp(g[..., -1])[..., None, None]

    def step(s, xs):
        qg_c, kd_c, u_c, w_c, qk_c, gl_c = xs
        u_new = u_c - w_c @ s
        o = qg_c @ s + qk_c @ u_new
        s = s * gl_c + jnp.swapaxes(kd_c, -1, -2) @ u_new
        return s, o

    xs = tuple(jnp.moveaxis(arr, 2, 0) for arr in (qg, kd, u, w, qk, gl))
    s, o = lax.scan(step, s0, xs)
    return _from_chunks(jnp.moveaxis(o, 0, 2)), s


def _diff_attention(q, k, v, lam):
    scale = D_HD ** -0.5
    s = jnp.einsum('bqhpd,bkhpd->bhpqk', q, k) * scale
    p = jax.nn.softmax(s, axis=-1)
    wts = p[:, :, 0] - lam * p[:, :, 1]
    return jnp.einsum('bhqk,bkhe->bqhe', wts, v)


def _centred_dwconv(x, w):
    pad = (w.shape[0] - 1) // 2
    return lax.conv_general_dilated(x, w[:, None, :], window_strides=(1,), padding=[(pad, pad)],
                                    dimension_numbers=('NWC', 'WIO', 'NWC'),
                                    feature_group_count=x.shape[-1])


def _mixers_jax(proj, gab, lw, li, ctx, with_gdn=True):
    B, L, _ = proj.shape
    latent = ctx is not None
    r_q = proj[..., C_RQ:C_RK].reshape(B, L, R_HEADS, R_DK)
    r_k = proj[..., C_RK:C_RV].reshape(B, L, R_HEADS, R_DK)
    r_v = proj[..., C_RV:C_RG].reshape(B, L, R_HEADS, R_DV)
    r_g = proj[..., C_RG:C_DQ]
    if latent:
        r_q, r_k = _axial_rope(r_q), _axial_rope(r_k)
        r0_f, r0_b = ctx['ret'][:, 0], ctx['ret'][:, 1]
    else:
        r0_f = r0_b = jnp.zeros((B, R_HEADS, R_DK, R_DV), F32)
    log_gamma = jax.nn.log_sigmoid(lw['ret_decay'])
    or_f, sr_f = _retention_scan(r_q, r_k, r_v, log_gamma[0], r0_f)
    or_b, sr_b = _retention_scan(_flip(r_q), _flip(r_k), _flip(r_v), log_gamma[1], r0_b)
    o_ret = _rmsnorm(or_f + _flip(or_b), lw['ret_norm']).reshape(B, L, BRANCH_W) * jax.nn.silu(r_g)

    d_q = proj[..., C_DQ:C_DK].reshape(B, L, D_HEADS, 2, D_HD)
    d_k = proj[..., C_DK:C_DV].reshape(B, L, D_HEADS, 2, D_HD)
    d_v = proj[..., C_DV:C_GQKV].reshape(B, L, D_HEADS, 2 * D_HD)
    if latent:
        q_att = _axial_rope(d_q)
        k_all = jnp.concatenate([_axial_rope(d_k), ctx['k']], axis=1)
        v_all = jnp.concatenate([d_v, ctx['v']], axis=1)
    else:
        q_att, k_all, v_all = d_q, d_k, d_v
    lam_init = 0.8 - 0.6 * math.exp(-0.3 * li)
    lp = lw['diff_lambda']
    lam = jnp.exp(jnp.sum(lp[0] * lp[1])) - jnp.exp(jnp.sum(lp[2] * lp[3])) + lam_init
    o_diff = _diff_attention(q_att, k_all, v_all, lam)
    o_diff = (_rmsnorm(o_diff, lw['diff_norm']) * (1.0 - lam_init)).reshape(B, L, BRANCH_W)
    if not with_gdn:
        return o_ret, o_diff, None, (d_k, d_v, jnp.stack([sr_f, sr_b], axis=1), None)

    g_qkv = jax.nn.silu(_centred_dwconv(proj[..., C_GQKV:C_GZ], lw['gdn_conv']))
    g_q, g_k, g_v = jnp.split(g_qkv, [G_HEADS * G_DK, 2 * G_HEADS * G_DK], axis=-1)
    g_q = _l2norm(g_q.reshape(B, L, G_HEADS, G_DK))
    g_k = _l2norm(g_k.reshape(B, L, G_HEADS, G_DK))
    g_v = g_v.reshape(B, L, G_HEADS, G_DV)
    g_z = proj[..., C_GZ:C_MG]
    g_a = gab[..., :16].reshape(B, L, 2, G_HEADS)
    g_b = gab[..., 16:32].reshape(B, L, 2, G_HEADS)
    g_log = -jnp.exp(lw['gdn_A_log']) * jax.nn.softplus(g_a + lw['gdn_dt_bias'])
    g_beta = jax.nn.sigmoid(g_b)
    if latent:
        g0_f, g0_b = ctx['gdn'][:, 0], ctx['gdn'][:, 1]
    else:
        g0_f = g0_b = jnp.zeros((B, G_HEADS, G_DK, G_DV), F32)
    og_f, sg_f = _gdn_scan(g_q, g_k, g_v, g_log[:, :, 0], g_beta[:, :, 0], g0_f)
    og_b, sg_b = _gdn_scan(_flip(g_q), _flip(g_k), _flip(g_v), _flip(g_log[:, :, 1]), _flip(g_beta[:, :, 1]), g0_b)
    o_gdn = _rmsnorm(og_f + _flip(og_b), lw['gdn_norm']) * jax.nn.silu(g_z.reshape(B, L, G_HEADS, G_DV))
    o_gdn = o_gdn.reshape(B, L, BRANCH_W)
    state = (d_k, d_v, jnp.stack([sr_f, sr_b], axis=1), jnp.stack([sg_f, sg_b], axis=1))
    return o_ret, o_diff, o_gdn, state


def _peer_jax(h, w_q, sub_keys, exp_u, exp_v):
    hb = h.reshape(N_TOK // TOK_BLOCK, TOK_BLOCK, D_MODEL)

    def block(hx):
        q = jnp.dot(hx, w_q, preferred_element_type=F32).reshape(TOK_BLOCK, PEER_HEADS, 2, PEER_DQ // 2)
        s = jnp.einsum('thpd,pnd->thpn', q, sub_keys)
        s_top, i_top = lax.top_k(s, PEER_TOPK)
        cand_s = (s_top[:, :, 0, :, None] + s_top[:, :, 1, None, :]).reshape(TOK_BLOCK, PEER_HEADS, -1)
        cand_i = (i_top[:, :, 0, :, None] * N_KEYS + i_top[:, :, 1, None, :]).reshape(TOK_BLOCK, PEER_HEADS, -1)
        best_s, pos = lax.top_k(cand_s, PEER_TOPK)
        idx = jnp.take_along_axis(cand_i, pos, axis=-1)
        gate = jax.nn.softmax(best_s, axis=-1)
        hf = hx.astype(F32)
        act = jax.nn.gelu(jnp.einsum('thkd,td->thk', exp_u[idx], hf), approximate=False)
        return jnp.einsum('thk,thkd->td', gate * act, exp_v[idx])

    return lax.map(block, hb).reshape(N_TOK, D_MODEL)


def _per_block(rows):
    idx = np.concatenate([np.zeros(N_CTX // MOD_BLK, np.int32)] +
                         [np.full(DEC_SEQ // MOD_BLK, 1 + b, np.int32) for b in range(DEC_BATCH)])
    return rows[idx][:, None, :]


def kernel(x_prompt, x_sample, cache_diff_k, cache_diff_v, state_ret, state_gdn, c, c_ctx,
           norm1_w, norm2_w, w_mod, b_mod, w_in, ret_decay, ret_norm, diff_lambda, diff_norm,
           gdn_conv, gdn_A_log, gdn_dt_bias, gdn_norm, w_branch, w_out,
           peer_wq, peer_keys, peer_u, peer_v, norm_f_w):
    x = jnp.concatenate([x_prompt.reshape(N_CTX, D_MODEL), x_sample.reshape(N_LAT, D_MODEL)], axis=0)
    cond = jnp.concatenate([c_ctx[None, :], c, jnp.zeros((8 - N_COND, D_MODEL), F32)], axis=0)
    mod_all = _modulation(cond, w_mod, b_mod)

    w_main = jnp.concatenate([w_in[:, :, :AB_OFF], w_in[:, :, AB_OFF + 32:]], axis=-1).astype(BF16)
    gate_cols = _gdn_gate_cols()
    w_ab = w_in[:, :, AB_OFF:AB_OFF + 32][:, :, gate_cols]
    w_ab = jnp.pad(w_ab, ((0, 0), (0, 0), (0, 0), (0, 128 - gate_cols.shape[1])))
    w_ab = w_ab.reshape(DEPTH, D_MODEL, gate_cols.shape[0] * 128).astype(BF16)
    wb_bf = w_branch.astype(BF16)
    wo_bf = w_out.astype(BF16)
    wq_bf = peer_wq.astype(BF16)

    ks, vs, srs, sgs = [], [], [], []
    for li in range(DEPTH):
        lw = dict(ret_decay=ret_decay[li], ret_norm=ret_norm[li], diff_lambda=diff_lambda[li],
                  diff_norm=diff_norm[li], gdn_conv=gdn_conv[li], gdn_A_log=gdn_A_log[li],
                  gdn_dt_bias=gdn_dt_bias[li], gdn_norm=gdn_norm[li])
        mod = mod_all[li, :N_COND]
        sh1, sc1, g1, sh2, sc2, g2 = [_per_block(m) for m in jnp.split(mod, 6, axis=-1)]
        h = _normmod(x, norm1_w[li], sc1, sh1, BF16)
        proj = _matmul(h, w_main[li], 512, 1024)
        gab = _matmul(h, w_ab[li], 512, 512)

        o_ret, sr_l = _retention(proj, ret_decay[li], ret_norm[li], SEQ, BATCH, 0)
        o_ret, _ = _retention(proj, ret_decay[li], ret_norm[li], DEC_SEQ, DEC_BATCH, N_CTX, s0=state_ret[:, li],
                              dst=o_ret)
        o_diff = _diff_attn(proj, diff_lambda[li], diff_norm[li], li, SEQ, BATCH, 0)
        o_diff = _diff_attn(proj, diff_lambda[li], diff_norm[li], li, DEC_SEQ, DEC_BATCH, N_CTX,
                            cache_diff_k, cache_diff_v, dst=o_diff)
        k_l = proj[:N_CTX, C_DK:C_DV].reshape(BATCH, SEQ, D_HEADS, 2, D_HD)
        v_l = proj[:N_CTX, C_DV:C_GQKV].reshape(BATCH, SEQ, D_HEADS, 2 * D_HD)

        alog = gdn_A_log[li].T.reshape(2 * G_HEADS, 1, 1)
        dtb = gdn_dt_bias[li].T.reshape(2 * G_HEADS, 1, 1)
        qkv_c = _gdn_prep(proj, gdn_conv[li], SEQ, BATCH, 0)
        o_gdn, sg_l = _gdn(qkv_c, proj, gab, alog, dtb, gdn_norm[li], None, SEQ, BATCH, 0)
        qkv_z = _gdn_prep(proj, gdn_conv[li], DEC_SEQ, DEC_BATCH, N_CTX)
        o_gdn, _ = _gdn(qkv_z, proj, gab, alog, dtb, gdn_norm[li], state_gdn[:, li], DEC_SEQ, DEC_BATCH, N_CTX,
                        dst=o_gdn)
        ks.append(k_l); vs.append(v_l); srs.append(sr_l); sgs.append(sg_l)

        x, h2 = _merge(o_ret, o_diff, o_gdn, proj, wb_bf[li], wo_bf[li], x, g1, norm2_w[li], sc2, sh2)
        idx, gate = _peer_route(h2, wq_bf[li], peer_keys[li])
        x = _peer_experts(h2, idx, gate, peer_u[li].astype(BF16), peer_v[li].astype(BF16), x, g2)

    zeros_blk = jnp.zeros((N_MOD_BLK, 1, D_MODEL), F32)
    y = _normmod(x, norm_f_w, zeros_blk, zeros_blk, F32)
    y_prompt = y[:N_CTX].reshape(BATCH, SEQ, D_MODEL)
    y_sample = y[N_CTX:].reshape(DEC_BATCH, DEC_SEQ, D_MODEL)
    return (y_prompt, y_sample, jnp.stack(ks, axis=1), jnp.stack(vs, axis=1),
            jnp.stack(srs, axis=1), jnp.stack(sgs, axis=1))
```

```python
import functools
import math

import numpy as np
import jax
import jax.numpy as jnp
from jax import lax
from jax.experimental import pallas as pl
from jax.experimental.pallas import tpu as pltpu

F32 = jnp.float32
BF16 = jnp.bfloat16

D_MODEL = 1024
BATCH = 16
SEQ = 256
DEPTH = 2
DEC_BATCH = 2
DEC_SEQ = 1024
PAST_LEN = 256
GRID_W = 64
ROPE_BASE = 10000.0
EPS = 1e-6
CHUNK = 64
R_HEADS, R_DK, R_DV = 4, 128, 256
D_HEADS, D_HD = 8, 64
G_HEADS, G_DK, G_DV = 8, 128, 128
CONV_K = 7
G_QKV = G_HEADS * (2 * G_DK + G_DV)
N_BRANCH = 3
BRANCH_W = 1024
PEER_HEADS = 8
PEER_DQ = 256
N_KEYS = 128
PEER_TOPK = 16
TOK_BLOCK = 128

N_CTX = BATCH * SEQ
N_LAT = DEC_BATCH * DEC_SEQ
N_TOK = N_CTX + N_LAT
N_COND = 1 + DEC_BATCH
MOD_BLK = 256
N_MOD_BLK = N_TOK // MOD_BLK

C_RQ, C_RK, C_RV, C_RG = 0, 512, 1024, 2048
C_DQ, C_DK, C_DV = 3072, 4096, 5120
C_GQKV, C_GZ, C_MG = 6144, 9216, 10240
N_MAIN = 13312
AB_OFF = 10240

VMEM_LIMIT = 56 * 1024 * 1024


def _cparams(sem):
    return pltpu.CompilerParams(dimension_semantics=sem, vmem_limit_bytes=VMEM_LIMIT)


def _mod_kernel(c_ref, w_ref, b_ref, o_ref):
    c = c_ref[...]
    a = c * jax.nn.sigmoid(c)
    o_ref[0] = jnp.dot(a, w_ref[0], preferred_element_type=F32,
                       precision=lax.Precision.HIGHEST) + b_ref[0]


def _modulation(cond_pad, w_mod, b_mod):
    tn = 1536
    return pl.pallas_call(
        _mod_kernel,
        out_shape=jax.ShapeDtypeStruct((DEPTH, 8, 6 * D_MODEL), F32),
        grid=(DEPTH, 6 * D_MODEL // tn),
        in_specs=[pl.BlockSpec((8, D_MODEL), lambda l, j: (0, 0)),
                  pl.BlockSpec((1, D_MODEL, tn), lambda l, j: (l, 0, j)),
                  pl.BlockSpec((1, 1, tn), lambda l, j: (l, 0, j))],
        out_specs=pl.BlockSpec((1, 8, tn), lambda l, j: (l, 0, j)),
        compiler_params=_cparams(("arbitrary", "arbitrary")),
        name="modulation",
    )(cond_pad, w_mod, b_mod.reshape(DEPTH, 1, 6 * D_MODEL))


def _normmod_kernel(x_ref, w_ref, sc_ref, sh_ref, o_ref):
    x = x_ref[...]
    y = x * lax.rsqrt(jnp.mean(x * x, axis=-1, keepdims=True) + EPS)
    y = y * w_ref[...]
    o_ref[...] = (y * (1.0 + sc_ref[0]) + sh_ref[0]).astype(o_ref.dtype)


def _normmod(x, w, sc_blk, sh_blk, out_dtype):
    tm = MOD_BLK
    return pl.pallas_call(
        _normmod_kernel,
        out_shape=jax.ShapeDtypeStruct((N_TOK, D_MODEL), out_dtype),
        grid=(N_TOK // tm,),
        in_specs=[pl.BlockSpec((tm, D_MODEL), lambda i: (i, 0)),
                  pl.BlockSpec((1, D_MODEL), lambda i: (0, 0)),
                  pl.BlockSpec((1, 1, D_MODEL), lambda i: (i, 0, 0)),
                  pl.BlockSpec((1, 1, D_MODEL), lambda i: (i, 0, 0))],
        out_specs=pl.BlockSpec((tm, D_MODEL), lambda i: (i, 0)),
        compiler_params=_cparams(("arbitrary",)),
        name="normmod",
    )(x, w.reshape(1, D_MODEL), sc_blk, sh_blk)


def _matmul_kernel(a_ref, b_ref, o_ref):
    o_ref[...] = jnp.dot(a_ref[...], b_ref[...], preferred_element_type=F32)


def _matmul(a, b, li, tm, tn):
    m, k = a.shape
    n = b.shape[2]
    return pl.pallas_call(
        _matmul_kernel,
        out_shape=jax.ShapeDtypeStruct((m, n), F32),
        grid=(n // tn, m // tm),
        in_specs=[pl.BlockSpec((tm, k), lambda j, i: (i, 0)),
                  pl.BlockSpec((None, k, tn), lambda j, i: (li, 0, j))],
        out_specs=pl.BlockSpec((tm, tn), lambda j, i: (i, j)),
        compiler_params=_cparams(("arbitrary", "arbitrary")),
        name="matmul",
    )(a, b)


def _merge_kernel(oa_ref, ob_ref, oc_ref, mga_ref, mgb_ref, mgc_ref, wb_ref, wo_ref, x_ref, g1_ref,
                  n2_ref, sc_ref, sh_ref, xo_ref, h_ref):
    mix = None
    for n, (o_ref, mg_ref) in enumerate(((oa_ref, mga_ref), (ob_ref, mgb_ref), (oc_ref, mgc_ref))):
        merged = jnp.dot(o_ref[...], wb_ref[n], preferred_element_type=F32)
        gate = jax.nn.sigmoid(mg_ref[...])
        mix = gate * merged if mix is None else mix + gate * merged
    m = jnp.dot(mix.astype(BF16), wo_ref[...], preferred_element_type=F32)
    x = x_ref[...] + g1_ref[0] * m
    xo_ref[...] = x
    y = x * lax.rsqrt(jnp.mean(x * x, axis=-1, keepdims=True) + EPS)
    y = y * n2_ref[...]
    h_ref[...] = (y * (1.0 + sc_ref[0]) + sh_ref[0]).astype(h_ref.dtype)


def _merge(o_ret, o_diff, o_gdn, proj, wb, wo, li, x, g1_blk, n2w, sc2_blk, sh2_blk):
    tm = MOD_BLK
    tok = lambda i: (i, 0)
    blk = lambda i: (i, 0, 0)
    return pl.pallas_call(
        _merge_kernel,
        out_shape=(jax.ShapeDtypeStruct((N_TOK, D_MODEL), F32),
                   jax.ShapeDtypeStruct((N_TOK, D_MODEL), BF16)),
        grid=(N_TOK // tm,),
        in_specs=[pl.BlockSpec((tm, BRANCH_W), tok),
                  pl.BlockSpec((tm, BRANCH_W), tok),
                  pl.BlockSpec((tm, BRANCH_W), tok),
                  pl.BlockSpec((tm, D_MODEL), lambda i: (i, C_MG // D_MODEL)),
                  pl.BlockSpec((tm, D_MODEL), lambda i: (i, C_MG // D_MODEL + 1)),
                  pl.BlockSpec((tm, D_MODEL), lambda i: (i, C_MG // D_MODEL + 2)),
                  pl.BlockSpec((None, N_BRANCH, BRANCH_W, D_MODEL), lambda i: (li, 0, 0, 0)),
                  pl.BlockSpec((None, D_MODEL, D_MODEL), lambda i: (li, 0, 0)),
                  pl.BlockSpec((tm, D_MODEL), tok),
                  pl.BlockSpec((1, 1, D_MODEL), blk),
                  pl.BlockSpec((1, D_MODEL), lambda i: (0, 0)),
                  pl.BlockSpec((1, 1, D_MODEL), blk),
                  pl.BlockSpec((1, 1, D_MODEL), blk)],
        out_specs=(pl.BlockSpec((tm, D_MODEL), tok), pl.BlockSpec((tm, D_MODEL), tok)),
        compiler_params=_cparams(("arbitrary",)),
        name="merge",
    )(o_ret, o_diff, o_gdn, proj, proj, proj, wb, wo, x, g1_blk, n2w.reshape(1, D_MODEL), sc2_blk, sh2_blk)


ROUTE_T = 128
ROUTE_HEADS = 4
N_PAIR = PEER_HEADS * PEER_TOPK
N_CAND = PEER_TOPK + (PEER_TOPK // 2) * (PEER_TOPK // 2 - 1) + PEER_TOPK // 2
NEG_INF = float("-inf")


def _split_bf16(x):
    hi = x.astype(BF16)
    return hi, (x - hi.astype(F32)).astype(BF16)


def _topk_rows(scores, rows, n_rows, payloads=None):
    scores = list(scores)
    vals = [[] for _ in scores]
    outs = [[] for _ in scores]
    for _ in range(PEER_TOPK):
        for n, s in enumerate(scores):
            m = jnp.max(s, axis=0, keepdims=True)
            pos = jnp.min(jnp.where(s == m, rows, n_rows), axis=0, keepdims=True)
            sel = rows == pos
            vals[n].append(m)
            if payloads is None:
                outs[n].append(pos)
            else:
                outs[n].append(jnp.sum(jnp.where(sel, payloads[n], 0), axis=0, keepdims=True))
            scores[n] = jnp.where(sel, NEG_INF, s)
    return [(jnp.concatenate(v, axis=0), jnp.concatenate(o, axis=0)) for v, o in zip(vals, outs)]


def _route_kernel(h_ref, wq_ref, khi_ref, klo_ref, idx_ref, gate_ref, q_scr, idx_scr, gate_scr):
    t = h_ref.shape[0]
    q = jnp.dot(h_ref[...], wq_ref[...], preferred_element_type=F32)
    for hp in range(2 * PEER_HEADS):
        q_scr[hp] = q[:, hp * 128:(hp + 1) * 128]
    rows_k = lax.broadcasted_iota(jnp.int32, (N_KEYS, t), 0)
    rows_c = lax.broadcasted_iota(jnp.int32, (N_CAND, t), 0)
    nt = (((1,), (1,)), ((), ()))

    half = PEER_TOPK // 2
    groups = [(slice(0, 1), slice(0, PEER_TOPK)), (slice(1, 2), slice(0, half))]
    groups += [(slice(k, k + 1), slice(0, half)) for k in range(2, half)]
    groups += [(slice(half, PEER_TOPK), slice(0, 1))]

    def heads(hh, carry):
        hs = [ROUTE_HEADS * hh + n for n in range(ROUTE_HEADS)]
        scores = []
        for h in hs:
            for p in range(2):
                qhi, qlo = _split_bf16(q_scr[2 * h + p])
                scores.append(lax.dot_general(khi_ref[p], qhi, nt, preferred_element_type=F32)
                              + lax.dot_general(khi_ref[p], qlo, nt, preferred_element_type=F32)
                              + lax.dot_general(klo_ref[p], qhi, nt, preferred_element_type=F32))
        tops = _topk_rows(scores, rows_k, N_KEYS)
        cands, cand_is = [], []
        for n in range(ROUTE_HEADS):
            (v1, i1), (v2, i2) = tops[2 * n], tops[2 * n + 1]
            cands.append(jnp.concatenate([v1[a] + v2[b] for a, b in groups], axis=0))
            cand_is.append(jnp.concatenate([i1[a] * N_KEYS + i2[b] for a, b in groups], axis=0))
        for h, (best, bidx) in zip(hs, _topk_rows(cands, rows_c, N_CAND, payloads=cand_is)):
            e = jnp.exp(best - best[0:1])
            gate_scr[h] = e / jnp.sum(e, axis=0, keepdims=True)
            idx_scr[h] = bidx
        return carry

    lax.fori_loop(0, PEER_HEADS // ROUTE_HEADS, heads, 0)
    idx_ref[...] = idx_scr[...].reshape(N_PAIR, t).T
    gate_ref[...] = gate_scr[...].reshape(N_PAIR, t).T


def _peer_route(h2, wq, li, keys):
    khi, klo = _split_bf16(keys)
    t = ROUTE_T
    n_tok = h2.shape[0]
    return pl.pallas_call(
        _route_kernel,
        out_shape=(jax.ShapeDtypeStruct((n_tok, N_PAIR), jnp.int32),
                   jax.ShapeDtypeStruct((n_tok, N_PAIR), F32)),
        grid=(n_tok // t,),
        in_specs=[pl.BlockSpec((t, D_MODEL), lambda i: (i, 0)),
                  pl.BlockSpec((None, D_MODEL, PEER_HEADS * PEER_DQ), lambda i: (li, 0, 0)),
                  pl.BlockSpec((2, N_KEYS, PEER_DQ // 2), lambda i: (0, 0, 0)),
                  pl.BlockSpec((2, N_KEYS, PEER_DQ // 2), lambda i: (0, 0, 0))],
        out_specs=(pl.BlockSpec((t, N_PAIR), lambda i: (i, 0)),
                   pl.BlockSpec((t, N_PAIR), lambda i: (i, 0))),
        scratch_shapes=[pltpu.VMEM((2 * PEER_HEADS, t, PEER_DQ // 2), F32),
                        pltpu.VMEM((PEER_HEADS, PEER_TOPK, t), jnp.int32),
                        pltpu.VMEM((PEER_HEADS, PEER_TOPK, t), F32)],
        compiler_params=_cparams(("arbitrary",)),
        name="peer_route",
    )(h2, wq, khi, klo)


EXP_TB = 256
EXP_EB = 2048
EXP_SUB = 256
EXP_AHEAD = 2
N_EXPERTS = N_KEYS * N_KEYS


def _expert_kernel(h_ref, idx_ref, gate_ref, u_ref, v_ref, x_ref, g2_ref, o_ref, g_scr, acc_ref):
    j = pl.program_id(1)
    tb = h_ref.shape[0]

    @pl.when(j == 0)
    def _():
        acc_ref[...] = jnp.zeros_like(acc_ref)
        sub = lax.broadcasted_iota(jnp.int32, (N_KEYS, N_PAIR), 0)

        def body(t, carry):
            e = idx_ref[pl.ds(t, 1), :]
            g = gate_ref[pl.ds(t, 1), :]
            xa = jnp.where(sub == (e >> 7), 1.0, 0.0).astype(BF16)
            yb = jnp.where(sub == (e & (N_KEYS - 1)), g, 0.0).astype(BF16)
            g_scr[pl.ds(pl.multiple_of(t * N_KEYS, N_KEYS), N_KEYS), :] = _dot_nt(xa, yb)
            return carry

        lax.fori_loop(0, tb, body, 0, unroll=16)

    h = h_ref[...]
    nsub = EXP_EB // EXP_SUB
    per = EXP_SUB // N_KEYS
    score = lambda c: _dot_nt(h, u_ref[c * EXP_SUB:(c + 1) * EXP_SUB, :])
    s = [score(c) for c in range(EXP_AHEAD)]
    acc = acc_ref[...]
    for c in range(nsub):
        gj = jnp.concatenate([g_scr[pl.ds(j * (nsub * per) + c * per + i, tb, stride=N_KEYS), :]
                              for i in range(per)], axis=1)
        w = 0.5 * s[c] * (1.0 + lax.erf(s[c] * (2.0 ** -0.5))) * gj
        acc = acc + jnp.dot(w.astype(BF16), v_ref[c * EXP_SUB:(c + 1) * EXP_SUB, :], preferred_element_type=F32)
        if c + EXP_AHEAD < nsub:
            s.append(score(c + EXP_AHEAD))
    acc_ref[...] = acc

    @pl.when(j == pl.num_programs(1) - 1)
    def _():
        o_ref[...] = x_ref[...] + g2_ref[0] * acc_ref[...]


def _peer_experts(h2, idx, gate, u, v, li, x, g2_blk):
    tb, eb = EXP_TB, EXP_EB
    n_tok = h2.shape[0]
    return pl.pallas_call(
        _expert_kernel,
        out_shape=jax.ShapeDtypeStruct((n_tok, D_MODEL), F32),
        grid=(n_tok // tb, N_EXPERTS // eb),
        in_specs=[pl.BlockSpec((tb, D_MODEL), lambda i, j: (i, 0)),
                  pl.BlockSpec((tb, N_PAIR), lambda i, j: (i, 0)),
                  pl.BlockSpec((tb, N_PAIR), lambda i, j: (i, 0)),
                  pl.BlockSpec((None, eb, D_MODEL), lambda i, j: (li, j, 0)),
                  pl.BlockSpec((None, eb, D_MODEL), lambda i, j: (li, j, 0)),
                  pl.BlockSpec((tb, D_MODEL), lambda i, j: (i, 0)),
                  pl.BlockSpec((1, 1, D_MODEL), lambda i, j: (i * tb // MOD_BLK, 0, 0))],
        out_specs=pl.BlockSpec((tb, D_MODEL), lambda i, j: (i, 0)),
        scratch_shapes=[pltpu.VMEM((tb * N_KEYS, N_KEYS), F32),
                        pltpu.VMEM((tb, D_MODEL), F32)],
        compiler_params=_cparams(("arbitrary", "arbitrary")),
        name="peer_experts",
    )(h2, idx, gate, u, v, x, g2_blk)


def _drop_refs(fn, pos, n, *refs):
    return fn(*refs[:pos], *refs[pos + n:])


def _branch_call(kernel_fn, dsts, in_specs, args, out_shape, out_specs, **kw):
    given = [(n, d) for n, d in enumerate(dsts) if d is not None]
    if given:
        pos = len(in_specs)
        in_specs = list(in_specs) + [pl.BlockSpec(memory_space=pl.ANY)] * len(given)
        args = list(args) + [d for _, d in given]
        kernel_fn = functools.partial(_drop_refs, kernel_fn, pos, len(given))
        kw["input_output_aliases"] = {pos + m: n for m, (n, _) in enumerate(given)}
    return pl.pallas_call(kernel_fn, out_shape=tuple(out_shape), in_specs=in_specs, out_specs=tuple(out_specs),
                          **kw)(*args)


def _kv_out_kernel(k_ref, v_ref, ko_ref, vo_ref):
    ko_ref[0] = k_ref[...]
    vo_ref[0] = v_ref[...]


def _kv_out(proj, li, dst_k, dst_v):
    shape = jax.ShapeDtypeStruct((BATCH, DEPTH, SEQ, D_HEADS * 2 * D_HD), F32)
    blk = D_HEADS * 2 * D_HD
    return _branch_call(
        _kv_out_kernel, [dst_k, dst_v],
        [pl.BlockSpec((SEQ, blk), lambda s: (s, C_DK // blk)), pl.BlockSpec((SEQ, blk), lambda s: (s, C_DV // blk))],
        [proj, proj], [shape, shape],
        [pl.BlockSpec((1, None, SEQ, blk), lambda s: (s, li, 0, 0))] * 2,
        grid=(BATCH,),
        compiler_params=_cparams(("arbitrary",)),
        name="kv_out",
    )


CONV_ROWS = 256
CONV_HALO = 8


def _gdn_prep_kernel(x_ref, w_ref, o_ref):
    sec = pl.program_id(1)
    L = x_ref.shape[0]
    half = CONV_K // 2
    for r0 in range(0, L, CONV_ROWS):
        s0, s1 = max(r0 - CONV_HALO, 0), min(r0 + CONV_ROWS + CONV_HALO, L)
        n = s1 - s0
        t_idx = lax.broadcasted_iota(jnp.int32, (n, 128), 0) + s0
        for cb in range(G_HEADS):
            cols = slice(cb * 128, (cb + 1) * 128)
            x = x_ref[s0:s1, cols]
            w = w_ref[0, :, cols]
            acc = x * w[half:half + 1]
            for d in range(-half, half + 1):
                if d == 0:
                    continue
                xs = pltpu.roll(x, (-d) % n, axis=0)
                valid = (t_idx + d >= 0) if d < 0 else (t_idx + d < L)
                acc = acc + jnp.where(valid, xs, 0.0) * w[half + d:half + d + 1]
            y = acc[r0 - s0:r0 - s0 + CONV_ROWS]
            y = y * jax.nn.sigmoid(y)
            inv = lax.rsqrt(jnp.sum(y * y, axis=-1, keepdims=True) + EPS)
            o_ref[r0:r0 + CONV_ROWS, cols] = y * jnp.where(sec < 2, inv, 1.0)


def _gdn_prep(proj, conv_w, seq_len, n_seq, tok_off):
    blk0 = tok_off // seq_len
    sec0 = C_GQKV // 1024
    return pl.pallas_call(
        _gdn_prep_kernel,
        out_shape=jax.ShapeDtypeStruct((n_seq * seq_len, G_QKV), F32),
        grid=(n_seq, 3),
        in_specs=[pl.BlockSpec((seq_len, 1024), lambda s, c: (blk0 + s, sec0 + c)),
                  pl.BlockSpec((1, CONV_K, 1024), lambda s, c: (0, 0, c))],
        out_specs=pl.BlockSpec((seq_len, 1024), lambda s, c: (s, c)),
        compiler_params=_cparams(("arbitrary", "arbitrary")),
        name="gdn_prep",
    )(proj, conv_w.reshape(1, CONV_K, G_QKV))


GDN_HB = 2


def _dot(a, b):
    return jnp.dot(a, b, preferred_element_type=F32)


def _dot_nt(a, b):
    return lax.dot_general(a, b, (((1,), (1,)), ((), ())), preferred_element_type=F32)


def _dot_tn(a, b):
    return lax.dot_general(a, b, (((0,), (0,)), ((), ())), preferred_element_type=F32)


def _mask_dot3(m, b):
    m16 = m.astype(BF16)
    bh = b.astype(BF16)
    r1 = b - bh.astype(F32)
    bm = r1.astype(BF16)
    bl = (r1 - bm.astype(F32)).astype(BF16)
    return _dot(m16, bh) + _dot(m16, bm) + _dot(m16, bl)


def _gdn_kernel(*refs, seq_len, has_init, emit_state):
    it = iter(refs)
    q_ref, k_ref, v_ref, z_ref, gab_ref, alog_ref, dtb_ref, nw_ref = [next(it) for _ in range(8)]
    s0_ref = next(it) if has_init else None
    o_ref = next(it)
    so_ref = next(it) if emit_state else None
    u_s, w_s, qg_s, kd_s, qk_s, gl_s, o_s = [next(it) for _ in range(7)]
    C = CHUNK
    nc = seq_len // C
    ri = lax.broadcasted_iota(jnp.int32, (C, C), 0)
    ci = lax.broadcasted_iota(jnp.int32, (C, C), 1)
    eye = jnp.where(ri == ci, 1.0, 0.0)
    ones = jnp.ones((C, C), F32)
    scale = G_DK ** -0.5
    incl = (ri >= ci, ri <= ci)
    strict = (ri > ci, ri < ci)
    levels = []
    for lv in range(6):
        b = 1 << lv
        same = (ri >> (lv + 1)) == (ci >> (lv + 1))
        r_hi, c_hi = (ri & b) != 0, (ci & b) != 0
        levels.append((same & r_hi & jnp.logical_not(c_hi), same & jnp.logical_not(r_hi) & c_hi))

    GROUP = 4
    HB = GDN_HB
    lanes = [slice(hh * 128, (hh + 1) * 128) for hh in range(HB)]
    tri = [jnp.where(m, 1.0, 0.0) for m in incl]

    def prep(grp, carry):
        chunks = [grp * GROUP + j for j in range(GROUP)]
        rows = [pl.ds(pl.multiple_of(c * C, C), C) for c in chunks]
        hj = [(hh, j) for hh in range(HB) for j in range(GROUP)]
        q = {(hh, j): q_ref[rows[j], lanes[hh]] * scale for hh, j in hj}
        k = {(hh, j): k_ref[rows[j], lanes[hh]] for hh, j in hj}
        v = {(hh, j): v_ref[rows[j], lanes[hh]] for hh, j in hj}
        k16 = {key: x.astype(BF16) for key, x in k.items()}
        kk = {key: _dot_nt(x, x) for key, x in k16.items()}
        qk = {key: _dot_nt(q[key].astype(BF16), k16[key]) for key in hj}
        probs = [(hh, j, d) for hh in range(HB) for j in range(GROUP) for d in range(2)]
        gab = [gab_ref[r, :] for r in rows]
        ga = lambda hh, j, d: gab[j][:, 2 * hh + d:2 * hh + d + 1]
        gb = lambda hh, j, d: gab[j][:, 2 * HB + 2 * hh + d:2 * HB + 2 * hh + d + 1]
        g = [-jnp.exp(alog_ref[2 * hh + d]) * jax.nn.softplus(ga(hh, j, d) + dtb_ref[2 * hh + d])
             for hh, j, d in probs]
        beta = [jax.nn.sigmoid(gb(hh, j, d)) for hh, j, d in probs]
        gcol = [_mask_dot3(tri[d], jnp.broadcast_to(g[p], (C, 128))) for p, (hh, j, d) in enumerate(probs)]
        grow = [_mask_dot3(ones, jnp.where(incl[1 - d], jnp.broadcast_to(g[p], (C, C)), 0.0))
                for p, (hh, j, d) in enumerate(probs)]
        dec = [jnp.exp(jnp.where(incl[d], gcol[p][:, :C] - grow[p], NEG_INF)) for p, (hh, j, d) in enumerate(probs)]
        a = [jnp.where(strict[d], kk[hh, j] * beta[p] * dec[p], 0.0) for p, (hh, j, d) in enumerate(probs)]
        t = [eye - jnp.where(levels[0][d], a[p], 0.0) for p, (hh, j, d) in enumerate(probs)]
        for lv in range(1, 6):
            t16 = [x.astype(BF16) for x in t]
            ct = [_dot(jnp.where(levels[lv][d], a[p], 0.0).astype(BF16), t16[p])
                  for p, (hh, j, d) in enumerate(probs)]
            t = [t[p] - _dot(t16[p], ct[p].astype(BF16)) for p in range(len(probs))]
        for p, (hh, j, d) in enumerate(probs):
            n = 2 * hh + d
            eg = jnp.exp(gcol[p])
            t16 = t[p].astype(BF16)
            glast = gcol[p][C - 1:C, :] if d == 0 else gcol[p][0:1, :]
            u_s[n, rows[j], :] = _dot(t16, (v[hh, j] * beta[p]).astype(BF16))
            w_s[n, rows[j], :] = _dot(t16, (k[hh, j] * beta[p] * eg).astype(BF16))
            qg_s[n, rows[j], :] = q[hh, j] * eg
            kd_s[n, rows[j], :] = k[hh, j] * jnp.exp(glast - gcol[p])
            qk_s[n, rows[j], :] = qk[hh, j] * dec[p]
            gl_s[n * nc + chunks[j]] = jnp.broadcast_to(jnp.exp(glast), (8, 128))
        return carry

    lax.fori_loop(0, nc // GROUP, prep, 0)
    o_s[...] = jnp.zeros_like(o_s)
    chains = [(hh, d) for hh in range(HB) for d in range(2)]

    def step(i, states):
        cs = (i, nc - 1 - i)
        rows = [pl.ds(pl.multiple_of(c * C, C), C) for c in cs]
        s16 = [s.astype(BF16) for s in states]
        ws = [_dot(w_s[n, rows[d], :].astype(BF16), s16[n]) for n, (hh, d) in enumerate(chains)]
        un16 = [(u_s[n, rows[d], :] - ws[n]).astype(BF16) for n, (hh, d) in enumerate(chains)]
        o = [_dot(qg_s[n, rows[d], :].astype(BF16), s16[n]) + _dot(qk_s[n, rows[d], :].astype(BF16), un16[n])
             for n, (hh, d) in enumerate(chains)]
        new = tuple(states[n] * gl_s[n * nc + cs[d]][0:1, :] + _dot_tn(kd_s[n, rows[d], :].astype(BF16), un16[n])
                    for n, (hh, d) in enumerate(chains))
        for n, (hh, d) in enumerate(chains):
            o_s[rows[d], lanes[hh]] += o[n]
        return new

    if has_init:
        init = tuple(s0_ref[0, d, hh] for hh, d in chains)
    else:
        init = (jnp.zeros((G_DK, G_DV), F32),) * len(chains)
    final = lax.fori_loop(0, nc, step, init)
    if emit_state:
        for n, (hh, d) in enumerate(chains):
            so_ref[0, d, hh] = final[n]
    for hh in range(HB):
        o = o_s[:, lanes[hh]]
        y = o * lax.rsqrt(jnp.mean(o * o, axis=-1, keepdims=True) + EPS) * nw_ref[...]
        z = z_ref[:, lanes[hh]]
        o_ref[:, lanes[hh]] = (y * (z * jax.nn.sigmoid(z))).astype(o_ref.dtype)


def _gdn_gate_cols():
    cols = []
    for blk in range(G_HEADS // GDN_HB):
        heads = range(blk * GDN_HB, (blk + 1) * GDN_HB)
        cols.append([kind * 2 * G_HEADS + d * G_HEADS + h for kind in range(2) for h in heads for d in range(2)])
    return np.asarray(cols, np.int32)


def _gdn(qkv, proj, gab, alog, dtb, nw, s0, seq_len, n_seq, tok_off, li, dst=None, dst_state=None):
    has_init = s0 is not None
    emit_state = not has_init
    blk0 = tok_off // seq_len
    hb = GDN_HB
    wl = hb * 128
    nblk = G_HEADS // hb
    tokcol = lambda off: pl.BlockSpec((seq_len, wl), lambda s, h: (s, off + h))
    in_specs = [tokcol(0), tokcol(nblk), tokcol(2 * nblk),
                pl.BlockSpec((seq_len, wl), lambda s, h: (blk0 + s, C_GZ // wl + h)),
                pl.BlockSpec((seq_len, 128), lambda s, h: (blk0 + s, h)),
                pl.BlockSpec((2 * hb, 1, 1), lambda s, h: (h, 0, 0)),
                pl.BlockSpec((2 * hb, 1, 1), lambda s, h: (h, 0, 0)),
                pl.BlockSpec((1, G_DV), lambda s, h: (0, 0))]
    args = [qkv, qkv, qkv, proj, gab, alog, dtb, nw.reshape(1, G_DV)]
    if has_init:
        in_specs.append(pl.BlockSpec((1, 2, hb, G_DK, G_DV), lambda s, h: (s, 0, h, 0, 0)))
        args.append(s0)
    out_shape = [jax.ShapeDtypeStruct((N_TOK, BRANCH_W), BF16)]
    out_specs = [pl.BlockSpec((seq_len, wl), lambda s, h: (blk0 + s, h))]
    if emit_state:
        out_shape.append(jax.ShapeDtypeStruct((n_seq, DEPTH, 2, G_HEADS, G_DK, G_DV), F32))
        out_specs.append(pl.BlockSpec((1, None, 2, hb, G_DK, G_DV), lambda s, h: (s, li, 0, h, 0, 0)))
    nc = seq_len // CHUNK
    res = _branch_call(
        functools.partial(_gdn_kernel, seq_len=seq_len, has_init=has_init, emit_state=emit_state),
        [dst, dst_state][:len(out_shape)], in_specs, args, out_shape, out_specs,
        grid=(n_seq, nblk),
        scratch_shapes=[pltpu.VMEM((2 * hb, seq_len, 128), F32)] * 4
                       + [pltpu.VMEM((2 * hb, seq_len, CHUNK), F32), pltpu.VMEM((2 * hb * nc, 8, 128), F32),
                          pltpu.VMEM((seq_len, wl), F32)],
        compiler_params=_cparams(("arbitrary", "arbitrary")),
        name="gdn",
    )
    return res if emit_state else (res[0], None)


def _rope_tables(seq_len, dim):
    n_rows = seq_len // GRID_W
    row = np.repeat(np.arange(n_rows), GRID_W).astype(np.float32)
    col = np.tile(np.arange(GRID_W), n_rows).astype(np.float32)
    nf = dim // 4
    inv = np.power(np.float32(ROPE_BASE), -np.arange(nf, dtype=np.float32) / np.float32(nf)).astype(np.float32)
    ang_r = row[:, None] * inv
    ang_c = col[:, None] * inv
    ang = np.concatenate([ang_r, ang_r, ang_c, ang_c], axis=-1)
    ang = np.tile(ang, (1, 128 // dim))
    return jnp.asarray(np.cos(ang), F32), jnp.asarray(np.sin(ang), F32)


def _rope(x, cos, sin, dim):
    quarter = dim // 4
    lane = lax.broadcasted_iota(jnp.int32, x.shape, 1)
    first = (lane & (2 * quarter - 1)) < quarter
    rot = jnp.where(first, -pltpu.roll(x, 128 - quarter, axis=1), pltpu.roll(x, quarter, axis=1))
    return x * cos + rot * sin


Q_TILE = 256


def _retention_kernel(*refs, seq_len, latent):
    it = iter(refs)
    q_ref, k_ref, v_ref, rg_ref, lg_ref, nw_ref = [next(it) for _ in range(6)]
    if latent:
        cos_ref, sin_ref, s0_ref = next(it), next(it), next(it)
    o_ref = next(it)
    so_ref = None if latent else next(it)
    L = seq_len
    lg = lg_ref[0]
    lg = jnp.minimum(lg, 0.0) - jnp.log(1.0 + jnp.exp(-jnp.abs(lg)))
    lgf, lgb = lg[0:1, :], lg[1:2, :]
    k = k_ref[...]
    if latent:
        k = _rope(k, cos_ref[...], sin_ref[...], R_DK)
    k16 = k.astype(BF16)
    v16 = v_ref[...].astype(BF16)
    for qt in range(L // Q_TILE):
        rows = slice(qt * Q_TILE, (qt + 1) * Q_TILE)
        q = q_ref[rows, :]
        if latent:
            q = _rope(q, cos_ref[rows, :], sin_ref[rows, :], R_DK)
        q16 = (q * R_DK ** -0.5).astype(BF16)
        a = _dot_nt(q16, k16)
        i = lax.broadcasted_iota(jnp.int32, (Q_TILE, L), 0) + qt * Q_TILE
        j = lax.broadcasted_iota(jnp.int32, (Q_TILE, L), 1)
        dist = (i - j).astype(F32)
        dec = (jnp.exp(jnp.where(dist >= 0, lgf * dist, NEG_INF))
               + jnp.exp(jnp.where(dist <= 0, -lgb * dist, NEG_INF)))
        o = _dot((a * dec).astype(BF16), v16)
        if latent:
            pos = (lax.broadcasted_iota(jnp.int32, (Q_TILE, 1), 0) + qt * Q_TILE).astype(F32)
            o = o + _dot(q16, s0_ref[0, 0, 0].astype(BF16)) * jnp.exp(lgf * (pos + 1.0))
            o = o + _dot(q16, s0_ref[0, 1, 0].astype(BF16)) * jnp.exp(lgb * (L - pos))
        y = o * lax.rsqrt(jnp.mean(o * o, axis=-1, keepdims=True) + EPS) * nw_ref[...]
        g = rg_ref[rows, :]
        o_ref[rows, :] = (y * (g * jax.nn.sigmoid(g))).astype(o_ref.dtype)
    if not latent:
        pos = lax.broadcasted_iota(jnp.int32, (L, 1), 0).astype(F32)
        so_ref[0, 0, 0] = _dot_tn((k * jnp.exp(lgf * (L - 1.0 - pos))).astype(BF16), v16)
        so_ref[0, 1, 0] = _dot_tn((k * jnp.exp(lgb * pos)).astype(BF16), v16)


def _retention(proj, decay, nw, seq_len, n_seq, tok_off, li, s0=None, dst=None, dst_state=None):
    latent = s0 is not None
    blk0 = tok_off // seq_len
    in_specs = [pl.BlockSpec((seq_len, R_DK), lambda s, h: (blk0 + s, C_RQ // R_DK + h)),
                pl.BlockSpec((seq_len, R_DK), lambda s, h: (blk0 + s, C_RK // R_DK + h)),
                pl.BlockSpec((seq_len, R_DV), lambda s, h: (blk0 + s, C_RV // R_DV + h)),
                pl.BlockSpec((seq_len, R_DV), lambda s, h: (blk0 + s, C_RG // R_DV + h)),
                pl.BlockSpec((1, 2, 1), lambda s, h: (h, 0, 0)),
                pl.BlockSpec((1, R_DV), lambda s, h: (0, 0))]
    args = [proj, proj, proj, proj, decay.T.reshape(R_HEADS, 2, 1), nw.reshape(1, R_DV)]
    out_shape = [jax.ShapeDtypeStruct((N_TOK, BRANCH_W), BF16)]
    out_specs = [pl.BlockSpec((seq_len, R_DV), lambda s, h: (blk0 + s, h))]
    if latent:
        cos, sin = _rope_tables(seq_len, R_DK)
        in_specs += [pl.BlockSpec((seq_len, 128), lambda s, h: (0, 0)),
                     pl.BlockSpec((seq_len, 128), lambda s, h: (0, 0)),
                     pl.BlockSpec((1, 2, 1, R_DK, R_DV), lambda s, h: (s, 0, h, 0, 0))]
        args += [cos, sin, s0]
    else:
        out_shape.append(jax.ShapeDtypeStruct((n_seq, DEPTH, 2, R_HEADS, R_DK, R_DV), F32))
        out_specs.append(pl.BlockSpec((1, None, 2, 1, R_DK, R_DV), lambda s, h: (s, li, 0, h, 0, 0)))
    res = _branch_call(
        functools.partial(_retention_kernel, seq_len=seq_len, latent=latent),
        [dst, dst_state][:len(out_shape)], in_specs, args, out_shape, out_specs,
        grid=(n_seq, R_HEADS),
        compiler_params=_cparams(("arbitrary", "arbitrary")),
        name="retention",
    )
    return (res[0], None) if latent else res


DA_HB = 4


def _diff_attn_kernel(*refs, seq_len, latent, lam_init):
    it = iter(refs)
    q_ref, k_ref, v_ref, lp_ref, nw_ref = [next(it) for _ in range(5)]
    if latent:
        cos_ref, sin_ref, ck_ref, cv_ref = [next(it) for _ in range(4)]
    o_ref = next(it)
    L = seq_len
    lp = lp_ref[...]
    lam = (jnp.exp(jnp.sum(lp[0:1] * lp[1:2], axis=-1, keepdims=True))
           - jnp.exp(jnp.sum(lp[2:3] * lp[3:4], axis=-1, keepdims=True)) + lam_init)
    w = 2 * D_HD
    lanes = [slice(hh * w, (hh + 1) * w) for hh in range(DA_HB)]
    keys, vals = [], []
    for hh in range(DA_HB):
        k = k_ref[:, lanes[hh]]
        if latent:
            k = _rope(k, cos_ref[...], sin_ref[...], D_HD)
        keys.append([k.astype(BF16)] + ([ck_ref[0, 0, :, lanes[hh]].astype(BF16)] if latent else []))
        vals.append([v_ref[:, lanes[hh]].astype(BF16)] + ([cv_ref[0, 0, :, lanes[hh]].astype(BF16)] if latent else []))
    lane = lax.broadcasted_iota(jnp.int32, (Q_TILE, w), 1)
    for qt in range(L // Q_TILE):
        rows = slice(qt * Q_TILE, (qt + 1) * Q_TILE)
        hp = [(hh, p) for hh in range(DA_HB) for p in range(2)]
        qs = []
        for hh in range(DA_HB):
            q = q_ref[rows, lanes[hh]]
            if latent:
                q = _rope(q, cos_ref[rows, :], sin_ref[rows, :], D_HD)
            qs.append(q * D_HD ** -0.5)
        s = [[_dot_nt(jnp.where((lane >= D_HD) == (p == 1), qs[hh], 0.0).astype(BF16), kk) for kk in keys[hh]]
             for hh, p in hp]
        outs = []
        for n, (hh, p) in enumerate(hp):
            m = s[n][0].max(axis=-1, keepdims=True)
            for x in s[n][1:]:
                m = jnp.maximum(m, x.max(axis=-1, keepdims=True))
            e = [jnp.exp(x - m) for x in s[n]]
            z = sum(x.sum(axis=-1, keepdims=True) for x in e)
            pv = sum(_dot(x.astype(BF16), vv) for x, vv in zip(e, vals[hh]))
            outs.append(pv / z)
        for hh in range(DA_HB):
            o = outs[2 * hh] - lam * outs[2 * hh + 1]
            y = o * lax.rsqrt(jnp.mean(o * o, axis=-1, keepdims=True) + EPS) * nw_ref[...]
            o_ref[rows, lanes[hh]] = (y * (1.0 - lam_init)).astype(o_ref.dtype)


def _diff_attn(proj, lam_p, nw, li, seq_len, n_seq, tok_off, cache_k=None, cache_v=None, dst=None):
    latent = cache_k is not None
    blk0 = tok_off // seq_len
    w = 2 * D_HD
    wl = DA_HB * w
    in_specs = [pl.BlockSpec((seq_len, wl), lambda s, h: (blk0 + s, C_DQ // wl + h)),
                pl.BlockSpec((seq_len, wl), lambda s, h: (blk0 + s, C_DK // wl + h)),
                pl.BlockSpec((seq_len, wl), lambda s, h: (blk0 + s, C_DV // wl + h)),
                pl.BlockSpec((4, D_HD), lambda s, h: (0, 0)),
                pl.BlockSpec((1, w), lambda s, h: (0, 0))]
    args = [proj, proj, proj, lam_p, nw.reshape(1, w)]
    if latent:
        cos, sin = _rope_tables(seq_len, D_HD)
        in_specs += [pl.BlockSpec((seq_len, 128), lambda s, h: (0, 0)),
                     pl.BlockSpec((seq_len, 128), lambda s, h: (0, 0)),
                     pl.BlockSpec((1, 1, PAST_LEN, wl), lambda s, h: (s, li, 0, h)),
                     pl.BlockSpec((1, 1, PAST_LEN, wl), lambda s, h: (s, li, 0, h))]
        args += [cos, sin, cache_k.reshape(DEC_BATCH, DEPTH, PAST_LEN, D_HEADS * w),
                 cache_v.reshape(DEC_BATCH, DEPTH, PAST_LEN, D_HEADS * w)]
    lam_init = 0.8 - 0.6 * math.exp(-0.3 * li)
    return _branch_call(
        functools.partial(_diff_attn_kernel, seq_len=seq_len, latent=latent, lam_init=lam_init),
        [dst], in_specs, args,
        [jax.ShapeDtypeStruct((N_TOK, BRANCH_W), BF16)],
        [pl.BlockSpec((seq_len, wl), lambda s, h: (blk0 + s, h))],
        grid=(n_seq, D_HEADS // DA_HB),
        compiler_params=_cparams(("arbitrary", "arbitrary")),
        name="diff_attn",
    )[0]


def _per_block(rows):
    idx = np.concatenate([np.zeros(N_CTX // MOD_BLK, np.int32)] +
                         [np.full(DEC_SEQ // MOD_BLK, 1 + b, np.int32) for b in range(DEC_BATCH)])
    return rows[idx][:, None, :]


def kernel(x_prompt, x_sample, cache_diff_k, cache_diff_v, state_ret, state_gdn, c, c_ctx,
           norm1_w, norm2_w, w_mod, b_mod, w_in, ret_decay, ret_norm, diff_lambda, diff_norm,
           gdn_conv, gdn_A_log, gdn_dt_bias, gdn_norm, w_branch, w_out,
           peer_wq, peer_keys, peer_u, peer_v, norm_f_w):
    x = jnp.concatenate([x_prompt.reshape(N_CTX, D_MODEL), x_sample.reshape(N_LAT, D_MODEL)], axis=0)
    cond = jnp.concatenate([c_ctx[None, :], c, jnp.zeros((8 - N_COND, D_MODEL), F32)], axis=0)
    mod_all = _modulation(cond, w_mod, b_mod)

    w_main = jnp.concatenate([w_in[:, :, :AB_OFF], w_in[:, :, AB_OFF + 32:]], axis=-1).astype(BF16)
    gate_cols = _gdn_gate_cols()
    w_ab = w_in[:, :, AB_OFF:AB_OFF + 32][:, :, gate_cols]
    w_ab = jnp.pad(w_ab, ((0, 0), (0, 0), (0, 0), (0, 128 - gate_cols.shape[1])))
    w_ab = w_ab.reshape(DEPTH, D_MODEL, gate_cols.shape[0] * 128).astype(BF16)
    wb_bf = w_branch.astype(BF16)
    wo_bf = w_out.astype(BF16)
    wq_bf = peer_wq.astype(BF16)
    u_bf = peer_u.astype(BF16)
    v_bf = peer_v.astype(BF16)

    new_k = new_v = new_ret = new_gdn = None
    for li in range(DEPTH):
        mod = mod_all[li, :N_COND]
        sh1, sc1, g1, sh2, sc2, g2 = [_per_block(m) for m in jnp.split(mod, 6, axis=-1)]
        h = _normmod(x, norm1_w[li], sc1, sh1, BF16)
        proj = _matmul(h, w_main, li, 512, 1024)
        gab = _matmul(h, w_ab, li, 512, 512)

        o_ret, new_ret = _retention(proj, ret_decay[li], ret_norm[li], SEQ, BATCH, 0, li, dst_state=new_ret)
        o_ret, _ = _retention(proj, ret_decay[li], ret_norm[li], DEC_SEQ, DEC_BATCH, N_CTX, li,
                              s0=state_ret[:, li], dst=o_ret)
        o_diff = _diff_attn(proj, diff_lambda[li], diff_norm[li], li, SEQ, BATCH, 0)
        o_diff = _diff_attn(proj, diff_lambda[li], diff_norm[li], li, DEC_SEQ, DEC_BATCH, N_CTX,
                            cache_diff_k, cache_diff_v, dst=o_diff)
        new_k, new_v = _kv_out(proj, li, new_k, new_v)

        alog = gdn_A_log[li].T.reshape(2 * G_HEADS, 1, 1)
        dtb = gdn_dt_bias[li].T.reshape(2 * G_HEADS, 1, 1)
        qkv_c = _gdn_prep(proj, gdn_conv[li], SEQ, BATCH, 0)
        o_gdn, new_gdn = _gdn(qkv_c, proj, gab, alog, dtb, gdn_norm[li], None, SEQ, BATCH, 0, li, dst_state=new_gdn)
        qkv_z = _gdn_prep(proj, gdn_conv[li], DEC_SEQ, DEC_BATCH, N_CTX)
        o_gdn, _ = _gdn(qkv_z, proj, gab, alog, dtb, gdn_norm[li], state_gdn[:, li], DEC_SEQ, DEC_BATCH, N_CTX, li,
                        dst=o_gdn)

        x, h2 = _merge(o_ret, o_diff, o_gdn, proj, wb_bf, wo_bf, li, x, g1, norm2_w[li], sc2, sh2)
        idx, gate = _peer_route(h2, wq_bf, li, peer_keys[li])
        x = _peer_experts(h2, idx, gate, u_bf, v_bf, li, x, g2)

    zeros_blk = jnp.zeros((N_MOD_BLK, 1, D_MODEL), F32)
    y = _normmod(x, norm_f_w, zeros_blk, zeros_blk, F32)
    y_prompt = y[:N_CTX].reshape(BATCH, SEQ, D_MODEL)
    y_sample = y[N_CTX:].reshape(DEC_BATCH, DEC_SEQ, D_MODEL)
    return (y_prompt, y_sample, new_k.reshape(BATCH, DEPTH, SEQ, D_HEADS, 2, D_HD),
            new_v.reshape(BATCH, DEPTH, SEQ, D_HEADS, 2 * D_HD), new_ret, new_gdn)
```

```python
import functools
import math

import numpy as np
import jax
import jax.numpy as jnp
from jax import lax
from jax.experimental import pallas as pl
from jax.experimental.pallas import tpu as pltpu

F32 = jnp.float32
BF16 = jnp.bfloat16

D_MODEL = 1024
BATCH = 16
SEQ = 256
DEPTH = 2
DEC_BATCH = 2
DEC_SEQ = 1024
PAST_LEN = 256
GRID_W = 64
ROPE_BASE = 10000.0
EPS = 1e-6
CHUNK = 64
R_HEADS, R_DK, R_DV = 4, 128, 256
D_HEADS, D_HD = 8, 64
G_HEADS, G_DK, G_DV = 8, 128, 128
CONV_K = 7
G_QKV = G_HEADS * (2 * G_DK + G_DV)
N_BRANCH = 3
BRANCH_W = 1024
PEER_HEADS = 8
PEER_DQ = 256
N_KEYS = 128
PEER_TOPK = 16
TOK_BLOCK = 128

N_CTX = BATCH * SEQ
N_LAT = DEC_BATCH * DEC_SEQ
N_TOK = N_CTX + N_LAT
N_COND = 1 + DEC_BATCH
MOD_BLK = 256
N_MOD_BLK = N_TOK // MOD_BLK

C_RQ, C_RK, C_RV, C_RG = 0, 512, 1024, 2048
C_DQ, C_DK, C_DV = 3072, 4096, 5120
C_GQKV, C_GZ, C_MG = 6144, 9216, 10240
N_MAIN = 13312
AB_OFF = 10240

VMEM_LIMIT = 56 * 1024 * 1024


def _cparams(sem):
    return pltpu.CompilerParams(dimension_semantics=sem, vmem_limit_bytes=VMEM_LIMIT)


def _mod_kernel(c_ref, w_ref, b_ref, o_ref):
    c = c_ref[...]
    a = c * jax.nn.sigmoid(c)
    o_ref[0] = jnp.dot(a, w_ref[0], preferred_element_type=F32,
                       precision=lax.Precision.HIGHEST) + b_ref[0]


def _modulation(cond_pad, w_mod, b_mod):
    tn = 1536
    return pl.pallas_call(
        _mod_kernel,
        out_shape=jax.ShapeDtypeStruct((DEPTH, 8, 6 * D_MODEL), F32),
        grid=(DEPTH, 6 * D_MODEL // tn),
        in_specs=[pl.BlockSpec((8, D_MODEL), lambda l, j: (0, 0)),
                  pl.BlockSpec((1, D_MODEL, tn), lambda l, j: (l, 0, j)),
                  pl.BlockSpec((1, 1, tn), lambda l, j: (l, 0, j))],
        out_specs=pl.BlockSpec((1, 8, tn), lambda l, j: (l, 0, j)),
        compiler_params=_cparams(("arbitrary", "arbitrary")),
        name="modulation",
    )(cond_pad, w_mod, b_mod.reshape(DEPTH, 1, 6 * D_MODEL))


def _normmod_kernel(x_ref, w_ref, sc_ref, sh_ref, o_ref):
    x = x_ref[...]
    y = x * lax.rsqrt(jnp.mean(x * x, axis=-1, keepdims=True) + EPS)
    y = y * w_ref[...]
    o_ref[...] = (y * (1.0 + sc_ref[0]) + sh_ref[0]).astype(o_ref.dtype)


def _normmod(x, w, sc_blk, sh_blk, out_dtype):
    tm = MOD_BLK
    return pl.pallas_call(
        _normmod_kernel,
        out_shape=jax.ShapeDtypeStruct((N_TOK, D_MODEL), out_dtype),
        grid=(N_TOK // tm,),
        in_specs=[pl.BlockSpec((tm, D_MODEL), lambda i: (i, 0)),
                  pl.BlockSpec((1, D_MODEL), lambda i: (0, 0)),
                  pl.BlockSpec((1, 1, D_MODEL), lambda i: (i, 0, 0)),
                  pl.BlockSpec((1, 1, D_MODEL), lambda i: (i, 0, 0))],
        out_specs=pl.BlockSpec((tm, D_MODEL), lambda i: (i, 0)),
        compiler_params=_cparams(("arbitrary",)),
        name="normmod",
    )(x, w.reshape(1, D_MODEL), sc_blk, sh_blk)


def _matmul_kernel(a_ref, b_ref, o_ref):
    o_ref[...] = jnp.dot(a_ref[...], b_ref[...], preferred_element_type=F32)


def _matmul(a, b, li, tm, tn):
    m, k = a.shape
    n = b.shape[2]
    assert m % tm == 0 and n % tn == 0, (m, n, tm, tn)
    return pl.pallas_call(
        _matmul_kernel,
        out_shape=jax.ShapeDtypeStruct((m, n), F32),
        grid=(n // tn, m // tm),
        in_specs=[pl.BlockSpec((tm, k), lambda j, i: (i, 0)),
                  pl.BlockSpec((None, k, tn), lambda j, i: (li, 0, j))],
        out_specs=pl.BlockSpec((tm, tn), lambda j, i: (i, j)),
        compiler_params=_cparams(("arbitrary", "arbitrary")),
        name="matmul",
    )(a, b)


def _merge_kernel(oa_ref, ob_ref, oc_ref, mga_ref, mgb_ref, mgc_ref, wb_ref, wo_ref, x_ref, g1_ref,
                  n2_ref, sc_ref, sh_ref, xo_ref, h_ref):
    mix = None
    for n, (o_ref, mg_ref) in enumerate(((oa_ref, mga_ref), (ob_ref, mgb_ref), (oc_ref, mgc_ref))):
        merged = jnp.dot(o_ref[...], wb_ref[n], preferred_element_type=F32)
        gate = jax.nn.sigmoid(mg_ref[...])
        mix = gate * merged if mix is None else mix + gate * merged
    m = jnp.dot(mix.astype(BF16), wo_ref[...], preferred_element_type=F32)
    x = x_ref[...] + g1_ref[0] * m
    xo_ref[...] = x
    y = x * lax.rsqrt(jnp.mean(x * x, axis=-1, keepdims=True) + EPS)
    y = y * n2_ref[...]
    h_ref[...] = (y * (1.0 + sc_ref[0]) + sh_ref[0]).astype(h_ref.dtype)


def _merge(o_ret, o_diff, o_gdn, proj, wb, wo, li, x, g1_blk, n2w, sc2_blk, sh2_blk):
    tm = MOD_BLK
    tok = lambda i: (i, 0)
    blk = lambda i: (i, 0, 0)
    return pl.pallas_call(
        _merge_kernel,
        out_shape=(jax.ShapeDtypeStruct((N_TOK, D_MODEL), F32),
                   jax.ShapeDtypeStruct((N_TOK, D_MODEL), BF16)),
        grid=(N_TOK // tm,),
        in_specs=[pl.BlockSpec((tm, BRANCH_W), tok),
                  pl.BlockSpec((tm, BRANCH_W), tok),
                  pl.BlockSpec((tm, BRANCH_W), tok),
                  pl.BlockSpec((tm, D_MODEL), lambda i: (i, C_MG // D_MODEL)),
                  pl.BlockSpec((tm, D_MODEL), lambda i: (i, C_MG // D_MODEL + 1)),
                  pl.BlockSpec((tm, D_MODEL), lambda i: (i, C_MG // D_MODEL + 2)),
                  pl.BlockSpec((None, N_BRANCH, BRANCH_W, D_MODEL), lambda i: (li, 0, 0, 0)),
                  pl.BlockSpec((None, D_MODEL, D_MODEL), lambda i: (li, 0, 0)),
                  pl.BlockSpec((tm, D_MODEL), tok),
                  pl.BlockSpec((1, 1, D_MODEL), blk),
                  pl.BlockSpec((1, D_MODEL), lambda i: (0, 0)),
                  pl.BlockSpec((1, 1, D_MODEL), blk),
                  pl.BlockSpec((1, 1, D_MODEL), blk)],
        out_specs=(pl.BlockSpec((tm, D_MODEL), tok), pl.BlockSpec((tm, D_MODEL), tok)),
        compiler_params=_cparams(("arbitrary",)),
        name="merge",
    )(o_ret, o_diff, o_gdn, proj, proj, proj, wb, wo, x, g1_blk, n2w.reshape(1, D_MODEL), sc2_blk, sh2_blk)


ROUTE_T = 128
ROUTE_HEADS = 4
N_PAIR = PEER_HEADS * PEER_TOPK
N_CAND = PEER_TOPK + (PEER_TOPK // 2) * (PEER_TOPK // 2 - 1) + PEER_TOPK // 2
NEG_INF = float("-inf")


def _split_bf16(x):
    hi = x.astype(BF16)
    return hi, (x - hi.astype(F32)).astype(BF16)


def _topk_rows(scores, rows, n_rows, payloads=None):
    scores = list(scores)
    vals = [[] for _ in scores]
    outs = [[] for _ in scores]
    for _ in range(PEER_TOPK):
        for n, s in enumerate(scores):
            m = jnp.max(s, axis=0, keepdims=True)
            pos = jnp.min(jnp.where(s == m, rows, n_rows), axis=0, keepdims=True)
            sel = rows == pos
            vals[n].append(m)
            if payloads is None:
                outs[n].append(pos)
            else:
                outs[n].append(jnp.sum(jnp.where(sel, payloads[n], 0), axis=0, keepdims=True))
            scores[n] = jnp.where(sel, NEG_INF, s)
    return [(jnp.concatenate(v, axis=0), jnp.concatenate(o, axis=0)) for v, o in zip(vals, outs)]


def _route_kernel(h_ref, wq_ref, khi_ref, klo_ref, idx_ref, gate_ref, q_scr, idx_scr, gate_scr):
    t = h_ref.shape[0]
    q = jnp.dot(h_ref[...], wq_ref[...], preferred_element_type=F32)
    for hp in range(2 * PEER_HEADS):
        q_scr[hp] = q[:, hp * 128:(hp + 1) * 128]
    rows_k = lax.broadcasted_iota(jnp.int32, (N_KEYS, t), 0)
    rows_c = lax.broadcasted_iota(jnp.int32, (N_CAND, t), 0)
    nt = (((1,), (1,)), ((), ()))

    half = PEER_TOPK // 2
    groups = [(slice(0, 1), slice(0, PEER_TOPK)), (slice(1, 2), slice(0, half))]
    groups += [(slice(k, k + 1), slice(0, half)) for k in range(2, half)]
    groups += [(slice(half, PEER_TOPK), slice(0, 1))]

    def heads(hh, carry):
        hs = [ROUTE_HEADS * hh + n for n in range(ROUTE_HEADS)]
        scores = []
        for h in hs:
            for p in range(2):
                qhi, qlo = _split_bf16(q_scr[2 * h + p])
                scores.append(lax.dot_general(khi_ref[p], qhi, nt, preferred_element_type=F32)
                              + lax.dot_general(khi_ref[p], qlo, nt, preferred_element_type=F32)
                              + lax.dot_general(klo_ref[p], qhi, nt, preferred_element_type=F32))
        tops = _topk_rows(scores, rows_k, N_KEYS)
        cands, cand_is = [], []
        for n in range(ROUTE_HEADS):
            (v1, i1), (v2, i2) = tops[2 * n], tops[2 * n + 1]
            cands.append(jnp.concatenate([v1[a] + v2[b] for a, b in groups], axis=0))
            cand_is.append(jnp.concatenate([i1[a] * N_KEYS + i2[b] for a, b in groups], axis=0))
        for h, (best, bidx) in zip(hs, _topk_rows(cands, rows_c, N_CAND, payloads=cand_is)):
            e = jnp.exp(best - best[0:1])
            gate_scr[h] = e / jnp.sum(e, axis=0, keepdims=True)
            idx_scr[h] = bidx
        return carry

    lax.fori_loop(0, PEER_HEADS // ROUTE_HEADS, heads, 0)
    idx_ref[...] = idx_scr[...].reshape(N_PAIR, t).T
    gate_ref[...] = gate_scr[...].reshape(N_PAIR, t).T


def _peer_route(h2, wq, li, keys):
    khi, klo = _split_bf16(keys)
    t = ROUTE_T
    n_tok = h2.shape[0]
    return pl.pallas_call(
        _route_kernel,
        out_shape=(jax.ShapeDtypeStruct((n_tok, N_PAIR), jnp.int32),
                   jax.ShapeDtypeStruct((n_tok, N_PAIR), F32)),
        grid=(n_tok // t,),
        in_specs=[pl.BlockSpec((t, D_MODEL), lambda i: (i, 0)),
                  pl.BlockSpec((None, D_MODEL, PEER_HEADS * PEER_DQ), lambda i: (li, 0, 0)),
                  pl.BlockSpec((2, N_KEYS, PEER_DQ // 2), lambda i: (0, 0, 0)),
                  pl.BlockSpec((2, N_KEYS, PEER_DQ // 2), lambda i: (0, 0, 0))],
        out_specs=(pl.BlockSpec((t, N_PAIR), lambda i: (i, 0)),
                   pl.BlockSpec((t, N_PAIR), lambda i: (i, 0))),
        scratch_shapes=[pltpu.VMEM((2 * PEER_HEADS, t, PEER_DQ // 2), F32),
                        pltpu.VMEM((PEER_HEADS, PEER_TOPK, t), jnp.int32),
                        pltpu.VMEM((PEER_HEADS, PEER_TOPK, t), F32)],
        compiler_params=_cparams(("arbitrary",)),
        name="peer_route",
    )(h2, wq, khi, klo)


EXP_TB = 256
EXP_EB = 2048
EXP_SUB = 256
EXP_AHEAD = 2
N_EXPERTS = N_KEYS * N_KEYS


def _expert_kernel(h_ref, idx_ref, gate_ref, u_ref, v_ref, x_ref, g2_ref, o_ref, g_scr, acc_ref):
    j = pl.program_id(1)
    tb = h_ref.shape[0]

    @pl.when(j == 0)
    def _():
        acc_ref[...] = jnp.zeros_like(acc_ref)
        sub = lax.broadcasted_iota(jnp.int32, (N_KEYS, N_PAIR), 0)

        def body(t, carry):
            e = idx_ref[pl.ds(t, 1), :]
            g = gate_ref[pl.ds(t, 1), :]
            xa = jnp.where(sub == (e >> 7), 1.0, 0.0).astype(BF16)
            yb = jnp.where(sub == (e & (N_KEYS - 1)), g, 0.0).astype(BF16)
            g_scr[pl.ds(pl.multiple_of(t * N_KEYS, N_KEYS), N_KEYS), :] = _dot_nt(xa, yb)
            return carry

        lax.fori_loop(0, tb, body, 0, unroll=16)

    h = h_ref[...]
    nsub = EXP_EB // EXP_SUB
    per = EXP_SUB // N_KEYS
    score = lambda c: _dot_nt(h, u_ref[c * EXP_SUB:(c + 1) * EXP_SUB, :])
    s = [score(c) for c in range(EXP_AHEAD)]
    acc = acc_ref[...]
    for c in range(nsub):
        gj = jnp.concatenate([g_scr[pl.ds(j * (nsub * per) + c * per + i, tb, stride=N_KEYS), :]
                              for i in range(per)], axis=1)
        w = 0.5 * s[c] * (1.0 + lax.erf(s[c] * (2.0 ** -0.5))) * gj
        acc = acc + jnp.dot(w.astype(BF16), v_ref[c * EXP_SUB:(c + 1) * EXP_SUB, :], preferred_element_type=F32)
        if c + EXP_AHEAD < nsub:
            s.append(score(c + EXP_AHEAD))
    acc_ref[...] = acc

    @pl.when(j == pl.num_programs(1) - 1)
    def _():
        o_ref[...] = x_ref[...] + g2_ref[0] * acc_ref[...]


def _peer_experts(h2, idx, gate, u, v, li, x, g2_blk):
    tb, eb = EXP_TB, EXP_EB
    n_tok = h2.shape[0]
    return pl.pallas_call(
        _expert_kernel,
        out_shape=jax.ShapeDtypeStruct((n_tok, D_MODEL), F32),
        grid=(n_tok // tb, N_EXPERTS // eb),
        in_specs=[pl.BlockSpec((tb, D_MODEL), lambda i, j: (i, 0)),
                  pl.BlockSpec((tb, N_PAIR), lambda i, j: (i, 0)),
                  pl.BlockSpec((tb, N_PAIR), lambda i, j: (i, 0)),
                  pl.BlockSpec((None, eb, D_MODEL), lambda i, j: (li, j, 0)),
                  pl.BlockSpec((None, eb, D_MODEL), lambda i, j: (li, j, 0)),
                  pl.BlockSpec((tb, D_MODEL), lambda i, j: (i, 0)),
                  pl.BlockSpec((1, 1, D_MODEL), lambda i, j: (i * tb // MOD_BLK, 0, 0))],
        out_specs=pl.BlockSpec((tb, D_MODEL), lambda i, j: (i, 0)),
        scratch_shapes=[pltpu.VMEM((tb * N_KEYS, N_KEYS), F32),
                        pltpu.VMEM((tb, D_MODEL), F32)],
        compiler_params=_cparams(("arbitrary", "arbitrary")),
        name="peer_experts",
    )(h2, idx, gate, u, v, x, g2_blk)


def _drop_refs(fn, pos, n, *refs):
    return fn(*refs[:pos], *refs[pos + n:])


def _branch_call(kernel_fn, dsts, in_specs, args, out_shape, out_specs, **kw):
    given = [(n, d) for n, d in enumerate(dsts) if d is not None]
    if given:
        pos = len(in_specs)
        in_specs = list(in_specs) + [pl.BlockSpec(memory_space=pl.ANY)] * len(given)
        args = list(args) + [d for _, d in given]
        kernel_fn = functools.partial(_drop_refs, kernel_fn, pos, len(given))
        kw["input_output_aliases"] = {pos + m: n for m, (n, _) in enumerate(given)}
    return pl.pallas_call(kernel_fn, out_shape=tuple(out_shape), in_specs=in_specs, out_specs=tuple(out_specs),
                          **kw)(*args)


def _kv_out_kernel(k_ref, v_ref, ko_ref, vo_ref):
    ko_ref[0] = k_ref[...]
    vo_ref[0] = v_ref[...]


def _kv_out(proj, li, dst_k, dst_v):
    shape = jax.ShapeDtypeStruct((BATCH, DEPTH, SEQ, D_HEADS * 2 * D_HD), F32)
    blk = D_HEADS * 2 * D_HD
    return _branch_call(
        _kv_out_kernel, [dst_k, dst_v],
        [pl.BlockSpec((SEQ, blk), lambda s: (s, C_DK // blk)), pl.BlockSpec((SEQ, blk), lambda s: (s, C_DV // blk))],
        [proj, proj], [shape, shape],
        [pl.BlockSpec((1, None, SEQ, blk), lambda s: (s, li, 0, 0))] * 2,
        grid=(BATCH,),
        compiler_params=_cparams(("arbitrary",)),
        name="kv_out",
    )


CONV_ROWS = 256
CONV_HALO = 8


def _gdn_prep_kernel(x_ref, w_ref, o_ref):
    sec = pl.program_id(1)
    L = x_ref.shape[0]
    half = CONV_K // 2
    n = CONV_ROWS + 2 * CONV_HALO
    zeros = jnp.zeros((CONV_HALO, 128), F32)
    for r0 in range(0, L, CONV_ROWS):
        for cb in range(G_HEADS):
            cols = slice(cb * 128, (cb + 1) * 128)
            top = x_ref[r0 - CONV_HALO:r0, cols] if r0 > 0 else zeros
            bot = x_ref[r0 + CONV_ROWS:r0 + CONV_ROWS + CONV_HALO, cols] if r0 + CONV_ROWS < L else zeros
            x = jnp.concatenate([top, x_ref[r0:r0 + CONV_ROWS, cols], bot], axis=0)
            w = w_ref[0, :, cols]
            acc = x * w[half:half + 1]
            for d in range(-half, half + 1):
                if d != 0:
                    acc = acc + pltpu.roll(x, (-d) % n, axis=0) * w[half + d:half + d + 1]
            y = acc[CONV_HALO:CONV_HALO + CONV_ROWS]
            y = y * jax.nn.sigmoid(y)
            inv = lax.rsqrt(jnp.sum(y * y, axis=-1, keepdims=True) + EPS)
            o_ref[r0:r0 + CONV_ROWS, cols] = y * jnp.where(sec < 2, inv, 1.0)


def _gdn_prep(proj, conv_w, seq_len, n_seq, tok_off):
    blk0 = tok_off // seq_len
    sec0 = C_GQKV // 1024
    return pl.pallas_call(
        _gdn_prep_kernel,
        out_shape=jax.ShapeDtypeStruct((n_seq * seq_len, G_QKV), F32),
        grid=(n_seq, 3),
        in_specs=[pl.BlockSpec((seq_len, 1024), lambda s, c: (blk0 + s, sec0 + c)),
                  pl.BlockSpec((1, CONV_K, 1024), lambda s, c: (0, 0, c))],
        out_specs=pl.BlockSpec((seq_len, 1024), lambda s, c: (s, c)),
        compiler_params=_cparams(("arbitrary", "arbitrary")),
        name="gdn_prep",
    )(proj, conv_w.reshape(1, CONV_K, G_QKV))


GDN_HB = 4


def _dot(a, b):
    return jnp.dot(a, b, preferred_element_type=F32)


def _dot_nt(a, b):
    return lax.dot_general(a, b, (((1,), (1,)), ((), ())), preferred_element_type=F32)


def _dot_tn(a, b):
    return lax.dot_general(a, b, (((0,), (0,)), ((), ())), preferred_element_type=F32)


def _mask_dot3(m, b):
    m16 = m.astype(BF16)
    bh = b.astype(BF16)
    r1 = b - bh.astype(F32)
    bm = r1.astype(BF16)
    bl = (r1 - bm.astype(F32)).astype(BF16)
    return _dot(m16, bh) + _dot(m16, bm) + _dot(m16, bl)


def _gdn_kernel(*refs, seq_len, has_init, emit_state):
    it = iter(refs)
    q_ref, k_ref, v_ref, z_ref, gab_ref, alog_ref, dtb_ref, nw_ref = [next(it) for _ in range(8)]
    s0_ref = next(it) if has_init else None
    o_ref = next(it)
    so_ref = next(it) if emit_state else None
    u_s, w_s, qg_s, kd_s, qk_s, gl_s, o_s = [next(it) for _ in range(7)]
    C = CHUNK
    nc = seq_len // C
    ri = lax.broadcasted_iota(jnp.int32, (C, C), 0)
    ci = lax.broadcasted_iota(jnp.int32, (C, C), 1)
    eye = jnp.where(ri == ci, 1.0, 0.0)
    ones = jnp.ones((C, C), F32)
    scale = G_DK ** -0.5
    incl = (ri >= ci, ri <= ci)
    strict = (ri > ci, ri < ci)
    levels = []
    for lv in range(6):
        b = 1 << lv
        same = (ri >> (lv + 1)) == (ci >> (lv + 1))
        r_hi, c_hi = (ri & b) != 0, (ci & b) != 0
        levels.append((same & r_hi & jnp.logical_not(c_hi), same & jnp.logical_not(r_hi) & c_hi))

    GROUP = 4
    HB = GDN_HB
    lanes = [slice(hh * 128, (hh + 1) * 128) for hh in range(HB)]
    tri = [jnp.where(m, 1.0, 0.0) for m in incl]

    def prep(grp, carry):
        chunks = [grp * GROUP + j for j in range(GROUP)]
        rows = [pl.ds(pl.multiple_of(c * C, C), C) for c in chunks]
        hj = [(hh, j) for hh in range(HB) for j in range(GROUP)]
        q = {(hh, j): q_ref[rows[j], lanes[hh]] * scale for hh, j in hj}
        k = {(hh, j): k_ref[rows[j], lanes[hh]] for hh, j in hj}
        v = {(hh, j): v_ref[rows[j], lanes[hh]] for hh, j in hj}
        k16 = {key: x.astype(BF16) for key, x in k.items()}
        kk = {key: _dot_nt(x, x) for key, x in k16.items()}
        qk = {key: _dot_nt(q[key].astype(BF16), k16[key]) for key in hj}
        probs = [(hh, j, d) for hh in range(HB) for j in range(GROUP) for d in range(2)]
        gab = [gab_ref[r, :] for r in rows]
        ga = lambda hh, j, d: gab[j][:, 2 * hh + d:2 * hh + d + 1]
        gb = lambda hh, j, d: gab[j][:, 2 * HB + 2 * hh + d:2 * HB + 2 * hh + d + 1]
        g = [-jnp.exp(alog_ref[2 * hh + d]) * jax.nn.softplus(ga(hh, j, d) + dtb_ref[2 * hh + d])
             for hh, j, d in probs]
        beta = [jax.nn.sigmoid(gb(hh, j, d)) for hh, j, d in probs]
        gcol = [_mask_dot3(tri[d], jnp.broadcast_to(g[p], (C, 128))) for p, (hh, j, d) in enumerate(probs)]
        grow = [_mask_dot3(ones, jnp.where(incl[1 - d], jnp.broadcast_to(g[p], (C, C)), 0.0))
                for p, (hh, j, d) in enumerate(probs)]
        dec = [jnp.exp(jnp.where(incl[d], gcol[p][:, :C] - grow[p], NEG_INF)) for p, (hh, j, d) in enumerate(probs)]
        a = [jnp.where(strict[d], kk[hh, j] * beta[p] * dec[p], 0.0) for p, (hh, j, d) in enumerate(probs)]
        t = [eye - jnp.where(levels[0][d], a[p], 0.0) for p, (hh, j, d) in enumerate(probs)]
        for lv in range(1, 6):
            t16 = [x.astype(BF16) for x in t]
            ct = [_dot(jnp.where(levels[lv][d], a[p], 0.0).astype(BF16), t16[p])
                  for p, (hh, j, d) in enumerate(probs)]
            t = [t[p] - _dot(t16[p], ct[p].astype(BF16)) for p in range(len(probs))]
        for p, (hh, j, d) in enumerate(probs):
            n = 2 * hh + d
            eg = jnp.exp(gcol[p])
            t16 = t[p].astype(BF16)
            glast = gcol[p][C - 1:C, :] if d == 0 else gcol[p][0:1, :]
            u_s[n, rows[j], :] = _dot(t16, (v[hh, j] * beta[p]).astype(BF16))
            w_s[n, rows[j], :] = _dot(t16, (k[hh, j] * beta[p] * eg).astype(BF16))
            qg_s[n, rows[j], :] = q[hh, j] * eg
            kd_s[n, rows[j], :] = k[hh, j] * jnp.exp(glast - gcol[p])
            qk_s[n, rows[j], :] = qk[hh, j] * dec[p]
            gl_s[n * nc + chunks[j]] = jnp.broadcast_to(jnp.exp(glast), (8, 128))
        return carry

    lax.fori_loop(0, nc // GROUP, prep, 0)
    o_s[...] = jnp.zeros_like(o_s)
    chains = [(hh, d) for hh in range(HB) for d in range(2)]

    def step(i, states):
        cs = (i, nc - 1 - i)
        rows = [pl.ds(pl.multiple_of(c * C, C), C) for c in cs]
        s16 = [s.astype(BF16) for s in states]
        ws = [_dot(w_s[n, rows[d], :].astype(BF16), s16[n]) for n, (hh, d) in enumerate(chains)]
        un16 = [(u_s[n, rows[d], :] - ws[n]).astype(BF16) for n, (hh, d) in enumerate(chains)]
        o = [_dot(qg_s[n, rows[d], :].astype(BF16), s16[n]) + _dot(qk_s[n, rows[d], :].astype(BF16), un16[n])
             for n, (hh, d) in enumerate(chains)]
        new = tuple(states[n] * gl_s[n * nc + cs[d]][0:1, :] + _dot_tn(kd_s[n, rows[d], :].astype(BF16), un16[n])
                    for n, (hh, d) in enumerate(chains))
        for n, (hh, d) in enumerate(chains):
            o_s[rows[d], lanes[hh]] += o[n]
        return new

    if has_init:
        init = tuple(s0_ref[0, d, hh] for hh, d in chains)
    else:
        init = (jnp.zeros((G_DK, G_DV), F32),) * len(chains)
    final = lax.fori_loop(0, nc, step, init)
    if emit_state:
        for n, (hh, d) in enumerate(chains):
            so_ref[0, d, hh] = final[n]
    for hh in range(HB):
        o = o_s[:, lanes[hh]]
        y = o * lax.rsqrt(jnp.mean(o * o, axis=-1, keepdims=True) + EPS) * nw_ref[...]
        z = z_ref[:, lanes[hh]]
        o_ref[:, lanes[hh]] = (y * (z * jax.nn.sigmoid(z))).astype(o_ref.dtype)


def _gdn_gate_cols():
    cols = []
    for blk in range(G_HEADS // GDN_HB):
        heads = range(blk * GDN_HB, (blk + 1) * GDN_HB)
        cols.append([kind * 2 * G_HEADS + d * G_HEADS + h for kind in range(2) for h in heads for d in range(2)])
    return np.asarray(cols, np.int32)


def _gdn(qkv, proj, gab, alog, dtb, nw, s0, seq_len, n_seq, tok_off, li, dst=None, dst_state=None):
    has_init = s0 is not None
    emit_state = not has_init
    blk0 = tok_off // seq_len
    hb = GDN_HB
    wl = hb * 128
    nblk = G_HEADS // hb
    tokcol = lambda off: pl.BlockSpec((seq_len, wl), lambda s, h: (s, off + h))
    in_specs = [tokcol(0), tokcol(nblk), tokcol(2 * nblk),
                pl.BlockSpec((seq_len, wl), lambda s, h: (blk0 + s, C_GZ // wl + h)),
                pl.BlockSpec((seq_len, 128), lambda s, h: (blk0 + s, h)),
                pl.BlockSpec((2 * hb, 1, 1), lambda s, h: (h, 0, 0)),
                pl.BlockSpec((2 * hb, 1, 1), lambda s, h: (h, 0, 0)),
                pl.BlockSpec((1, G_DV), lambda s, h: (0, 0))]
    args = [qkv, qkv, qkv, proj, gab, alog, dtb, nw.reshape(1, G_DV)]
    if has_init:
        in_specs.append(pl.BlockSpec((1, 2, hb, G_DK, G_DV), lambda s, h: (s, 0, h, 0, 0)))
        args.append(s0)
    out_shape = [jax.ShapeDtypeStruct((N_TOK, BRANCH_W), BF16)]
    out_specs = [pl.BlockSpec((seq_len, wl), lambda s, h: (blk0 + s, h))]
    if emit_state:
        out_shape.append(jax.ShapeDtypeStruct((n_seq, DEPTH, 2, G_HEADS, G_DK, G_DV), F32))
        out_specs.append(pl.BlockSpec((1, None, 2, hb, G_DK, G_DV), lambda s, h: (s, li, 0, h, 0, 0)))
    nc = seq_len // CHUNK
    res = _branch_call(
        functools.partial(_gdn_kernel, seq_len=seq_len, has_init=has_init, emit_state=emit_state),
        [dst, dst_state][:len(out_shape)], in_specs, args, out_shape, out_specs,
        grid=(n_seq, nblk),
        scratch_shapes=[pltpu.VMEM((2 * hb, seq_len, 128), F32)] * 4
                       + [pltpu.VMEM((2 * hb, seq_len, CHUNK), F32), pltpu.VMEM((2 * hb * nc, 8, 128), F32),
                          pltpu.VMEM((seq_len, wl), F32)],
        compiler_params=_cparams(("arbitrary", "arbitrary")),
        name="gdn",
    )
    return res if emit_state else (res[0], None)


def _rope_tables(seq_len, dim):
    n_rows = seq_len // GRID_W
    row = np.repeat(np.arange(n_rows), GRID_W).astype(np.float32)
    col = np.tile(np.arange(GRID_W), n_rows).astype(np.float32)
    nf = dim // 4
    inv = np.power(np.float32(ROPE_BASE), -np.arange(nf, dtype=np.float32) / np.float32(nf)).astype(np.float32)
    ang_r = row[:, None] * inv
    ang_c = col[:, None] * inv
    ang = np.concatenate([ang_r, ang_r, ang_c, ang_c], axis=-1)
    ang = np.tile(ang, (1, 128 // dim))
    return jnp.asarray(np.cos(ang), F32), jnp.asarray(np.sin(ang), F32)


def _rope(x, cos, sin, dim):
    quarter = dim // 4
    lane = lax.broadcasted_iota(jnp.int32, x.shape, 1)
    first = (lane & (2 * quarter - 1)) < quarter
    rot = jnp.where(first, -pltpu.roll(x, 128 - quarter, axis=1), pltpu.roll(x, quarter, axis=1))
    return x * cos + rot * sin


Q_TILE = 256


def _retention_kernel(*refs, seq_len, latent):
    it = iter(refs)
    q_ref, k_ref, v_ref, rg_ref, lg_ref, nw_ref = [next(it) for _ in range(6)]
    if latent:
        cos_ref, sin_ref, s0_ref = next(it), next(it), next(it)
    o_ref = next(it)
    so_ref = None if latent else next(it)
    L = seq_len
    lg = lg_ref[0]
    lg = jnp.minimum(lg, 0.0) - jnp.log(1.0 + jnp.exp(-jnp.abs(lg)))
    lgf, lgb = lg[0:1, :], lg[1:2, :]
    k = k_ref[...]
    if latent:
        k = _rope(k, cos_ref[...], sin_ref[...], R_DK)
    k16 = k.astype(BF16)
    v16 = v_ref[...].astype(BF16)
    for qt in range(L // Q_TILE):
        rows = slice(qt * Q_TILE, (qt + 1) * Q_TILE)
        q = q_ref[rows, :]
        if latent:
            q = _rope(q, cos_ref[rows, :], sin_ref[rows, :], R_DK)
        q16 = (q * R_DK ** -0.5).astype(BF16)
        a = _dot_nt(q16, k16)
        i = lax.broadcasted_iota(jnp.int32, (Q_TILE, L), 0) + qt * Q_TILE
        j = lax.broadcasted_iota(jnp.int32, (Q_TILE, L), 1)
        dist = (i - j).astype(F32)
        dec = (jnp.exp(jnp.where(dist >= 0, lgf * dist, NEG_INF))
               + jnp.exp(jnp.where(dist <= 0, -lgb * dist, NEG_INF)))
        o = _dot((a * dec).astype(BF16), v16)
        if latent:
            pos = (lax.broadcasted_iota(jnp.int32, (Q_TILE, 1), 0) + qt * Q_TILE).astype(F32)
            o = o + _dot(q16, s0_ref[0, 0, 0].astype(BF16)) * jnp.exp(lgf * (pos + 1.0))
            o = o + _dot(q16, s0_ref[0, 1, 0].astype(BF16)) * jnp.exp(lgb * (L - pos))
        y = o * lax.rsqrt(jnp.mean(o * o, axis=-1, keepdims=True) + EPS) * nw_ref[...]
        g = rg_ref[rows, :]
        o_ref[rows, :] = (y * (g * jax.nn.sigmoid(g))).astype(o_ref.dtype)
    if not latent:
        pos = lax.broadcasted_iota(jnp.int32, (L, 1), 0).astype(F32)
        so_ref[0, 0, 0] = _dot_tn((k * jnp.exp(lgf * (L - 1.0 - pos))).astype(BF16), v16)
        so_ref[0, 1, 0] = _dot_tn((k * jnp.exp(lgb * pos)).astype(BF16), v16)


def _retention(proj, decay, nw, seq_len, n_seq, tok_off, li, s0=None, dst=None, dst_state=None):
    latent = s0 is not None
    blk0 = tok_off // seq_len
    in_specs = [pl.BlockSpec((seq_len, R_DK), lambda s, h: (blk0 + s, C_RQ // R_DK + h)),
                pl.BlockSpec((seq_len, R_DK), lambda s, h: (blk0 + s, C_RK // R_DK + h)),
                pl.BlockSpec((seq_len, R_DV), lambda s, h: (blk0 + s, C_RV // R_DV + h)),
                pl.BlockSpec((seq_len, R_DV), lambda s, h: (blk0 + s, C_RG // R_DV + h)),
                pl.BlockSpec((1, 2, 1), lambda s, h: (h, 0, 0)),
                pl.BlockSpec((1, R_DV), lambda s, h: (0, 0))]
    args = [proj, proj, proj, proj, decay.T.reshape(R_HEADS, 2, 1), nw.reshape(1, R_DV)]
    out_shape = [jax.ShapeDtypeStruct((N_TOK, BRANCH_W), BF16)]
    out_specs = [pl.BlockSpec((seq_len, R_DV), lambda s, h: (blk0 + s, h))]
    if latent:
        cos, sin = _rope_tables(seq_len, R_DK)
        in_specs += [pl.BlockSpec((seq_len, 128), lambda s, h: (0, 0)),
                     pl.BlockSpec((seq_len, 128), lambda s, h: (0, 0)),
                     pl.BlockSpec((1, 2, 1, R_DK, R_DV), lambda s, h: (s, 0, h, 0, 0))]
        args += [cos, sin, s0]
    else:
        out_shape.append(jax.ShapeDtypeStruct((n_seq, DEPTH, 2, R_HEADS, R_DK, R_DV), F32))
        out_specs.append(pl.BlockSpec((1, None, 2, 1, R_DK, R_DV), lambda s, h: (s, li, 0, h, 0, 0)))
    res = _branch_call(
        functools.partial(_retention_kernel, seq_len=seq_len, latent=latent),
        [dst, dst_state][:len(out_shape)], in_specs, args, out_shape, out_specs,
        grid=(n_seq, R_HEADS),
        compiler_params=_cparams(("arbitrary", "arbitrary")),
        name="retention",
    )
    return (res[0], None) if latent else res


DA_HB = 4


def _diff_attn_kernel(*refs, seq_len, latent, lam_init):
    it = iter(refs)
    q_ref, k_ref, v_ref, lp_ref, nw_ref = [next(it) for _ in range(5)]
    if latent:
        cos_ref, sin_ref, ck_ref, cv_ref = [next(it) for _ in range(4)]
    o_ref = next(it)
    L = seq_len
    lp = lp_ref[...]
    lam = (jnp.exp(jnp.sum(lp[0:1] * lp[1:2], axis=-1, keepdims=True))
           - jnp.exp(jnp.sum(lp[2:3] * lp[3:4], axis=-1, keepdims=True)) + lam_init)
    w = 2 * D_HD
    lanes = [slice(hh * w, (hh + 1) * w) for hh in range(DA_HB)]
    keys, vals = [], []
    for hh in range(DA_HB):
        k = k_ref[:, lanes[hh]]
        if latent:
            k = _rope(k, cos_ref[...], sin_ref[...], D_HD)
        keys.append([k.astype(BF16)] + ([ck_ref[0, 0, :, lanes[hh]].astype(BF16)] if latent else []))
        vals.append([v_ref[:, lanes[hh]].astype(BF16)] + ([cv_ref[0, 0, :, lanes[hh]].astype(BF16)] if latent else []))
    lane = lax.broadcasted_iota(jnp.int32, (Q_TILE, w), 1)
    for qt in range(L // Q_TILE):
        rows = slice(qt * Q_TILE, (qt + 1) * Q_TILE)
        hp = [(hh, p) for hh in range(DA_HB) for p in range(2)]
        qs = []
        for hh in range(DA_HB):
            q = q_ref[rows, lanes[hh]]
            if latent:
                q = _rope(q, cos_ref[rows, :], sin_ref[rows, :], D_HD)
            qs.append(q * D_HD ** -0.5)
        s = [[_dot_nt(jnp.where((lane >= D_HD) == (p == 1), qs[hh], 0.0).astype(BF16), kk) for kk in keys[hh]]
             for hh, p in hp]
        outs = []
        for n, (hh, p) in enumerate(hp):
            m = s[n][0].max(axis=-1, keepdims=True)
            for x in s[n][1:]:
                m = jnp.maximum(m, x.max(axis=-1, keepdims=True))
            e = [jnp.exp(x - m) for x in s[n]]
            z = sum(x.sum(axis=-1, keepdims=True) for x in e)
            pv = sum(_dot(x.astype(BF16), vv) for x, vv in zip(e, vals[hh]))
            outs.append(pv / z)
        for hh in range(DA_HB):
            o = outs[2 * hh] - lam * outs[2 * hh + 1]
            y = o * lax.rsqrt(jnp.mean(o * o, axis=-1, keepdims=True) + EPS) * nw_ref[...]
            o_ref[rows, lanes[hh]] = (y * (1.0 - lam_init)).astype(o_ref.dtype)


def _diff_attn(proj, lam_p, nw, li, seq_len, n_seq, tok_off, cache_k=None, cache_v=None, dst=None):
    latent = cache_k is not None
    blk0 = tok_off // seq_len
    w = 2 * D_HD
    wl = DA_HB * w
    in_specs = [pl.BlockSpec((seq_len, wl), lambda s, h: (blk0 + s, C_DQ // wl + h)),
                pl.BlockSpec((seq_len, wl), lambda s, h: (blk0 + s, C_DK // wl + h)),
                pl.BlockSpec((seq_len, wl), lambda s, h: (blk0 + s, C_DV // wl + h)),
                pl.BlockSpec((4, D_HD), lambda s, h: (0, 0)),
                pl.BlockSpec((1, w), lambda s, h: (0, 0))]
    args = [proj, proj, proj, lam_p, nw.reshape(1, w)]
    if latent:
        cos, sin = _rope_tables(seq_len, D_HD)
        in_specs += [pl.BlockSpec((seq_len, 128), lambda s, h: (0, 0)),
                     pl.BlockSpec((seq_len, 128), lambda s, h: (0, 0)),
                     pl.BlockSpec((1, 1, PAST_LEN, wl), lambda s, h: (s, li, 0, h)),
                     pl.BlockSpec((1, 1, PAST_LEN, wl), lambda s, h: (s, li, 0, h))]
        args += [cos, sin, cache_k.reshape(DEC_BATCH, DEPTH, PAST_LEN, D_HEADS * w),
                 cache_v.reshape(DEC_BATCH, DEPTH, PAST_LEN, D_HEADS * w)]
    lam_init = 0.8 - 0.6 * math.exp(-0.3 * li)
    return _branch_call(
        functools.partial(_diff_attn_kernel, seq_len=seq_len, latent=latent, lam_init=lam_init),
        [dst], in_specs, args,
        [jax.ShapeDtypeStruct((N_TOK, BRANCH_W), BF16)],
        [pl.BlockSpec((seq_len, wl), lambda s, h: (blk0 + s, h))],
        grid=(n_seq, D_HEADS // DA_HB),
        compiler_params=_cparams(("arbitrary", "arbitrary")),
        name="diff_attn",
    )[0]


def _per_block(rows):
    idx = np.concatenate([np.zeros(N_CTX // MOD_BLK, np.int32)] +
                         [np.full(DEC_SEQ // MOD_BLK, 1 + b, np.int32) for b in range(DEC_BATCH)])
    return rows[idx][:, None, :]


def kernel(x_prompt, x_sample, cache_diff_k, cache_diff_v, state_ret, state_gdn, c, c_ctx,
           norm1_w, norm2_w, w_mod, b_mod, w_in, ret_decay, ret_norm, diff_lambda, diff_norm,
           gdn_conv, gdn_A_log, gdn_dt_bias, gdn_norm, w_branch, w_out,
           peer_wq, peer_keys, peer_u, peer_v, norm_f_w):
    x = jnp.concatenate([x_prompt.reshape(N_CTX, D_MODEL), x_sample.reshape(N_LAT, D_MODEL)], axis=0)
    cond = jnp.concatenate([c_ctx[None, :], c, jnp.zeros((8 - N_COND, D_MODEL), F32)], axis=0)
    mod_all = _modulation(cond, w_mod, b_mod)

    w_main = jnp.concatenate([w_in[:, :, :AB_OFF], w_in[:, :, AB_OFF + 32:]], axis=-1).astype(BF16)
    gate_cols = _gdn_gate_cols()
    w_ab = w_in[:, :, AB_OFF:AB_OFF + 32][:, :, gate_cols]
    w_ab = jnp.pad(w_ab, ((0, 0), (0, 0), (0, 0), (0, 128 - gate_cols.shape[1])))
    w_ab = w_ab.reshape(DEPTH, D_MODEL, gate_cols.shape[0] * 128).astype(BF16)
    wb_bf = w_branch.astype(BF16)
    wo_bf = w_out.astype(BF16)
    wq_bf = peer_wq.astype(BF16)
    u_bf = peer_u.astype(BF16)
    v_bf = peer_v.astype(BF16)

    new_k = new_v = new_ret = new_gdn = None
    for li in range(DEPTH):
        mod = mod_all[li, :N_COND]
        sh1, sc1, g1, sh2, sc2, g2 = [_per_block(m) for m in jnp.split(mod, 6, axis=-1)]
        h = _normmod(x, norm1_w[li], sc1, sh1, BF16)
        proj = _matmul(h, w_main, li, 1024, 1024)
        gab = _matmul(h, w_ab, li, 512, w_ab.shape[2])

        o_ret, new_ret = _retention(proj, ret_decay[li], ret_norm[li], SEQ, BATCH, 0, li, dst_state=new_ret)
        o_ret, _ = _retention(proj, ret_decay[li], ret_norm[li], DEC_SEQ, DEC_BATCH, N_CTX, li,
                              s0=state_ret[:, li], dst=o_ret)
        o_diff = _diff_attn(proj, diff_lambda[li], diff_norm[li], li, SEQ, BATCH, 0)
        o_diff = _diff_attn(proj, diff_lambda[li], diff_norm[li], li, DEC_SEQ, DEC_BATCH, N_CTX,
                            cache_diff_k, cache_diff_v, dst=o_diff)
        new_k, new_v = _kv_out(proj, li, new_k, new_v)

        alog = gdn_A_log[li].T.reshape(2 * G_HEADS, 1, 1)
        dtb = gdn_dt_bias[li].T.reshape(2 * G_HEADS, 1, 1)
        qkv_c = _gdn_prep(proj, gdn_conv[li], SEQ, BATCH, 0)
        o_gdn, new_gdn = _gdn(qkv_c, proj, gab, alog, dtb, gdn_norm[li], None, SEQ, BATCH, 0, li, dst_state=new_gdn)
        qkv_z = _gdn_prep(proj, gdn_conv[li], DEC_SEQ, DEC_BATCH, N_CTX)
        o_gdn, _ = _gdn(qkv_z, proj, gab, alog, dtb, gdn_norm[li], state_gdn[:, li], DEC_SEQ, DEC_BATCH, N_CTX, li,
                        dst=o_gdn)

        x, h2 = _merge(o_ret, o_diff, o_gdn, proj, wb_bf, wo_bf, li, x, g1, norm2_w[li], sc2, sh2)
        idx, gate = _peer_route(h2, wq_bf, li, peer_keys[li])
        x = _peer_experts(h2, idx, gate, u_bf, v_bf, li, x, g2)

    zeros_blk = jnp.zeros((N_MOD_BLK, 1, D_MODEL), F32)
    y = _normmod(x, norm_f_w, zeros_blk, zeros_blk, F32)
    y_prompt = y[:N_CTX].reshape(BATCH, SEQ, D_MODEL)
    y_sample = y[N_CTX:].reshape(DEC_BATCH, DEC_SEQ, D_MODEL)
    return (y_prompt, y_sample, new_k.reshape(BATCH, DEPTH, SEQ, D_HEADS, 2, D_HD),
            new_v.reshape(BATCH, DEPTH, SEQ, D_HEADS, 2 * D_HD), new_ret, new_gdn)
```

```python
import functools
import math

import numpy as np
import jax
import jax.numpy as jnp
from jax import lax
from jax.experimental import pallas as pl
from jax.experimental.pallas import tpu as pltpu

F32 = jnp.float32
BF16 = jnp.bfloat16

D_MODEL = 1024
BATCH = 16
SEQ = 256
DEPTH = 2
DEC_BATCH = 2
DEC_SEQ = 1024
PAST_LEN = 256
GRID_W = 64
ROPE_BASE = 10000.0
EPS = 1e-6
CHUNK = 64
R_HEADS, R_DK, R_DV = 4, 128, 256
D_HEADS, D_HD = 8, 64
G_HEADS, G_DK, G_DV = 8, 128, 128
CONV_K = 7
G_QKV = G_HEADS * (2 * G_DK + G_DV)
N_BRANCH = 3
BRANCH_W = 1024
PEER_HEADS = 8
PEER_DQ = 256
N_KEYS = 128
PEER_TOPK = 16
TOK_BLOCK = 128

N_CTX = BATCH * SEQ
N_LAT = DEC_BATCH * DEC_SEQ
N_TOK = N_CTX + N_LAT
N_COND = 1 + DEC_BATCH
MOD_BLK = 256
N_MOD_BLK = N_TOK // MOD_BLK

C_RQ, C_RK, C_RV, C_RG = 0, 512, 1024, 2048
C_DQ, C_DK, C_DV = 3072, 4096, 5120
C_GQKV, C_GZ, C_MG = 6144, 9216, 10240
N_MAIN = 13312
AB_OFF = 10240

VMEM_LIMIT = 56 * 1024 * 1024


def _cparams(sem):
    return pltpu.CompilerParams(dimension_semantics=sem, vmem_limit_bytes=VMEM_LIMIT)


def _mod_kernel(c_ref, w_ref, b_ref, o_ref):
    c = c_ref[...]
    a = c * jax.nn.sigmoid(c)
    o_ref[0] = jnp.dot(a, w_ref[0], preferred_element_type=F32,
                       precision=lax.Precision.HIGHEST) + b_ref[0]


def _modulation(cond_pad, w_mod, b_mod):
    tn = 1536
    return pl.pallas_call(
        _mod_kernel,
        out_shape=jax.ShapeDtypeStruct((DEPTH, 8, 6 * D_MODEL), F32),
        grid=(DEPTH, 6 * D_MODEL // tn),
        in_specs=[pl.BlockSpec((8, D_MODEL), lambda l, j: (0, 0)),
                  pl.BlockSpec((1, D_MODEL, tn), lambda l, j: (l, 0, j)),
                  pl.BlockSpec((1, 1, tn), lambda l, j: (l, 0, j))],
        out_specs=pl.BlockSpec((1, 8, tn), lambda l, j: (l, 0, j)),
        compiler_params=_cparams(("arbitrary", "arbitrary")),
        name="modulation",
    )(cond_pad, w_mod, b_mod.reshape(DEPTH, 1, 6 * D_MODEL))


def _normmod_kernel(x_ref, w_ref, sc_ref, sh_ref, o_ref):
    x = x_ref[...]
    y = x * lax.rsqrt(jnp.mean(x * x, axis=-1, keepdims=True) + EPS)
    y = y * w_ref[...]
    o_ref[...] = (y * (1.0 + sc_ref[0]) + sh_ref[0]).astype(o_ref.dtype)


def _normmod(x, w, sc_blk, sh_blk, out_dtype):
    tm = MOD_BLK
    return pl.pallas_call(
        _normmod_kernel,
        out_shape=jax.ShapeDtypeStruct((N_TOK, D_MODEL), out_dtype),
        grid=(N_TOK // tm,),
        in_specs=[pl.BlockSpec((tm, D_MODEL), lambda i: (i, 0)),
                  pl.BlockSpec((1, D_MODEL), lambda i: (0, 0)),
                  pl.BlockSpec((1, 1, D_MODEL), lambda i: (i, 0, 0)),
                  pl.BlockSpec((1, 1, D_MODEL), lambda i: (i, 0, 0))],
        out_specs=pl.BlockSpec((tm, D_MODEL), lambda i: (i, 0)),
        compiler_params=_cparams(("arbitrary",)),
        name="normmod",
    )(x, w.reshape(1, D_MODEL), sc_blk, sh_blk)


def _matmul_kernel(a_ref, b_ref, o_ref):
    o_ref[...] = jnp.dot(a_ref[...], b_ref[...], preferred_element_type=F32)


def _matmul(a, b, li, tm, tn):
    m, k = a.shape
    n = b.shape[2]
    assert m % tm == 0 and n % tn == 0, (m, n, tm, tn)
    return pl.pallas_call(
        _matmul_kernel,
        out_shape=jax.ShapeDtypeStruct((m, n), F32),
        grid=(n // tn, m // tm),
        in_specs=[pl.BlockSpec((tm, k), lambda j, i: (i, 0)),
                  pl.BlockSpec((None, k, tn), lambda j, i: (li, 0, j))],
        out_specs=pl.BlockSpec((tm, tn), lambda j, i: (i, j)),
        compiler_params=_cparams(("arbitrary", "arbitrary")),
        name="matmul",
    )(a, b)


def _merge_kernel(oa_ref, ob_ref, oc_ref, mga_ref, mgb_ref, mgc_ref, wb_ref, wo_ref, x_ref, g1_ref,
                  n2_ref, sc_ref, sh_ref, xo_ref, h_ref):
    mix = None
    for n, (o_ref, mg_ref) in enumerate(((oa_ref, mga_ref), (ob_ref, mgb_ref), (oc_ref, mgc_ref))):
        merged = jnp.dot(o_ref[...], wb_ref[n], preferred_element_type=F32)
        gate = jax.nn.sigmoid(mg_ref[...])
        mix = gate * merged if mix is None else mix + gate * merged
    m = jnp.dot(mix.astype(BF16), wo_ref[...], preferred_element_type=F32)
    x = x_ref[...] + g1_ref[0] * m
    xo_ref[...] = x
    y = x * lax.rsqrt(jnp.mean(x * x, axis=-1, keepdims=True) + EPS)
    y = y * n2_ref[...]
    h_ref[...] = (y * (1.0 + sc_ref[0]) + sh_ref[0]).astype(h_ref.dtype)


def _merge(o_ret, o_diff, o_gdn, proj, wb, wo, li, x, g1_blk, n2w, sc2_blk, sh2_blk):
    tm = MOD_BLK
    tok = lambda i: (i, 0)
    blk = lambda i: (i, 0, 0)
    return pl.pallas_call(
        _merge_kernel,
        out_shape=(jax.ShapeDtypeStruct((N_TOK, D_MODEL), F32),
                   jax.ShapeDtypeStruct((N_TOK, D_MODEL), BF16)),
        grid=(N_TOK // tm,),
        in_specs=[pl.BlockSpec((tm, BRANCH_W), tok),
                  pl.BlockSpec((tm, BRANCH_W), tok),
                  pl.BlockSpec((tm, BRANCH_W), tok),
                  pl.BlockSpec((tm, D_MODEL), lambda i: (i, C_MG // D_MODEL)),
                  pl.BlockSpec((tm, D_MODEL), lambda i: (i, C_MG // D_MODEL + 1)),
                  pl.BlockSpec((tm, D_MODEL), lambda i: (i, C_MG // D_MODEL + 2)),
                  pl.BlockSpec((None, N_BRANCH, BRANCH_W, D_MODEL), lambda i: (li, 0, 0, 0)),
                  pl.BlockSpec((None, D_MODEL, D_MODEL), lambda i: (li, 0, 0)),
                  pl.BlockSpec((tm, D_MODEL), tok),
                  pl.BlockSpec((1, 1, D_MODEL), blk),
                  pl.BlockSpec((1, D_MODEL), lambda i: (0, 0)),
                  pl.BlockSpec((1, 1, D_MODEL), blk),
                  pl.BlockSpec((1, 1, D_MODEL), blk)],
        out_specs=(pl.BlockSpec((tm, D_MODEL), tok), pl.BlockSpec((tm, D_MODEL), tok)),
        compiler_params=_cparams(("arbitrary",)),
        name="merge",
    )(o_ret, o_diff, o_gdn, proj, proj, proj, wb, wo, x, g1_blk, n2w.reshape(1, D_MODEL), sc2_blk, sh2_blk)


ROUTE_T = 128
ROUTE_HEADS = 4
N_PAIR = PEER_HEADS * PEER_TOPK
N_CAND = PEER_TOPK + (PEER_TOPK // 2) * (PEER_TOPK // 2 - 1) + PEER_TOPK // 2
NEG_INF = float("-inf")


def _split_bf16(x):
    hi = x.astype(BF16)
    return hi, (x - hi.astype(F32)).astype(BF16)


def _topk_rows(scores, rows, n_rows, payloads=None):
    scores = list(scores)
    vals = [[] for _ in scores]
    outs = [[] for _ in scores]
    for _ in range(PEER_TOPK):
        for n, s in enumerate(scores):
            m = jnp.max(s, axis=0, keepdims=True)
            pos = jnp.min(jnp.where(s == m, rows, n_rows), axis=0, keepdims=True)
            sel = rows == pos
            vals[n].append(m)
            if payloads is None:
                outs[n].append(pos)
            else:
                outs[n].append(jnp.sum(jnp.where(sel, payloads[n], 0), axis=0, keepdims=True))
            scores[n] = jnp.where(sel, NEG_INF, s)
    return [(jnp.concatenate(v, axis=0), jnp.concatenate(o, axis=0)) for v, o in zip(vals, outs)]


def _route_kernel(h_ref, wq_ref, khi_ref, klo_ref, idx_ref, gate_ref, q_scr, idx_scr, gate_scr):
    t = h_ref.shape[0]
    q = jnp.dot(h_ref[...], wq_ref[...], preferred_element_type=F32)
    for hp in range(2 * PEER_HEADS):
        q_scr[hp] = q[:, hp * 128:(hp + 1) * 128]
    rows_k = lax.broadcasted_iota(jnp.int32, (N_KEYS, t), 0)
    rows_c = lax.broadcasted_iota(jnp.int32, (N_CAND, t), 0)
    nt = (((1,), (1,)), ((), ()))

    half = PEER_TOPK // 2
    groups = [(slice(0, 1), slice(0, PEER_TOPK)), (slice(1, 2), slice(0, half))]
    groups += [(slice(k, k + 1), slice(0, half)) for k in range(2, half)]
    groups += [(slice(half, PEER_TOPK), slice(0, 1))]

    def heads(hh, carry):
        hs = [ROUTE_HEADS * hh + n for n in range(ROUTE_HEADS)]
        scores = []
        for h in hs:
            for p in range(2):
                qhi, qlo = _split_bf16(q_scr[2 * h + p])
                scores.append(lax.dot_general(khi_ref[p], qhi, nt, preferred_element_type=F32)
                              + lax.dot_general(khi_ref[p], qlo, nt, preferred_element_type=F32)
                              + lax.dot_general(klo_ref[p], qhi, nt, preferred_element_type=F32))
        tops = _topk_rows(scores, rows_k, N_KEYS)
        cands, cand_is = [], []
        for n in range(ROUTE_HEADS):
            (v1, i1), (v2, i2) = tops[2 * n], tops[2 * n + 1]
            cands.append(jnp.concatenate([v1[a] + v2[b] for a, b in groups], axis=0))
            cand_is.append(jnp.concatenate([i1[a] * N_KEYS + i2[b] for a, b in groups], axis=0))
        for h, (best, bidx) in zip(hs, _topk_rows(cands, rows_c, N_CAND, payloads=cand_is)):
            e = jnp.exp(best - best[0:1])
            gate_scr[h] = e / jnp.sum(e, axis=0, keepdims=True)
            idx_scr[h] = bidx
        return carry

    lax.fori_loop(0, PEER_HEADS // ROUTE_HEADS, heads, 0)
    idx_ref[...] = idx_scr[...].reshape(N_PAIR, t).T
    gate_ref[...] = gate_scr[...].reshape(N_PAIR, t).T


def _peer_route(h2, wq, li, keys):
    khi, klo = _split_bf16(keys)
    t = ROUTE_T
    n_tok = h2.shape[0]
    return pl.pallas_call(
        _route_kernel,
        out_shape=(jax.ShapeDtypeStruct((n_tok, N_PAIR), jnp.int32),
                   jax.ShapeDtypeStruct((n_tok, N_PAIR), F32)),
        grid=(n_tok // t,),
        in_specs=[pl.BlockSpec((t, D_MODEL), lambda i: (i, 0)),
                  pl.BlockSpec((None, D_MODEL, PEER_HEADS * PEER_DQ), lambda i: (li, 0, 0)),
                  pl.BlockSpec((2, N_KEYS, PEER_DQ // 2), lambda i: (0, 0, 0)),
                  pl.BlockSpec((2, N_KEYS, PEER_DQ // 2), lambda i: (0, 0, 0))],
        out_specs=(pl.BlockSpec((t, N_PAIR), lambda i: (i, 0)),
                   pl.BlockSpec((t, N_PAIR), lambda i: (i, 0))),
        scratch_shapes=[pltpu.VMEM((2 * PEER_HEADS, t, PEER_DQ // 2), F32),
                        pltpu.VMEM((PEER_HEADS, PEER_TOPK, t), jnp.int32),
                        pltpu.VMEM((PEER_HEADS, PEER_TOPK, t), F32)],
        compiler_params=_cparams(("arbitrary",)),
        name="peer_route",
    )(h2, wq, khi, klo)


EXP_TB = 256
EXP_EB = 2048
EXP_SUB = 256
EXP_AHEAD = 2
N_EXPERTS = N_KEYS * N_KEYS


def _expert_kernel(h_ref, idx_ref, gate_ref, u_ref, v_ref, x_ref, g2_ref, o_ref, g_scr, acc_ref):
    j = pl.program_id(1)
    tb = h_ref.shape[0]

    @pl.when(j == 0)
    def _():
        acc_ref[...] = jnp.zeros_like(acc_ref)
        sub = lax.broadcasted_iota(jnp.int32, (N_KEYS, N_PAIR), 0)

        def body(t, carry):
            e = idx_ref[pl.ds(t, 1), :]
            g = gate_ref[pl.ds(t, 1), :]
            xa = jnp.where(sub == (e >> 7), 1.0, 0.0).astype(BF16)
            yb = jnp.where(sub == (e & (N_KEYS - 1)), g, 0.0).astype(BF16)
            g_scr[pl.ds(pl.multiple_of(t * N_KEYS, N_KEYS), N_KEYS), :] = _dot_nt(xa, yb)
            return carry

        lax.fori_loop(0, tb, body, 0, unroll=32)

    h = h_ref[...]
    nsub = EXP_EB // EXP_SUB
    per = EXP_SUB // N_KEYS
    score = lambda c: _dot_nt(h, u_ref[c * EXP_SUB:(c + 1) * EXP_SUB, :])
    s = [score(c) for c in range(EXP_AHEAD)]
    acc = acc_ref[...]
    for c in range(nsub):
        gj = jnp.concatenate([g_scr[pl.ds(j * (nsub * per) + c * per + i, tb, stride=N_KEYS), :]
                              for i in range(per)], axis=1)
        w = 0.5 * s[c] * (1.0 + lax.erf(s[c] * (2.0 ** -0.5))) * gj
        acc = acc + jnp.dot(w.astype(BF16), v_ref[c * EXP_SUB:(c + 1) * EXP_SUB, :], preferred_element_type=F32)
        if c + EXP_AHEAD < nsub:
            s.append(score(c + EXP_AHEAD))
    acc_ref[...] = acc

    @pl.when(j == pl.num_programs(1) - 1)
    def _():
        o_ref[...] = x_ref[...] + g2_ref[0] * acc_ref[...]


def _peer_experts(h2, idx, gate, u, v, li, x, g2_blk):
    tb, eb = EXP_TB, EXP_EB
    n_tok = h2.shape[0]
    return pl.pallas_call(
        _expert_kernel,
        out_shape=jax.ShapeDtypeStruct((n_tok, D_MODEL), F32),
        grid=(n_tok // tb, N_EXPERTS // eb),
        in_specs=[pl.BlockSpec((tb, D_MODEL), lambda i, j: (i, 0)),
                  pl.BlockSpec((tb, N_PAIR), lambda i, j: (i, 0)),
                  pl.BlockSpec((tb, N_PAIR), lambda i, j: (i, 0)),
                  pl.BlockSpec((None, eb, D_MODEL), lambda i, j: (li, j, 0)),
                  pl.BlockSpec((None, eb, D_MODEL), lambda i, j: (li, j, 0)),
                  pl.BlockSpec((tb, D_MODEL), lambda i, j: (i, 0)),
                  pl.BlockSpec((1, 1, D_MODEL), lambda i, j: (i * tb // MOD_BLK, 0, 0))],
        out_specs=pl.BlockSpec((tb, D_MODEL), lambda i, j: (i, 0)),
        scratch_shapes=[pltpu.VMEM((tb * N_KEYS, N_KEYS), F32),
                        pltpu.VMEM((tb, D_MODEL), F32)],
        compiler_params=_cparams(("arbitrary", "arbitrary")),
        name="peer_experts",
    )(h2, idx, gate, u, v, x, g2_blk)


def _drop_refs(fn, pos, n, *refs):
    return fn(*refs[:pos], *refs[pos + n:])


def _branch_call(kernel_fn, dsts, in_specs, args, out_shape, out_specs, **kw):
    given = [(n, d) for n, d in enumerate(dsts) if d is not None]
    if given:
        pos = len(in_specs)
        in_specs = list(in_specs) + [pl.BlockSpec(memory_space=pl.ANY)] * len(given)
        args = list(args) + [d for _, d in given]
        kernel_fn = functools.partial(_drop_refs, kernel_fn, pos, len(given))
        kw["input_output_aliases"] = {pos + m: n for m, (n, _) in enumerate(given)}
    return pl.pallas_call(kernel_fn, out_shape=tuple(out_shape), in_specs=in_specs, out_specs=tuple(out_specs),
                          **kw)(*args)


def _kv_out_kernel(k_ref, v_ref, ko_ref, vo_ref):
    ko_ref[0] = k_ref[...]
    vo_ref[0] = v_ref[...]


def _kv_out(proj, li, dst_k, dst_v):
    shape = jax.ShapeDtypeStruct((BATCH, DEPTH, SEQ, D_HEADS * 2 * D_HD), F32)
    blk = D_HEADS * 2 * D_HD
    return _branch_call(
        _kv_out_kernel, [dst_k, dst_v],
        [pl.BlockSpec((SEQ, blk), lambda s: (s, C_DK // blk)), pl.BlockSpec((SEQ, blk), lambda s: (s, C_DV // blk))],
        [proj, proj], [shape, shape],
        [pl.BlockSpec((1, None, SEQ, blk), lambda s: (s, li, 0, 0))] * 2,
        grid=(BATCH,),
        compiler_params=_cparams(("arbitrary",)),
        name="kv_out",
    )


CONV_ROWS = 256
CONV_HALO = 8


def _gdn_prep_kernel(x_ref, w_ref, o_ref):
    sec = pl.program_id(1)
    L = x_ref.shape[0]
    half = CONV_K // 2
    n = CONV_ROWS + 2 * CONV_HALO
    zeros = jnp.zeros((CONV_HALO, 128), F32)
    for r0 in range(0, L, CONV_ROWS):
        for cb in range(G_HEADS):
            cols = slice(cb * 128, (cb + 1) * 128)
            top = x_ref[r0 - CONV_HALO:r0, cols] if r0 > 0 else zeros
            bot = x_ref[r0 + CONV_ROWS:r0 + CONV_ROWS + CONV_HALO, cols] if r0 + CONV_ROWS < L else zeros
            x = jnp.concatenate([top, x_ref[r0:r0 + CONV_ROWS, cols], bot], axis=0)
            w = w_ref[0, :, cols]
            acc = x * w[half:half + 1]
            for d in range(-half, half + 1):
                if d != 0:
                    acc = acc + pltpu.roll(x, (-d) % n, axis=0) * w[half + d:half + d + 1]
            y = acc[CONV_HALO:CONV_HALO + CONV_ROWS]
            y = y * jax.nn.sigmoid(y)
            inv = lax.rsqrt(jnp.sum(y * y, axis=-1, keepdims=True) + EPS)
            o_ref[r0:r0 + CONV_ROWS, cols] = y * jnp.where(sec < 2, inv, 1.0)


def _gdn_prep(proj, conv_w, seq_len, n_seq, tok_off):
    blk0 = tok_off // seq_len
    sec0 = C_GQKV // 1024
    return pl.pallas_call(
        _gdn_prep_kernel,
        out_shape=jax.ShapeDtypeStruct((n_seq * seq_len, G_QKV), F32),
        grid=(n_seq, 3),
        in_specs=[pl.BlockSpec((seq_len, 1024), lambda s, c: (blk0 + s, sec0 + c)),
                  pl.BlockSpec((1, CONV_K, 1024), lambda s, c: (0, 0, c))],
        out_specs=pl.BlockSpec((seq_len, 1024), lambda s, c: (s, c)),
        compiler_params=_cparams(("arbitrary", "arbitrary")),
        name="gdn_prep",
    )(proj, conv_w.reshape(1, CONV_K, G_QKV))


GDN_HB = 4


def _dot(a, b):
    return jnp.dot(a, b, preferred_element_type=F32)


def _dot_nt(a, b):
    return lax.dot_general(a, b, (((1,), (1,)), ((), ())), preferred_element_type=F32)


def _dot_tn(a, b):
    return lax.dot_general(a, b, (((0,), (0,)), ((), ())), preferred_element_type=F32)


def _mask_dot3(m, b):
    m16 = m.astype(BF16)
    bh = b.astype(BF16)
    r1 = b - bh.astype(F32)
    bm = r1.astype(BF16)
    bl = (r1 - bm.astype(F32)).astype(BF16)
    return _dot(m16, bh) + _dot(m16, bm) + _dot(m16, bl)


def _gdn_kernel(*refs, seq_len, has_init, emit_state):
    it = iter(refs)
    q_ref, k_ref, v_ref, z_ref, gab_ref, alog_ref, dtb_ref, nw_ref = [next(it) for _ in range(8)]
    s0_ref = next(it) if has_init else None
    o_ref = next(it)
    so_ref = next(it) if emit_state else None
    u_s, w_s, qg_s, kd_s, qk_s, gl_s, o_s = [next(it) for _ in range(7)]
    C = CHUNK
    nc = seq_len // C
    ri = lax.broadcasted_iota(jnp.int32, (C, C), 0)
    ci = lax.broadcasted_iota(jnp.int32, (C, C), 1)
    eye = jnp.where(ri == ci, 1.0, 0.0)
    ones = jnp.ones((C, C), F32)
    scale = G_DK ** -0.5
    incl = (ri >= ci, ri <= ci)
    strict = (ri > ci, ri < ci)
    levels = []
    for lv in range(6):
        b = 1 << lv
        same = (ri >> (lv + 1)) == (ci >> (lv + 1))
        r_hi, c_hi = (ri & b) != 0, (ci & b) != 0
        levels.append((same & r_hi & jnp.logical_not(c_hi), same & jnp.logical_not(r_hi) & c_hi))

    GROUP = 4
    HB = GDN_HB
    lanes = [slice(hh * 128, (hh + 1) * 128) for hh in range(HB)]
    tri = [jnp.where(m, 1.0, 0.0) for m in incl]

    def prep(grp, carry):
        chunks = [grp * GROUP + j for j in range(GROUP)]
        rows = [pl.ds(pl.multiple_of(c * C, C), C) for c in chunks]
        hj = [(hh, j) for hh in range(HB) for j in range(GROUP)]
        q = {(hh, j): q_ref[rows[j], lanes[hh]] * scale for hh, j in hj}
        k = {(hh, j): k_ref[rows[j], lanes[hh]] for hh, j in hj}
        v = {(hh, j): v_ref[rows[j], lanes[hh]] for hh, j in hj}
        k16 = {key: x.astype(BF16) for key, x in k.items()}
        kk = {key: _dot_nt(x, x) for key, x in k16.items()}
        qk = {key: _dot_nt(q[key].astype(BF16), k16[key]) for key in hj}
        probs = [(hh, j, d) for hh in range(HB) for j in range(GROUP) for d in range(2)]
        gab = [gab_ref[r, :] for r in rows]
        ga = lambda hh, j, d: gab[j][:, 2 * hh + d:2 * hh + d + 1]
        gb = lambda hh, j, d: gab[j][:, 2 * HB + 2 * hh + d:2 * HB + 2 * hh + d + 1]
        g = [-jnp.exp(alog_ref[2 * hh + d]) * jax.nn.softplus(ga(hh, j, d) + dtb_ref[2 * hh + d])
             for hh, j, d in probs]
        beta = [jax.nn.sigmoid(gb(hh, j, d)) for hh, j, d in probs]
        gcol = [_mask_dot3(tri[d], jnp.broadcast_to(g[p], (C, 128))) for p, (hh, j, d) in enumerate(probs)]
        grow = [_mask_dot3(ones, jnp.where(incl[1 - d], jnp.broadcast_to(g[p], (C, C)), 0.0))
                for p, (hh, j, d) in enumerate(probs)]
        dec = [jnp.exp(jnp.where(incl[d], gcol[p][:, :C] - grow[p], NEG_INF)) for p, (hh, j, d) in enumerate(probs)]
        a = [jnp.where(strict[d], kk[hh, j] * beta[p] * dec[p], 0.0) for p, (hh, j, d) in enumerate(probs)]
        t = [eye - jnp.where(levels[0][d], a[p], 0.0) for p, (hh, j, d) in enumerate(probs)]
        for lv in range(1, 6):
            t16 = [x.astype(BF16) for x in t]
            ct = [_dot(jnp.where(levels[lv][d], a[p], 0.0).astype(BF16), t16[p])
                  for p, (hh, j, d) in enumerate(probs)]
            t = [t[p] - _dot(t16[p], ct[p].astype(BF16)) for p in range(len(probs))]
        for p, (hh, j, d) in enumerate(probs):
            n = 2 * hh + d
            eg = jnp.exp(gcol[p])
            t16 = t[p].astype(BF16)
            glast = gcol[p][C - 1:C, :] if d == 0 else gcol[p][0:1, :]
            u_s[n, rows[j], :] = _dot(t16, (v[hh, j] * beta[p]).astype(BF16))
            w_s[n, rows[j], :] = _dot(t16, (k[hh, j] * beta[p] * eg).astype(BF16))
            qg_s[n, rows[j], :] = q[hh, j] * eg
            kd_s[n, rows[j], :] = k[hh, j] * jnp.exp(glast - gcol[p])
            qk_s[n, rows[j], :] = qk[hh, j] * dec[p]
            gl_s[n * nc + chunks[j]] = jnp.broadcast_to(jnp.exp(glast), (8, 128))
        return carry

    lax.fori_loop(0, nc // GROUP, prep, 0)
    o_s[...] = jnp.zeros_like(o_s)
    chains = [(hh, d) for hh in range(HB) for d in range(2)]

    def step(i, states):
        cs = (i, nc - 1 - i)
        rows = [pl.ds(pl.multiple_of(c * C, C), C) for c in cs]
        s16 = [s.astype(BF16) for s in states]
        ws = [_dot(w_s[n, rows[d], :].astype(BF16), s16[n]) for n, (hh, d) in enumerate(chains)]
        un16 = [(u_s[n, rows[d], :] - ws[n]).astype(BF16) for n, (hh, d) in enumerate(chains)]
        o = [_dot(qg_s[n, rows[d], :].astype(BF16), s16[n]) + _dot(qk_s[n, rows[d], :].astype(BF16), un16[n])
             for n, (hh, d) in enumerate(chains)]
        new = tuple(states[n] * gl_s[n * nc + cs[d]][0:1, :] + _dot_tn(kd_s[n, rows[d], :].astype(BF16), un16[n])
                    for n, (hh, d) in enumerate(chains))
        for n, (hh, d) in enumerate(chains):
            o_s[rows[d], lanes[hh]] += o[n]
        return new

    if has_init:
        init = tuple(s0_ref[0, d, hh] for hh, d in chains)
    else:
        init = (jnp.zeros((G_DK, G_DV), F32),) * len(chains)
    final = lax.fori_loop(0, nc, step, init)
    if emit_state:
        for n, (hh, d) in enumerate(chains):
            so_ref[0, d, hh] = final[n]
    for hh in range(HB):
        o = o_s[:, lanes[hh]]
        y = o * lax.rsqrt(jnp.mean(o * o, axis=-1, keepdims=True) + EPS) * nw_ref[...]
        z = z_ref[:, lanes[hh]]
        o_ref[:, lanes[hh]] = (y * (z * jax.nn.sigmoid(z))).astype(o_ref.dtype)


def _gdn_gate_cols():
    cols = []
    for blk in range(G_HEADS // GDN_HB):
        heads = range(blk * GDN_HB, (blk + 1) * GDN_HB)
        cols.append([kind * 2 * G_HEADS + d * G_HEADS + h for kind in range(2) for h in heads for d in range(2)])
    return np.asarray(cols, np.int32)


def _gdn(qkv, proj, gab, alog, dtb, nw, s0, seq_len, n_seq, tok_off, li, dst=None, dst_state=None):
    has_init = s0 is not None
    emit_state = not has_init
    blk0 = tok_off // seq_len
    hb = GDN_HB
    wl = hb * 128
    nblk = G_HEADS // hb
    tokcol = lambda off: pl.BlockSpec((seq_len, wl), lambda s, h: (s, off + h))
    in_specs = [tokcol(0), tokcol(nblk), tokcol(2 * nblk),
                pl.BlockSpec((seq_len, wl), lambda s, h: (blk0 + s, C_GZ // wl + h)),
                pl.BlockSpec((seq_len, 128), lambda s, h: (blk0 + s, h)),
                pl.BlockSpec((2 * hb, 1, 1), lambda s, h: (h, 0, 0)),
                pl.BlockSpec((2 * hb, 1, 1), lambda s, h: (h, 0, 0)),
                pl.BlockSpec((1, G_DV), lambda s, h: (0, 0))]
    args = [qkv, qkv, qkv, proj, gab, alog, dtb, nw.reshape(1, G_DV)]
    if has_init:
        in_specs.append(pl.BlockSpec((1, 2, hb, G_DK, G_DV), lambda s, h: (s, 0, h, 0, 0)))
        args.append(s0)
    out_shape = [jax.ShapeDtypeStruct((N_TOK, BRANCH_W), BF16)]
    out_specs = [pl.BlockSpec((seq_len, wl), lambda s, h: (blk0 + s, h))]
    if emit_state:
        out_shape.append(jax.ShapeDtypeStruct((n_seq, DEPTH, 2, G_HEADS, G_DK, G_DV), F32))
        out_specs.append(pl.BlockSpec((1, None, 2, hb, G_DK, G_DV), lambda s, h: (s, li, 0, h, 0, 0)))
    nc = seq_len // CHUNK
    res = _branch_call(
        functools.partial(_gdn_kernel, seq_len=seq_len, has_init=has_init, emit_state=emit_state),
        [dst, dst_state][:len(out_shape)], in_specs, args, out_shape, out_specs,
        grid=(n_seq, nblk),
        scratch_shapes=[pltpu.VMEM((2 * hb, seq_len, 128), F32)] * 4
                       + [pltpu.VMEM((2 * hb, seq_len, CHUNK), F32), pltpu.VMEM((2 * hb * nc, 8, 128), F32),
                          pltpu.VMEM((seq_len, wl), F32)],
        compiler_params=_cparams(("arbitrary", "arbitrary")),
        name="gdn",
    )
    return res if emit_state else (res[0], None)


def _rope_tables(seq_len, dim):
    n_rows = seq_len // GRID_W
    row = np.repeat(np.arange(n_rows), GRID_W).astype(np.float32)
    col = np.tile(np.arange(GRID_W), n_rows).astype(np.float32)
    nf = dim // 4
    inv = np.power(np.float32(ROPE_BASE), -np.arange(nf, dtype=np.float32) / np.float32(nf)).astype(np.float32)
    ang_r = row[:, None] * inv
    ang_c = col[:, None] * inv
    ang = np.concatenate([ang_r, ang_r, ang_c, ang_c], axis=-1)
    ang = np.tile(ang, (1, 128 // dim))
    return jnp.asarray(np.cos(ang), F32), jnp.asarray(np.sin(ang), F32)


def _rope(x, cos, sin, dim):
    quarter = dim // 4
    lane = lax.broadcasted_iota(jnp.int32, x.shape, 1)
    first = (lane & (2 * quarter - 1)) < quarter
    rot = jnp.where(first, -pltpu.roll(x, 128 - quarter, axis=1), pltpu.roll(x, quarter, axis=1))
    return x * cos + rot * sin


Q_TILE = 256


def _retention_kernel(*refs, seq_len, latent):
    it = iter(refs)
    q_ref, k_ref, v_ref, rg_ref, lg_ref, nw_ref = [next(it) for _ in range(6)]
    if latent:
        cos_ref, sin_ref, s0_ref = next(it), next(it), next(it)
    o_ref = next(it)
    so_ref = None if latent else next(it)
    L = seq_len
    heads = range(R_HEADS)
    qk_l = [slice(h * R_DK, (h + 1) * R_DK) for h in heads]
    v_l = [slice(h * R_DV, (h + 1) * R_DV) for h in heads]
    lgf, lgb, ks, k16, v16 = [], [], [], [], []
    for h in heads:
        lg = lg_ref[h]
        lg = jnp.minimum(lg, 0.0) - jnp.log(1.0 + jnp.exp(-jnp.abs(lg)))
        lgf.append(lg[0:1, :])
        lgb.append(lg[1:2, :])
        k = k_ref[:, qk_l[h]]
        if latent:
            k = _rope(k, cos_ref[...], sin_ref[...], R_DK)
        ks.append(k)
        k16.append(k.astype(BF16))
        v16.append(v_ref[:, v_l[h]].astype(BF16))
    for qt in range(L // Q_TILE):
        rows = slice(qt * Q_TILE, (qt + 1) * Q_TILE)
        i = lax.broadcasted_iota(jnp.int32, (Q_TILE, L), 0) + qt * Q_TILE
        j = lax.broadcasted_iota(jnp.int32, (Q_TILE, L), 1)
        dist = (i - j).astype(F32)
        pos = (lax.broadcasted_iota(jnp.int32, (Q_TILE, 1), 0) + qt * Q_TILE).astype(F32)
        q16 = []
        for h in heads:
            q = q_ref[rows, qk_l[h]]
            if latent:
                q = _rope(q, cos_ref[rows, :], sin_ref[rows, :], R_DK)
            q16.append((q * R_DK ** -0.5).astype(BF16))
        a = [_dot_nt(q16[h], k16[h]) for h in heads]
        if latent:
            sf = [_dot(q16[h], s0_ref[0, 0, h].astype(BF16)) for h in heads]
            sb = [_dot(q16[h], s0_ref[0, 1, h].astype(BF16)) for h in heads]
        w16 = [(a[h] * (jnp.exp(jnp.where(dist >= 0, lgf[h] * dist, NEG_INF))
                        + jnp.exp(jnp.where(dist <= 0, -lgb[h] * dist, NEG_INF)))).astype(BF16) for h in heads]
        o = [_dot(w16[h], v16[h]) for h in heads]
        for h in heads:
            oh = o[h]
            if latent:
                oh = oh + sf[h] * jnp.exp(lgf[h] * (pos + 1.0)) + sb[h] * jnp.exp(lgb[h] * (L - pos))
            y = oh * lax.rsqrt(jnp.mean(oh * oh, axis=-1, keepdims=True) + EPS) * nw_ref[...]
            g = rg_ref[rows, v_l[h]]
            o_ref[rows, v_l[h]] = (y * (g * jax.nn.sigmoid(g))).astype(o_ref.dtype)
    if not latent:
        pos = lax.broadcasted_iota(jnp.int32, (L, 1), 0).astype(F32)
        for h in heads:
            so_ref[0, 0, h] = _dot_tn((ks[h] * jnp.exp(lgf[h] * (L - 1.0 - pos))).astype(BF16), v16[h])
            so_ref[0, 1, h] = _dot_tn((ks[h] * jnp.exp(lgb[h] * pos)).astype(BF16), v16[h])


def _retention(proj, decay, nw, seq_len, n_seq, tok_off, li, s0=None, dst=None, dst_state=None):
    latent = s0 is not None
    blk0 = tok_off // seq_len
    wqk, wv = R_HEADS * R_DK, R_HEADS * R_DV
    in_specs = [pl.BlockSpec((seq_len, wqk), lambda s: (blk0 + s, C_RQ // wqk)),
                pl.BlockSpec((seq_len, wqk), lambda s: (blk0 + s, C_RK // wqk)),
                pl.BlockSpec((seq_len, wv), lambda s: (blk0 + s, C_RV // wv)),
                pl.BlockSpec((seq_len, wv), lambda s: (blk0 + s, C_RG // wv)),
                pl.BlockSpec((R_HEADS, 2, 1), lambda s: (0, 0, 0)),
                pl.BlockSpec((1, R_DV), lambda s: (0, 0))]
    args = [proj, proj, proj, proj, decay.T.reshape(R_HEADS, 2, 1), nw.reshape(1, R_DV)]
    out_shape = [jax.ShapeDtypeStruct((N_TOK, BRANCH_W), BF16)]
    out_specs = [pl.BlockSpec((seq_len, wv), lambda s: (blk0 + s, 0))]
    if latent:
        cos, sin = _rope_tables(seq_len, R_DK)
        in_specs += [pl.BlockSpec((seq_len, 128), lambda s: (0, 0)),
                     pl.BlockSpec((seq_len, 128), lambda s: (0, 0)),
                     pl.BlockSpec((1, 2, R_HEADS, R_DK, R_DV), lambda s: (s, 0, 0, 0, 0))]
        args += [cos, sin, s0]
    else:
        out_shape.append(jax.ShapeDtypeStruct((n_seq, DEPTH, 2, R_HEADS, R_DK, R_DV), F32))
        out_specs.append(pl.BlockSpec((1, None, 2, R_HEADS, R_DK, R_DV), lambda s: (s, li, 0, 0, 0, 0)))
    res = _branch_call(
        functools.partial(_retention_kernel, seq_len=seq_len, latent=latent),
        [dst, dst_state][:len(out_shape)], in_specs, args, out_shape, out_specs,
        grid=(n_seq,),
        compiler_params=_cparams(("arbitrary",)),
        name="retention",
    )
    return (res[0], None) if latent else res


DA_HB = 4


def _diff_attn_kernel(*refs, seq_len, latent, lam_init):
    it = iter(refs)
    q_ref, k_ref, v_ref, lp_ref, nw_ref = [next(it) for _ in range(5)]
    if latent:
        cos_ref, sin_ref, ck_ref, cv_ref = [next(it) for _ in range(4)]
    o_ref = next(it)
    L = seq_len
    lp = lp_ref[...]
    lam = (jnp.exp(jnp.sum(lp[0:1] * lp[1:2], axis=-1, keepdims=True))
           - jnp.exp(jnp.sum(lp[2:3] * lp[3:4], axis=-1, keepdims=True)) + lam_init)
    w = 2 * D_HD
    lanes = [slice(hh * w, (hh + 1) * w) for hh in range(DA_HB)]
    keys, vals = [], []
    for hh in range(DA_HB):
        k = k_ref[:, lanes[hh]]
        if latent:
            k = _rope(k, cos_ref[...], sin_ref[...], D_HD)
        keys.append([k.astype(BF16)] + ([ck_ref[0, 0, :, lanes[hh]].astype(BF16)] if latent else []))
        vals.append([v_ref[:, lanes[hh]].astype(BF16)] + ([cv_ref[0, 0, :, lanes[hh]].astype(BF16)] if latent else []))
    lane = lax.broadcasted_iota(jnp.int32, (Q_TILE, w), 1)
    for qt in range(L // Q_TILE):
        rows = slice(qt * Q_TILE, (qt + 1) * Q_TILE)
        hp = [(hh, p) for hh in range(DA_HB) for p in range(2)]
        qs = []
        for hh in range(DA_HB):
            q = q_ref[rows, lanes[hh]]
            if latent:
                q = _rope(q, cos_ref[rows, :], sin_ref[rows, :], D_HD)
            qs.append(q * D_HD ** -0.5)
        s = [[_dot_nt(jnp.where((lane >= D_HD) == (p == 1), qs[hh], 0.0).astype(BF16), kk) for kk in keys[hh]]
             for hh, p in hp]
        outs = []
        for n, (hh, p) in enumerate(hp):
            m = s[n][0].max(axis=-1, keepdims=True)
            for x in s[n][1:]:
                m = jnp.maximum(m, x.max(axis=-1, keepdims=True))
            e = [jnp.exp(x - m) for x in s[n]]
            z = sum(x.sum(axis=-1, keepdims=True) for x in e)
            pv = sum(_dot(x.astype(BF16), vv) for x, vv in zip(e, vals[hh]))
            outs.append(pv / z)
        for hh in range(DA_HB):
            o = outs[2 * hh] - lam * outs[2 * hh + 1]
            y = o * lax.rsqrt(jnp.mean(o * o, axis=-1, keepdims=True) + EPS) * nw_ref[...]
            o_ref[rows, lanes[hh]] = (y * (1.0 - lam_init)).astype(o_ref.dtype)


def _diff_attn(proj, lam_p, nw, li, seq_len, n_seq, tok_off, cache_k=None, cache_v=None, dst=None):
    latent = cache_k is not None
    blk0 = tok_off // seq_len
    w = 2 * D_HD
    wl = DA_HB * w
    in_specs = [pl.BlockSpec((seq_len, wl), lambda s, h: (blk0 + s, C_DQ // wl + h)),
                pl.BlockSpec((seq_len, wl), lambda s, h: (blk0 + s, C_DK // wl + h)),
                pl.BlockSpec((seq_len, wl), lambda s, h: (blk0 + s, C_DV // wl + h)),
                pl.BlockSpec((4, D_HD), lambda s, h: (0, 0)),
                pl.BlockSpec((1, w), lambda s, h: (0, 0))]
    args = [proj, proj, proj, lam_p, nw.reshape(1, w)]
    if latent:
        cos, sin = _rope_tables(seq_len, D_HD)
        in_specs += [pl.BlockSpec((seq_len, 128), lambda s, h: (0, 0)),
                     pl.BlockSpec((seq_len, 128), lambda s, h: (0, 0)),
                     pl.BlockSpec((1, 1, PAST_LEN, wl), lambda s, h: (s, li, 0, h)),
                     pl.BlockSpec((1, 1, PAST_LEN, wl), lambda s, h: (s, li, 0, h))]
        args += [cos, sin, cache_k.reshape(DEC_BATCH, DEPTH, PAST_LEN, D_HEADS * w),
                 cache_v.reshape(DEC_BATCH, DEPTH, PAST_LEN, D_HEADS * w)]
    lam_init = 0.8 - 0.6 * math.exp(-0.3 * li)
    return _branch_call(
        functools.partial(_diff_attn_kernel, seq_len=seq_len, latent=latent, lam_init=lam_init),
        [dst], in_specs, args,
        [jax.ShapeDtypeStruct((N_TOK, BRANCH_W), BF16)],
        [pl.BlockSpec((seq_len, wl), lambda s, h: (blk0 + s, h))],
        grid=(n_seq, D_HEADS // DA_HB),
        compiler_params=_cparams(("arbitrary", "arbitrary")),
        name="diff_attn",
    )[0]


def _per_block(rows):
    idx = np.concatenate([np.zeros(N_CTX // MOD_BLK, np.int32)] +
                         [np.full(DEC_SEQ // MOD_BLK, 1 + b, np.int32) for b in range(DEC_BATCH)])
    return rows[idx][:, None, :]


def kernel(x_prompt, x_sample, cache_diff_k, cache_diff_v, state_ret, state_gdn, c, c_ctx,
           norm1_w, norm2_w, w_mod, b_mod, w_in, ret_decay, ret_norm, diff_lambda, diff_norm,
           gdn_conv, gdn_A_log, gdn_dt_bias, gdn_norm, w_branch, w_out,
           peer_wq, peer_keys, peer_u, peer_v, norm_f_w):
    x = jnp.concatenate([x_prompt.reshape(N_CTX, D_MODEL), x_sample.reshape(N_LAT, D_MODEL)], axis=0)
    cond = jnp.concatenate([c_ctx[None, :], c, jnp.zeros((8 - N_COND, D_MODEL), F32)], axis=0)
    mod_all = _modulation(cond, w_mod, b_mod)

    w_main = jnp.concatenate([w_in[:, :, :AB_OFF], w_in[:, :, AB_OFF + 32:]], axis=-1).astype(BF16)
    gate_cols = _gdn_gate_cols()
    w_ab = w_in[:, :, AB_OFF:AB_OFF + 32][:, :, gate_cols]
    w_ab = jnp.pad(w_ab, ((0, 0), (0, 0), (0, 0), (0, 128 - gate_cols.shape[1])))
    w_ab = w_ab.reshape(DEPTH, D_MODEL, gate_cols.shape[0] * 128).astype(BF16)
    wb_bf = w_branch.astype(BF16)
    wo_bf = w_out.astype(BF16)
    wq_bf = peer_wq.astype(BF16)
    u_bf = peer_u.astype(BF16)
    v_bf = peer_v.astype(BF16)

    new_k = new_v = new_ret = new_gdn = None
    for li in range(DEPTH):
        mod = mod_all[li, :N_COND]
        sh1, sc1, g1, sh2, sc2, g2 = [_per_block(m) for m in jnp.split(mod, 6, axis=-1)]
        h = _normmod(x, norm1_w[li], sc1, sh1, BF16)
        proj = _matmul(h, w_main, li, 2048, 1024)
        gab = _matmul(h, w_ab, li, 512, w_ab.shape[2])

        o_ret, new_ret = _retention(proj, ret_decay[li], ret_norm[li], SEQ, BATCH, 0, li, dst_state=new_ret)
        o_ret, _ = _retention(proj, ret_decay[li], ret_norm[li], DEC_SEQ, DEC_BATCH, N_CTX, li,
                              s0=state_ret[:, li], dst=o_ret)
        o_diff = _diff_attn(proj, diff_lambda[li], diff_norm[li], li, SEQ, BATCH, 0)
        o_diff = _diff_attn(proj, diff_lambda[li], diff_norm[li], li, DEC_SEQ, DEC_BATCH, N_CTX,
                            cache_diff_k, cache_diff_v, dst=o_diff)
        new_k, new_v = _kv_out(proj, li, new_k, new_v)

        alog = gdn_A_log[li].T.reshape(2 * G_HEADS, 1, 1)
        dtb = gdn_dt_bias[li].T.reshape(2 * G_HEADS, 1, 1)
        qkv_c = _gdn_prep(proj, gdn_conv[li], SEQ, BATCH, 0)
        o_gdn, new_gdn = _gdn(qkv_c, proj, gab, alog, dtb, gdn_norm[li], None, SEQ, BATCH, 0, li, dst_state=new_gdn)
        qkv_z = _gdn_prep(proj, gdn_conv[li], DEC_SEQ, DEC_BATCH, N_CTX)
        o_gdn, _ = _gdn(qkv_z, proj, gab, alog, dtb, gdn_norm[li], state_gdn[:, li], DEC_SEQ, DEC_BATCH, N_CTX, li,
                        dst=o_gdn)

        x, h2 = _merge(o_ret, o_diff, o_gdn, proj, wb_bf, wo_bf, li, x, g1, norm2_w[li], sc2, sh2)
        idx, gate = _peer_route(h2, wq_bf, li, peer_keys[li])
        x = _peer_experts(h2, idx, gate, u_bf, v_bf, li, x, g2)

    zeros_blk = jnp.zeros((N_MOD_BLK, 1, D_MODEL), F32)
    y = _normmod(x, norm_f_w, zeros_blk, zeros_blk, F32)
    y_prompt = y[:N_CTX].reshape(BATCH, SEQ, D_MODEL)
    y_sample = y[N_CTX:].reshape(DEC_BATCH, DEC_SEQ, D_MODEL)
    return (y_prompt, y_sample, new_k.reshape(BATCH, DEPTH, SEQ, D_HEADS, 2, D_HD),
            new_v.reshape(BATCH, DEPTH, SEQ, D_HEADS, 2 * D_HD), new_ret, new_gdn)
```

```python
import functools
import math

import numpy as np
import jax
import jax.numpy as jnp
from jax import lax
from jax.experimental import pallas as pl
from jax.experimental.pallas import tpu as pltpu

F32 = jnp.float32
BF16 = jnp.bfloat16

D_MODEL = 1024
BATCH = 16
SEQ = 256
DEPTH = 2
DEC_BATCH = 2
DEC_SEQ = 1024
PAST_LEN = 256
GRID_W = 64
ROPE_BASE = 10000.0
EPS = 1e-6
CHUNK = 64
R_HEADS, R_DK, R_DV = 4, 128, 256
D_HEADS, D_HD = 8, 64
G_HEADS, G_DK, G_DV = 8, 128, 128
CONV_K = 7
G_QKV = G_HEADS * (2 * G_DK + G_DV)
N_BRANCH = 3
BRANCH_W = 1024
PEER_HEADS = 8
PEER_DQ = 256
N_KEYS = 128
PEER_TOPK = 16
TOK_BLOCK = 128

N_CTX = BATCH * SEQ
N_LAT = DEC_BATCH * DEC_SEQ
N_TOK = N_CTX + N_LAT
N_COND = 1 + DEC_BATCH
MOD_BLK = 256
N_MOD_BLK = N_TOK // MOD_BLK

C_RQ, C_RK, C_RV, C_RG = 0, 512, 1024, 2048
C_DQ, C_DK, C_DV = 3072, 4096, 5120
C_GQKV, C_GZ, C_MG = 6144, 9216, 10240
N_MAIN = 13312
AB_OFF = 10240

VMEM_LIMIT = 56 * 1024 * 1024


def _cparams(sem):
    return pltpu.CompilerParams(dimension_semantics=sem, vmem_limit_bytes=VMEM_LIMIT)


def _mod_kernel(c_ref, w_ref, b_ref, o_ref):
    c = c_ref[...]
    a = c * jax.nn.sigmoid(c)
    o_ref[0] = jnp.dot(a, w_ref[0], preferred_element_type=F32,
                       precision=lax.Precision.HIGHEST) + b_ref[0]


def _modulation(cond_pad, w_mod, b_mod):
    tn = 1536
    return pl.pallas_call(
        _mod_kernel,
        out_shape=jax.ShapeDtypeStruct((DEPTH, 8, 6 * D_MODEL), F32),
        grid=(DEPTH, 6 * D_MODEL // tn),
        in_specs=[pl.BlockSpec((8, D_MODEL), lambda l, j: (0, 0)),
                  pl.BlockSpec((1, D_MODEL, tn), lambda l, j: (l, 0, j)),
                  pl.BlockSpec((1, 1, tn), lambda l, j: (l, 0, j))],
        out_specs=pl.BlockSpec((1, 8, tn), lambda l, j: (l, 0, j)),
        compiler_params=_cparams(("arbitrary", "arbitrary")),
        name="modulation",
    )(cond_pad, w_mod, b_mod.reshape(DEPTH, 1, 6 * D_MODEL))


def _normmod_kernel(x_ref, w_ref, sc_ref, sh_ref, o_ref):
    x = x_ref[...]
    y = x * lax.rsqrt(jnp.mean(x * x, axis=-1, keepdims=True) + EPS)
    y = y * w_ref[...]
    o_ref[...] = (y * (1.0 + sc_ref[0]) + sh_ref[0]).astype(o_ref.dtype)


def _normmod(x, w, sc_blk, sh_blk, out_dtype):
    tm = MOD_BLK
    return pl.pallas_call(
        _normmod_kernel,
        out_shape=jax.ShapeDtypeStruct((N_TOK, D_MODEL), out_dtype),
        grid=(N_TOK // tm,),
        in_specs=[pl.BlockSpec((tm, D_MODEL), lambda i: (i, 0)),
                  pl.BlockSpec((1, D_MODEL), lambda i: (0, 0)),
                  pl.BlockSpec((1, 1, D_MODEL), lambda i: (i, 0, 0)),
                  pl.BlockSpec((1, 1, D_MODEL), lambda i: (i, 0, 0))],
        out_specs=pl.BlockSpec((tm, D_MODEL), lambda i: (i, 0)),
        compiler_params=_cparams(("arbitrary",)),
        name="normmod",
    )(x, w.reshape(1, D_MODEL), sc_blk, sh_blk)


def _matmul_kernel(a_ref, b_ref, o_ref):
    o_ref[...] = jnp.dot(a_ref[...], b_ref[...], preferred_element_type=F32)


def _matmul(a, b, li, tm, tn):
    m, k = a.shape
    n = b.shape[2]
    assert m % tm == 0 and n % tn == 0, (m, n, tm, tn)
    return pl.pallas_call(
        _matmul_kernel,
        out_shape=jax.ShapeDtypeStruct((m, n), F32),
        grid=(n // tn, m // tm),
        in_specs=[pl.BlockSpec((tm, k), lambda j, i: (i, 0)),
                  pl.BlockSpec((None, k, tn), lambda j, i: (li, 0, j))],
        out_specs=pl.BlockSpec((tm, tn), lambda j, i: (i, j)),
        compiler_params=_cparams(("arbitrary", "arbitrary")),
        name="matmul",
    )(a, b)


def _merge_kernel(oa_ref, ob_ref, oc_ref, mga_ref, mgb_ref, mgc_ref, wb_ref, wo_ref, x_ref, g1_ref,
                  n2_ref, sc_ref, sh_ref, xo_ref, h_ref):
    mix = None
    for n, (o_ref, mg_ref) in enumerate(((oa_ref, mga_ref), (ob_ref, mgb_ref), (oc_ref, mgc_ref))):
        merged = jnp.dot(o_ref[...], wb_ref[n], preferred_element_type=F32)
        gate = jax.nn.sigmoid(mg_ref[...])
        mix = gate * merged if mix is None else mix + gate * merged
    m = jnp.dot(mix.astype(BF16), wo_ref[...], preferred_element_type=F32)
    x = x_ref[...] + g1_ref[0] * m
    xo_ref[...] = x
    y = x * lax.rsqrt(jnp.mean(x * x, axis=-1, keepdims=True) + EPS)
    y = y * n2_ref[...]
    h_ref[...] = (y * (1.0 + sc_ref[0]) + sh_ref[0]).astype(h_ref.dtype)


def _merge(o_ret, o_diff, o_gdn, proj, wb, wo, li, x, g1_blk, n2w, sc2_blk, sh2_blk):
    tm = MOD_BLK
    tok = lambda i: (i, 0)
    blk = lambda i: (i, 0, 0)
    return pl.pallas_call(
        _merge_kernel,
        out_shape=(jax.ShapeDtypeStruct((N_TOK, D_MODEL), F32),
                   jax.ShapeDtypeStruct((N_TOK, D_MODEL), BF16)),
        grid=(N_TOK // tm,),
        in_specs=[pl.BlockSpec((tm, BRANCH_W), tok),
                  pl.BlockSpec((tm, BRANCH_W), tok),
                  pl.BlockSpec((tm, BRANCH_W), tok),
                  pl.BlockSpec((tm, D_MODEL), lambda i: (i, C_MG // D_MODEL)),
                  pl.BlockSpec((tm, D_MODEL), lambda i: (i, C_MG // D_MODEL + 1)),
                  pl.BlockSpec((tm, D_MODEL), lambda i: (i, C_MG // D_MODEL + 2)),
                  pl.BlockSpec((None, N_BRANCH, BRANCH_W, D_MODEL), lambda i: (li, 0, 0, 0)),
                  pl.BlockSpec((None, D_MODEL, D_MODEL), lambda i: (li, 0, 0)),
                  pl.BlockSpec((tm, D_MODEL), tok),
                  pl.BlockSpec((1, 1, D_MODEL), blk),
                  pl.BlockSpec((1, D_MODEL), lambda i: (0, 0)),
                  pl.BlockSpec((1, 1, D_MODEL), blk),
                  pl.BlockSpec((1, 1, D_MODEL), blk)],
        out_specs=(pl.BlockSpec((tm, D_MODEL), tok), pl.BlockSpec((tm, D_MODEL), tok)),
        compiler_params=_cparams(("arbitrary",)),
        name="merge",
    )(o_ret, o_diff, o_gdn, proj, proj, proj, wb, wo, x, g1_blk, n2w.reshape(1, D_MODEL), sc2_blk, sh2_blk)


ROUTE_T = 128
ROUTE_HEADS = 8
N_PAIR = PEER_HEADS * PEER_TOPK
N_CAND = PEER_TOPK + (PEER_TOPK // 2) * (PEER_TOPK // 2 - 1) + PEER_TOPK // 2
NEG_INF = float("-inf")


def _split_bf16(x):
    hi = x.astype(BF16)
    return hi, (x - hi.astype(F32)).astype(BF16)


def _topk_rows(scores, rows, n_rows, payloads=None):
    scores = list(scores)
    vals = [[] for _ in scores]
    outs = [[] for _ in scores]
    for _ in range(PEER_TOPK):
        for n, s in enumerate(scores):
            m = jnp.max(s, axis=0, keepdims=True)
            pos = jnp.min(jnp.where(s == m, rows, n_rows), axis=0, keepdims=True)
            sel = rows == pos
            vals[n].append(m)
            if payloads is None:
                outs[n].append(pos)
            else:
                outs[n].append(jnp.sum(jnp.where(sel, payloads[n], 0), axis=0, keepdims=True))
            scores[n] = jnp.where(sel, NEG_INF, s)
    return [(jnp.concatenate(v, axis=0), jnp.concatenate(o, axis=0)) for v, o in zip(vals, outs)]


def _route_kernel(h_ref, wq_ref, khi_ref, klo_ref, idx_ref, gate_ref, q_scr, idx_scr, gate_scr):
    t = h_ref.shape[0]
    q = jnp.dot(h_ref[...], wq_ref[...], preferred_element_type=F32)
    for hp in range(2 * PEER_HEADS):
        q_scr[hp] = q[:, hp * 128:(hp + 1) * 128]
    rows_k = lax.broadcasted_iota(jnp.int32, (N_KEYS, t), 0)
    rows_c = lax.broadcasted_iota(jnp.int32, (N_CAND, t), 0)
    nt = (((1,), (1,)), ((), ()))

    half = PEER_TOPK // 2
    groups = [(slice(0, 1), slice(0, PEER_TOPK)), (slice(1, 2), slice(0, half))]
    groups += [(slice(k, k + 1), slice(0, half)) for k in range(2, half)]
    groups += [(slice(half, PEER_TOPK), slice(0, 1))]

    def heads(hh, carry):
        hs = [ROUTE_HEADS * hh + n for n in range(ROUTE_HEADS)]
        scores = []
        for h in hs:
            for p in range(2):
                qhi, qlo = _split_bf16(q_scr[2 * h + p])
                scores.append(lax.dot_general(khi_ref[p], qhi, nt, preferred_element_type=F32)
                              + lax.dot_general(khi_ref[p], qlo, nt, preferred_element_type=F32)
                              + lax.dot_general(klo_ref[p], qhi, nt, preferred_element_type=F32))
        tops = _topk_rows(scores, rows_k, N_KEYS)
        cands, cand_is = [], []
        for n in range(ROUTE_HEADS):
            (v1, i1), (v2, i2) = tops[2 * n], tops[2 * n + 1]
            cands.append(jnp.concatenate([v1[a] + v2[b] for a, b in groups], axis=0))
            cand_is.append(jnp.concatenate([i1[a] * N_KEYS + i2[b] for a, b in groups], axis=0))
        for h, (best, bidx) in zip(hs, _topk_rows(cands, rows_c, N_CAND, payloads=cand_is)):
            e = jnp.exp(best - best[0:1])
            gate_scr[h] = e / jnp.sum(e, axis=0, keepdims=True)
            idx_scr[h] = bidx
        return carry

    lax.fori_loop(0, PEER_HEADS // ROUTE_HEADS, heads, 0)
    idx_ref[...] = idx_scr[...].reshape(N_PAIR, t).T
    gate_ref[...] = gate_scr[...].reshape(N_PAIR, t).T


def _peer_route(h2, wq, li, keys):
    khi, klo = _split_bf16(keys)
    t = ROUTE_T
    n_tok = h2.shape[0]
    return pl.pallas_call(
        _route_kernel,
        out_shape=(jax.ShapeDtypeStruct((n_tok, N_PAIR), jnp.int32),
                   jax.ShapeDtypeStruct((n_tok, N_PAIR), F32)),
        grid=(n_tok // t,),
        in_specs=[pl.BlockSpec((t, D_MODEL), lambda i: (i, 0)),
                  pl.BlockSpec((None, D_MODEL, PEER_HEADS * PEER_DQ), lambda i: (li, 0, 0)),
                  pl.BlockSpec((2, N_KEYS, PEER_DQ // 2), lambda i: (0, 0, 0)),
                  pl.BlockSpec((2, N_KEYS, PEER_DQ // 2), lambda i: (0, 0, 0))],
        out_specs=(pl.BlockSpec((t, N_PAIR), lambda i: (i, 0)),
                   pl.BlockSpec((t, N_PAIR), lambda i: (i, 0))),
        scratch_shapes=[pltpu.VMEM((2 * PEER_HEADS, t, PEER_DQ // 2), F32),
                        pltpu.VMEM((PEER_HEADS, PEER_TOPK, t), jnp.int32),
                        pltpu.VMEM((PEER_HEADS, PEER_TOPK, t), F32)],
        compiler_params=_cparams(("arbitrary",)),
        name="peer_route",
    )(h2, wq, khi, klo)


EXP_TB = 256
EXP_EB = 2048
EXP_SUB = 256
EXP_AHEAD = 2
N_EXPERTS = N_KEYS * N_KEYS


def _expert_kernel(h_ref, idx_ref, gate_ref, u_ref, v_ref, x_ref, g2_ref, o_ref, g_scr, acc_ref):
    j = pl.program_id(1)
    tb = h_ref.shape[0]

    @pl.when(j == 0)
    def _():
        acc_ref[...] = jnp.zeros_like(acc_ref)
        sub = lax.broadcasted_iota(jnp.int32, (N_KEYS, N_PAIR), 0)

        def body(t, carry):
            e = idx_ref[pl.ds(t, 1), :]
            g = gate_ref[pl.ds(t, 1), :]
            xa = jnp.where(sub == (e >> 7), 1.0, 0.0).astype(BF16)
            yb = jnp.where(sub == (e & (N_KEYS - 1)), g, 0.0).astype(BF16)
            g_scr[pl.ds(pl.multiple_of(t * N_KEYS, N_KEYS), N_KEYS), :] = _dot_nt(xa, yb)
            return carry

        lax.fori_loop(0, tb, body, 0, unroll=32)

    h = h_ref[...]
    nsub = EXP_EB // EXP_SUB
    per = EXP_SUB // N_KEYS
    score = lambda c: _dot_nt(h, u_ref[c * EXP_SUB:(c + 1) * EXP_SUB, :])
    s = [score(c) for c in range(EXP_AHEAD)]
    acc = acc_ref[...]
    for c in range(nsub):
        gj = jnp.concatenate([g_scr[pl.ds(j * (nsub * per) + c * per + i, tb, stride=N_KEYS), :]
                              for i in range(per)], axis=1)
        w = 0.5 * s[c] * (1.0 + lax.erf(s[c] * (2.0 ** -0.5))) * gj
        acc = acc + jnp.dot(w.astype(BF16), v_ref[c * EXP_SUB:(c + 1) * EXP_SUB, :], preferred_element_type=F32)
        if c + EXP_AHEAD < nsub:
            s.append(score(c + EXP_AHEAD))
    acc_ref[...] = acc

    @pl.when(j == pl.num_programs(1) - 1)
    def _():
        o_ref[...] = x_ref[...] + g2_ref[0] * acc_ref[...]


def _peer_experts(h2, idx, gate, u, v, li, x, g2_blk):
    tb, eb = EXP_TB, EXP_EB
    n_tok = h2.shape[0]
    return pl.pallas_call(
        _expert_kernel,
        out_shape=jax.ShapeDtypeStruct((n_tok, D_MODEL), F32),
        grid=(n_tok // tb, N_EXPERTS // eb),
        in_specs=[pl.BlockSpec((tb, D_MODEL), lambda i, j: (i, 0)),
                  pl.BlockSpec((tb, N_PAIR), lambda i, j: (i, 0)),
                  pl.BlockSpec((tb, N_PAIR), lambda i, j: (i, 0)),
                  pl.BlockSpec((None, eb, D_MODEL), lambda i, j: (li, j, 0)),
                  pl.BlockSpec((None, eb, D_MODEL), lambda i, j: (li, j, 0)),
                  pl.BlockSpec((tb, D_MODEL), lambda i, j: (i, 0)),
                  pl.BlockSpec((1, 1, D_MODEL), lambda i, j: (i * tb // MOD_BLK, 0, 0))],
        out_specs=pl.BlockSpec((tb, D_MODEL), lambda i, j: (i, 0)),
        scratch_shapes=[pltpu.VMEM((tb * N_KEYS, N_KEYS), F32),
                        pltpu.VMEM((tb, D_MODEL), F32)],
        compiler_params=_cparams(("arbitrary", "arbitrary")),
        name="peer_experts",
    )(h2, idx, gate, u, v, x, g2_blk)


def _drop_refs(fn, pos, n, *refs):
    return fn(*refs[:pos], *refs[pos + n:])


def _branch_call(kernel_fn, dsts, in_specs, args, out_shape, out_specs, **kw):
    given = [(n, d) for n, d in enumerate(dsts) if d is not None]
    if given:
        pos = len(in_specs)
        in_specs = list(in_specs) + [pl.BlockSpec(memory_space=pl.ANY)] * len(given)
        args = list(args) + [d for _, d in given]
        kernel_fn = functools.partial(_drop_refs, kernel_fn, pos, len(given))
        kw["input_output_aliases"] = {pos + m: n for m, (n, _) in enumerate(given)}
    return pl.pallas_call(kernel_fn, out_shape=tuple(out_shape), in_specs=in_specs, out_specs=tuple(out_specs),
                          **kw)(*args)


def _kv_out_kernel(k_ref, v_ref, ko_ref, vo_ref):
    ko_ref[0] = k_ref[...]
    vo_ref[0] = v_ref[...]


def _kv_out(proj, li, dst_k, dst_v):
    shape = jax.ShapeDtypeStruct((BATCH, DEPTH, SEQ, D_HEADS * 2 * D_HD), F32)
    blk = D_HEADS * 2 * D_HD
    return _branch_call(
        _kv_out_kernel, [dst_k, dst_v],
        [pl.BlockSpec((SEQ, blk), lambda s: (s, C_DK // blk)), pl.BlockSpec((SEQ, blk), lambda s: (s, C_DV // blk))],
        [proj, proj], [shape, shape],
        [pl.BlockSpec((1, None, SEQ, blk), lambda s: (s, li, 0, 0))] * 2,
        grid=(BATCH,),
        compiler_params=_cparams(("arbitrary",)),
        name="kv_out",
    )


CONV_ROWS = 256
CONV_HALO = 8


def _gdn_prep_kernel(x_ref, w_ref, o_ref):
    sec = pl.program_id(1)
    L = x_ref.shape[0]
    half = CONV_K // 2
    n = CONV_ROWS + 2 * CONV_HALO
    zeros = jnp.zeros((CONV_HALO, 128), F32)
    for r0 in range(0, L, CONV_ROWS):
        for cb in range(G_HEADS):
            cols = slice(cb * 128, (cb + 1) * 128)
            top = x_ref[r0 - CONV_HALO:r0, cols] if r0 > 0 else zeros
            bot = x_ref[r0 + CONV_ROWS:r0 + CONV_ROWS + CONV_HALO, cols] if r0 + CONV_ROWS < L else zeros
            x = jnp.concatenate([top, x_ref[r0:r0 + CONV_ROWS, cols], bot], axis=0)
            w = w_ref[0, :, cols]
            acc = x * w[half:half + 1]
            for d in range(-half, half + 1):
                if d != 0:
                    acc = acc + pltpu.roll(x, (-d) % n, axis=0) * w[half + d:half + d + 1]
            y = acc[CONV_HALO:CONV_HALO + CONV_ROWS]
            y = y * jax.nn.sigmoid(y)
            inv = lax.rsqrt(jnp.sum(y * y, axis=-1, keepdims=True) + EPS)
            o_ref[r0:r0 + CONV_ROWS, cols] = y * jnp.where(sec < 2, inv, 1.0)


def _gdn_prep(proj, conv_w, seq_len, n_seq, tok_off):
    blk0 = tok_off // seq_len
    sec0 = C_GQKV // 1024
    return pl.pallas_call(
        _gdn_prep_kernel,
        out_shape=jax.ShapeDtypeStruct((n_seq * seq_len, G_QKV), F32),
        grid=(n_seq, 3),
        in_specs=[pl.BlockSpec((seq_len, 1024), lambda s, c: (blk0 + s, sec0 + c)),
                  pl.BlockSpec((1, CONV_K, 1024), lambda s, c: (0, 0, c))],
        out_specs=pl.BlockSpec((seq_len, 1024), lambda s, c: (s, c)),
        compiler_params=_cparams(("arbitrary", "arbitrary")),
        name="gdn_prep",
    )(proj, conv_w.reshape(1, CONV_K, G_QKV))


GDN_HB = 4


def _dot(a, b):
    return jnp.dot(a, b, preferred_element_type=F32)


def _dot_nt(a, b):
    return lax.dot_general(a, b, (((1,), (1,)), ((), ())), preferred_element_type=F32)


def _dot_tn(a, b):
    return lax.dot_general(a, b, (((0,), (0,)), ((), ())), preferred_element_type=F32)


def _mask_dot3(m, b):
    m16 = m.astype(BF16)
    bh = b.astype(BF16)
    r1 = b - bh.astype(F32)
    bm = r1.astype(BF16)
    bl = (r1 - bm.astype(F32)).astype(BF16)
    return _dot(m16, bh) + _dot(m16, bm) + _dot(m16, bl)


def _gdn_kernel(*refs, seq_len, has_init, emit_state):
    it = iter(refs)
    q_ref, k_ref, v_ref, z_ref, gab_ref, alog_ref, dtb_ref, nw_ref = [next(it) for _ in range(8)]
    s0_ref = next(it) if has_init else None
    o_ref = next(it)
    so_ref = next(it) if emit_state else None
    u_s, w_s, qg_s, kd_s, qk_s, gl_s, o_s = [next(it) for _ in range(7)]
    C = CHUNK
    nc = seq_len // C
    ri = lax.broadcasted_iota(jnp.int32, (C, C), 0)
    ci = lax.broadcasted_iota(jnp.int32, (C, C), 1)
    eye = jnp.where(ri == ci, 1.0, 0.0)
    ones = jnp.ones((C, C), F32)
    scale = G_DK ** -0.5
    incl = (ri >= ci, ri <= ci)
    strict = (ri > ci, ri < ci)
    levels = []
    for lv in range(6):
        b = 1 << lv
        same = (ri >> (lv + 1)) == (ci >> (lv + 1))
        r_hi, c_hi = (ri & b) != 0, (ci & b) != 0
        levels.append((same & r_hi & jnp.logical_not(c_hi), same & jnp.logical_not(r_hi) & c_hi))

    GROUP = 4
    HB = GDN_HB
    lanes = [slice(hh * 128, (hh + 1) * 128) for hh in range(HB)]
    tri = [jnp.where(m, 1.0, 0.0) for m in incl]

    def prep(grp, carry):
        chunks = [grp * GROUP + j for j in range(GROUP)]
        rows = [pl.ds(pl.multiple_of(c * C, C), C) for c in chunks]
        hj = [(hh, j) for hh in range(HB) for j in range(GROUP)]
        q = {(hh, j): q_ref[rows[j], lanes[hh]] * scale for hh, j in hj}
        k = {(hh, j): k_ref[rows[j], lanes[hh]] for hh, j in hj}
        v = {(hh, j): v_ref[rows[j], lanes[hh]] for hh, j in hj}
        k16 = {key: x.astype(BF16) for key, x in k.items()}
        kk = {key: _dot_nt(x, x) for key, x in k16.items()}
        qk = {key: _dot_nt(q[key].astype(BF16), k16[key]) for key in hj}
        probs = [(hh, j, d) for hh in range(HB) for j in range(GROUP) for d in range(2)]
        gab = [gab_ref[r, :] for r in rows]
        ga = lambda hh, j, d: gab[j][:, 2 * hh + d:2 * hh + d + 1]
        gb = lambda hh, j, d: gab[j][:, 2 * HB + 2 * hh + d:2 * HB + 2 * hh + d + 1]
        g = [-jnp.exp(alog_ref[2 * hh + d]) * jax.nn.softplus(ga(hh, j, d) + dtb_ref[2 * hh + d])
             for hh, j, d in probs]
        beta = [jax.nn.sigmoid(gb(hh, j, d)) for hh, j, d in probs]
        gcol = [_mask_dot3(tri[d], jnp.broadcast_to(g[p], (C, 128))) for p, (hh, j, d) in enumerate(probs)]
        grow = [_mask_dot3(ones, jnp.where(incl[1 - d], jnp.broadcast_to(g[p], (C, C)), 0.0))
                for p, (hh, j, d) in enumerate(probs)]
        dec = [jnp.exp(jnp.where(incl[d], gcol[p][:, :C] - grow[p], NEG_INF)) for p, (hh, j, d) in enumerate(probs)]
        a = [jnp.where(strict[d], kk[hh, j] * beta[p] * dec[p], 0.0) for p, (hh, j, d) in enumerate(probs)]
        t = [eye - jnp.where(levels[0][d], a[p], 0.0) for p, (hh, j, d) in enumerate(probs)]
        for lv in range(1, 6):
            t16 = [x.astype(BF16) for x in t]
            ct = [_dot(jnp.where(levels[lv][d], a[p], 0.0).astype(BF16), t16[p])
                  for p, (hh, j, d) in enumerate(probs)]
            t = [t[p] - _dot(t16[p], ct[p].astype(BF16)) for p in range(len(probs))]
        for p, (hh, j, d) in enumerate(probs):
            n = 2 * hh + d
            eg = jnp.exp(gcol[p])
            t16 = t[p].astype(BF16)
            glast = gcol[p][C - 1:C, :] if d == 0 else gcol[p][0:1, :]
            u_s[n, rows[j], :] = _dot(t16, (v[hh, j] * beta[p]).astype(BF16))
            w_s[n, rows[j], :] = _dot(t16, (k[hh, j] * beta[p] * eg).astype(BF16))
            qg_s[n, rows[j], :] = q[hh, j] * eg
            kd_s[n, rows[j], :] = k[hh, j] * jnp.exp(glast - gcol[p])
            qk_s[n, rows[j], :] = qk[hh, j] * dec[p]
            gl_s[n * nc + chunks[j]] = jnp.broadcast_to(jnp.exp(glast), (8, 128))
        return carry

    lax.fori_loop(0, nc // GROUP, prep, 0)
    o_s[...] = jnp.zeros_like(o_s)
    chains = [(hh, d) for hh in range(HB) for d in range(2)]

    def step(i, states):
        cs = (i, nc - 1 - i)
        rows = [pl.ds(pl.multiple_of(c * C, C), C) for c in cs]
        s16 = [s.astype(BF16) for s in states]
        ws = [_dot(w_s[n, rows[d], :].astype(BF16), s16[n]) for n, (hh, d) in enumerate(chains)]
        un16 = [(u_s[n, rows[d], :] - ws[n]).astype(BF16) for n, (hh, d) in enumerate(chains)]
        o = [_dot(qg_s[n, rows[d], :].astype(BF16), s16[n]) + _dot(qk_s[n, rows[d], :].astype(BF16), un16[n])
             for n, (hh, d) in enumerate(chains)]
        new = tuple(states[n] * gl_s[n * nc + cs[d]][0:1, :] + _dot_tn(kd_s[n, rows[d], :].astype(BF16), un16[n])
                    for n, (hh, d) in enumerate(chains))
        for n, (hh, d) in enumerate(chains):
            o_s[rows[d], lanes[hh]] += o[n]
        return new

    if has_init:
        init = tuple(s0_ref[0, d, hh] for hh, d in chains)
    else:
        init = (jnp.zeros((G_DK, G_DV), F32),) * len(chains)
    final = lax.fori_loop(0, nc, step, init)
    if emit_state:
        for n, (hh, d) in enumerate(chains):
            so_ref[0, d, hh] = final[n]
    for hh in range(HB):
        o = o_s[:, lanes[hh]]
        y = o * lax.rsqrt(jnp.mean(o * o, axis=-1, keepdims=True) + EPS) * nw_ref[...]
        z = z_ref[:, lanes[hh]]
        o_ref[:, lanes[hh]] = (y * (z * jax.nn.sigmoid(z))).astype(o_ref.dtype)


def _gdn_gate_cols():
    cols = []
    for blk in range(G_HEADS // GDN_HB):
        heads = range(blk * GDN_HB, (blk + 1) * GDN_HB)
        cols.append([kind * 2 * G_HEADS + d * G_HEADS + h for kind in range(2) for h in heads for d in range(2)])
    return np.asarray(cols, np.int32)


def _gdn(qkv, proj, gab, alog, dtb, nw, s0, seq_len, n_seq, tok_off, li, dst=None, dst_state=None):
    has_init = s0 is not None
    emit_state = not has_init
    blk0 = tok_off // seq_len
    hb = GDN_HB
    wl = hb * 128
    nblk = G_HEADS // hb
    tokcol = lambda off: pl.BlockSpec((seq_len, wl), lambda s, h: (s, off + h))
    in_specs = [tokcol(0), tokcol(nblk), tokcol(2 * nblk),
                pl.BlockSpec((seq_len, wl), lambda s, h: (blk0 + s, C_GZ // wl + h)),
                pl.BlockSpec((seq_len, 128), lambda s, h: (blk0 + s, h)),
                pl.BlockSpec((2 * hb, 1, 1), lambda s, h: (h, 0, 0)),
                pl.BlockSpec((2 * hb, 1, 1), lambda s, h: (h, 0, 0)),
                pl.BlockSpec((1, G_DV), lambda s, h: (0, 0))]
    args = [qkv, qkv, qkv, proj, gab, alog, dtb, nw.reshape(1, G_DV)]
    if has_init:
        in_specs.append(pl.BlockSpec((1, 2, hb, G_DK, G_DV), lambda s, h: (s, 0, h, 0, 0)))
        args.append(s0)
    out_shape = [jax.ShapeDtypeStruct((N_TOK, BRANCH_W), BF16)]
    out_specs = [pl.BlockSpec((seq_len, wl), lambda s, h: (blk0 + s, h))]
    if emit_state:
        out_shape.append(jax.ShapeDtypeStruct((n_seq, DEPTH, 2, G_HEADS, G_DK, G_DV), F32))
        out_specs.append(pl.BlockSpec((1, None, 2, hb, G_DK, G_DV), lambda s, h: (s, li, 0, h, 0, 0)))
    nc = seq_len // CHUNK
    res = _branch_call(
        functools.partial(_gdn_kernel, seq_len=seq_len, has_init=has_init, emit_state=emit_state),
        [dst, dst_state][:len(out_shape)], in_specs, args, out_shape, out_specs,
        grid=(n_seq, nblk),
        scratch_shapes=[pltpu.VMEM((2 * hb, seq_len, 128), F32)] * 4
                       + [pltpu.VMEM((2 * hb, seq_len, CHUNK), F32), pltpu.VMEM((2 * hb * nc, 8, 128), F32),
                          pltpu.VMEM((seq_len, wl), F32)],
        compiler_params=_cparams(("arbitrary", "arbitrary")),
        name="gdn",
    )
    return res if emit_state else (res[0], None)


def _rope_tables(seq_len, dim):
    n_rows = seq_len // GRID_W
    row = np.repeat(np.arange(n_rows), GRID_W).astype(np.float32)
    col = np.tile(np.arange(GRID_W), n_rows).astype(np.float32)
    nf = dim // 4
    inv = np.power(np.float32(ROPE_BASE), -np.arange(nf, dtype=np.float32) / np.float32(nf)).astype(np.float32)
    ang_r = row[:, None] * inv
    ang_c = col[:, None] * inv
    ang = np.concatenate([ang_r, ang_r, ang_c, ang_c], axis=-1)
    ang = np.tile(ang, (1, 128 // dim))
    return jnp.asarray(np.cos(ang), F32), jnp.asarray(np.sin(ang), F32)


def _rope(x, cos, sin, dim):
    quarter = dim // 4
    lane = lax.broadcasted_iota(jnp.int32, x.shape, 1)
    first = (lane & (2 * quarter - 1)) < quarter
    rot = jnp.where(first, -pltpu.roll(x, 128 - quarter, axis=1), pltpu.roll(x, quarter, axis=1))
    return x * cos + rot * sin


Q_TILE = 256


def _retention_kernel(*refs, seq_len, latent):
    it = iter(refs)
    q_ref, k_ref, v_ref, rg_ref, lg_ref, nw_ref = [next(it) for _ in range(6)]
    if latent:
        cos_ref, sin_ref, s0_ref = next(it), next(it), next(it)
    o_ref = next(it)
    so_ref = None if latent else next(it)
    L = seq_len
    heads = range(R_HEADS)
    qk_l = [slice(h * R_DK, (h + 1) * R_DK) for h in heads]
    v_l = [slice(h * R_DV, (h + 1) * R_DV) for h in heads]
    lgf, lgb, ks, k16, v16 = [], [], [], [], []
    for h in heads:
        lg = lg_ref[h]
        lg = jnp.minimum(lg, 0.0) - jnp.log(1.0 + jnp.exp(-jnp.abs(lg)))
        lgf.append(lg[0:1, :])
        lgb.append(lg[1:2, :])
        k = k_ref[:, qk_l[h]]
        if latent:
            k = _rope(k, cos_ref[...], sin_ref[...], R_DK)
        ks.append(k)
        k16.append(k.astype(BF16))
        v16.append(v_ref[:, v_l[h]].astype(BF16))
    for qt in range(L // Q_TILE):
        rows = slice(qt * Q_TILE, (qt + 1) * Q_TILE)
        i = lax.broadcasted_iota(jnp.int32, (Q_TILE, L), 0) + qt * Q_TILE
        j = lax.broadcasted_iota(jnp.int32, (Q_TILE, L), 1)
        dist = (i - j).astype(F32)
        pos = (lax.broadcasted_iota(jnp.int32, (Q_TILE, 1), 0) + qt * Q_TILE).astype(F32)
        q16 = []
        for h in heads:
            q = q_ref[rows, qk_l[h]]
            if latent:
                q = _rope(q, cos_ref[rows, :], sin_ref[rows, :], R_DK)
            q16.append((q * R_DK ** -0.5).astype(BF16))
        a = [_dot_nt(q16[h], k16[h]) for h in heads]
        if latent:
            sf = [_dot(q16[h], s0_ref[0, 0, h].astype(BF16)) for h in heads]
            sb = [_dot(q16[h], s0_ref[0, 1, h].astype(BF16)) for h in heads]
        w16 = [(a[h] * (jnp.exp(jnp.where(dist >= 0, lgf[h] * dist, NEG_INF))
                        + jnp.exp(jnp.where(dist <= 0, -lgb[h] * dist, NEG_INF)))).astype(BF16) for h in heads]
        o = [_dot(w16[h], v16[h]) for h in heads]
        for h in heads:
            oh = o[h]
            if latent:
                oh = oh + sf[h] * jnp.exp(lgf[h] * (pos + 1.0)) + sb[h] * jnp.exp(lgb[h] * (L - pos))
            y = oh * lax.rsqrt(jnp.mean(oh * oh, axis=-1, keepdims=True) + EPS) * nw_ref[...]
            g = rg_ref[rows, v_l[h]]
            o_ref[rows, v_l[h]] = (y * (g * jax.nn.sigmoid(g))).astype(o_ref.dtype)
    if not latent:
        pos = lax.broadcasted_iota(jnp.int32, (L, 1), 0).astype(F32)
        for h in heads:
            so_ref[0, 0, h] = _dot_tn((ks[h] * jnp.exp(lgf[h] * (L - 1.0 - pos))).astype(BF16), v16[h])
            so_ref[0, 1, h] = _dot_tn((ks[h] * jnp.exp(lgb[h] * pos)).astype(BF16), v16[h])


def _retention(proj, decay, nw, seq_len, n_seq, tok_off, li, s0=None, dst=None, dst_state=None):
    latent = s0 is not None
    blk0 = tok_off // seq_len
    wqk, wv = R_HEADS * R_DK, R_HEADS * R_DV
    in_specs = [pl.BlockSpec((seq_len, wqk), lambda s: (blk0 + s, C_RQ // wqk)),
                pl.BlockSpec((seq_len, wqk), lambda s: (blk0 + s, C_RK // wqk)),
                pl.BlockSpec((seq_len, wv), lambda s: (blk0 + s, C_RV // wv)),
                pl.BlockSpec((seq_len, wv), lambda s: (blk0 + s, C_RG // wv)),
                pl.BlockSpec((R_HEADS, 2, 1), lambda s: (0, 0, 0)),
                pl.BlockSpec((1, R_DV), lambda s: (0, 0))]
    args = [proj, proj, proj, proj, decay.T.reshape(R_HEADS, 2, 1), nw.reshape(1, R_DV)]
    out_shape = [jax.ShapeDtypeStruct((N_TOK, BRANCH_W), BF16)]
    out_specs = [pl.BlockSpec((seq_len, wv), lambda s: (blk0 + s, 0))]
    if latent:
        cos, sin = _rope_tables(seq_len, R_DK)
        in_specs += [pl.BlockSpec((seq_len, 128), lambda s: (0, 0)),
                     pl.BlockSpec((seq_len, 128), lambda s: (0, 0)),
                     pl.BlockSpec((1, 2, R_HEADS, R_DK, R_DV), lambda s: (s, 0, 0, 0, 0))]
        args += [cos, sin, s0]
    else:
        out_shape.append(jax.ShapeDtypeStruct((n_seq, DEPTH, 2, R_HEADS, R_DK, R_DV), F32))
        out_specs.append(pl.BlockSpec((1, None, 2, R_HEADS, R_DK, R_DV), lambda s: (s, li, 0, 0, 0, 0)))
    res = _branch_call(
        functools.partial(_retention_kernel, seq_len=seq_len, latent=latent),
        [dst, dst_state][:len(out_shape)], in_specs, args, out_shape, out_specs,
        grid=(n_seq,),
        compiler_params=_cparams(("arbitrary",)),
        name="retention",
    )
    return (res[0], None) if latent else res


DA_HB = 4


def _diff_attn_kernel(*refs, seq_len, latent, lam_init):
    it = iter(refs)
    q_ref, k_ref, v_ref, lp_ref, nw_ref = [next(it) for _ in range(5)]
    if latent:
        cos_ref, sin_ref, ck_ref, cv_ref = [next(it) for _ in range(4)]
    o_ref = next(it)
    L = seq_len
    lp = lp_ref[...]
    lam = (jnp.exp(jnp.sum(lp[0:1] * lp[1:2], axis=-1, keepdims=True))
           - jnp.exp(jnp.sum(lp[2:3] * lp[3:4], axis=-1, keepdims=True)) + lam_init)
    w = 2 * D_HD
    lanes = [slice(hh * w, (hh + 1) * w) for hh in range(DA_HB)]
    keys, vals = [], []
    for hh in range(DA_HB):
        k = k_ref[:, lanes[hh]]
        if latent:
            k = _rope(k, cos_ref[...], sin_ref[...], D_HD)
        keys.append([k.astype(BF16)] + ([ck_ref[0, 0, :, lanes[hh]].astype(BF16)] if latent else []))
        vals.append([v_ref[:, lanes[hh]].astype(BF16)] + ([cv_ref[0, 0, :, lanes[hh]].astype(BF16)] if latent else []))
    lane = lax.broadcasted_iota(jnp.int32, (Q_TILE, w), 1)
    for qt in range(L // Q_TILE):
        rows = slice(qt * Q_TILE, (qt + 1) * Q_TILE)
        hp = [(hh, p) for hh in range(DA_HB) for p in range(2)]
        qs = []
        for hh in range(DA_HB):
            q = q_ref[rows, lanes[hh]]
            if latent:
                q = _rope(q, cos_ref[rows, :], sin_ref[rows, :], D_HD)
            qs.append(q * D_HD ** -0.5)
        s = [[_dot_nt(jnp.where((lane >= D_HD) == (p == 1), qs[hh], 0.0).astype(BF16), kk) for kk in keys[hh]]
             for hh, p in hp]
        outs = []
        for n, (hh, p) in enumerate(hp):
            m = s[n][0].max(axis=-1, keepdims=True)
            for x in s[n][1:]:
                m = jnp.maximum(m, x.max(axis=-1, keepdims=True))
            e = [jnp.exp(x - m) for x in s[n]]
            z = sum(x.sum(axis=-1, keepdims=True) for x in e)
            pv = sum(_dot(x.astype(BF16), vv) for x, vv in zip(e, vals[hh]))
            outs.append(pv / z)
        for hh in range(DA_HB):
            o = outs[2 * hh] - lam * outs[2 * hh + 1]
            y = o * lax.rsqrt(jnp.mean(o * o, axis=-1, keepdims=True) + EPS) * nw_ref[...]
            o_ref[rows, lanes[hh]] = (y * (1.0 - lam_init)).astype(o_ref.dtype)


def _diff_attn(proj, lam_p, nw, li, seq_len, n_seq, tok_off, cache_k=None, cache_v=None, dst=None):
    latent = cache_k is not None
    blk0 = tok_off // seq_len
    w = 2 * D_HD
    wl = DA_HB * w
    in_specs = [pl.BlockSpec((seq_len, wl), lambda s, h: (blk0 + s, C_DQ // wl + h)),
                pl.BlockSpec((seq_len, wl), lambda s, h: (blk0 + s, C_DK // wl + h)),
                pl.BlockSpec((seq_len, wl), lambda s, h: (blk0 + s, C_DV // wl + h)),
                pl.BlockSpec((4, D_HD), lambda s, h: (0, 0)),
                pl.BlockSpec((1, w), lambda s, h: (0, 0))]
    args = [proj, proj, proj, lam_p, nw.reshape(1, w)]
    if latent:
        cos, sin = _rope_tables(seq_len, D_HD)
        in_specs += [pl.BlockSpec((seq_len, 128), lambda s, h: (0, 0)),
                     pl.BlockSpec((seq_len, 128), lambda s, h: (0, 0)),
                     pl.BlockSpec((1, 1, PAST_LEN, wl), lambda s, h: (s, li, 0, h)),
                     pl.BlockSpec((1, 1, PAST_LEN, wl), lambda s, h: (s, li, 0, h))]
        args += [cos, sin, cache_k.reshape(DEC_BATCH, DEPTH, PAST_LEN, D_HEADS * w),
                 cache_v.reshape(DEC_BATCH, DEPTH, PAST_LEN, D_HEADS * w)]
    lam_init = 0.8 - 0.6 * math.exp(-0.3 * li)
    return _branch_call(
        functools.partial(_diff_attn_kernel, seq_len=seq_len, latent=latent, lam_init=lam_init),
        [dst], in_specs, args,
        [jax.ShapeDtypeStruct((N_TOK, BRANCH_W), BF16)],
        [pl.BlockSpec((seq_len, wl), lambda s, h: (blk0 + s, h))],
        grid=(n_seq, D_HEADS // DA_HB),
        compiler_params=_cparams(("arbitrary", "arbitrary")),
        name="diff_attn",
    )[0]


def _per_block(rows):
    idx = np.concatenate([np.zeros(N_CTX // MOD_BLK, np.int32)] +
                         [np.full(DEC_SEQ // MOD_BLK, 1 + b, np.int32) for b in range(DEC_BATCH)])
    return rows[idx][:, None, :]


def kernel(x_prompt, x_sample, cache_diff_k, cache_diff_v, state_ret, state_gdn, c, c_ctx,
           norm1_w, norm2_w, w_mod, b_mod, w_in, ret_decay, ret_norm, diff_lambda, diff_norm,
           gdn_conv, gdn_A_log, gdn_dt_bias, gdn_norm, w_branch, w_out,
           peer_wq, peer_keys, peer_u, peer_v, norm_f_w):
    x = jnp.concatenate([x_prompt.reshape(N_CTX, D_MODEL), x_sample.reshape(N_LAT, D_MODEL)], axis=0)
    cond = jnp.concatenate([c_ctx[None, :], c, jnp.zeros((8 - N_COND, D_MODEL), F32)], axis=0)
    mod_all = _modulation(cond, w_mod, b_mod)

    w_main = jnp.concatenate([w_in[:, :, :AB_OFF], w_in[:, :, AB_OFF + 32:]], axis=-1).astype(BF16)
    gate_cols = _gdn_gate_cols()
    w_ab = w_in[:, :, AB_OFF:AB_OFF + 32][:, :, gate_cols]
    w_ab = jnp.pad(w_ab, ((0, 0), (0, 0), (0, 0), (0, 128 - gate_cols.shape[1])))
    w_ab = w_ab.reshape(DEPTH, D_MODEL, gate_cols.shape[0] * 128).astype(BF16)
    wb_bf = w_branch.astype(BF16)
    wo_bf = w_out.astype(BF16)
    wq_bf = peer_wq.astype(BF16)
    u_bf = peer_u.astype(BF16)
    v_bf = peer_v.astype(BF16)

    new_k = new_v = new_ret = new_gdn = None
    for li in range(DEPTH):
        mod = mod_all[li, :N_COND]
        sh1, sc1, g1, sh2, sc2, g2 = [_per_block(m) for m in jnp.split(mod, 6, axis=-1)]
        h = _normmod(x, norm1_w[li], sc1, sh1, BF16)
        proj = _matmul(h, w_main, li, 2048, 1024)
        gab = _matmul(h, w_ab, li, 512, w_ab.shape[2])

        o_ret, new_ret = _retention(proj, ret_decay[li], ret_norm[li], SEQ, BATCH, 0, li, dst_state=new_ret)
        o_ret, _ = _retention(proj, ret_decay[li], ret_norm[li], DEC_SEQ, DEC_BATCH, N_CTX, li,
                              s0=state_ret[:, li], dst=o_ret)
        o_diff = _diff_attn(proj, diff_lambda[li], diff_norm[li], li, SEQ, BATCH, 0)
        o_diff = _diff_attn(proj, diff_lambda[li], diff_norm[li], li, DEC_SEQ, DEC_BATCH, N_CTX,
                            cache_diff_k, cache_diff_v, dst=o_diff)
        new_k, new_v = _kv_out(proj, li, new_k, new_v)

        alog = gdn_A_log[li].T.reshape(2 * G_HEADS, 1, 1)
        dtb = gdn_dt_bias[li].T.reshape(2 * G_HEADS, 1, 1)
        qkv_c = _gdn_prep(proj, gdn_conv[li], SEQ, BATCH, 0)
        o_gdn, new_gdn = _gdn(qkv_c, proj, gab, alog, dtb, gdn_norm[li], None, SEQ, BATCH, 0, li, dst_state=new_gdn)
        qkv_z = _gdn_prep(proj, gdn_conv[li], DEC_SEQ, DEC_BATCH, N_CTX)
        o_gdn, _ = _gdn(qkv_z, proj, gab, alog, dtb, gdn_norm[li], state_gdn[:, li], DEC_SEQ, DEC_BATCH, N_CTX, li,
                        dst=o_gdn)

        x, h2 = _merge(o_ret, o_diff, o_gdn, proj, wb_bf, wo_bf, li, x, g1, norm2_w[li], sc2, sh2)
        idx, gate = _peer_route(h2, wq_bf, li, peer_keys[li])
        x = _peer_experts(h2, idx, gate, u_bf, v_bf, li, x, g2)

    zeros_blk = jnp.zeros((N_MOD_BLK, 1, D_MODEL), F32)
    y = _normmod(x, norm_f_w, zeros_blk, zeros_blk, F32)
    y_prompt = y[:N_CTX].reshape(BATCH, SEQ, D_MODEL)
    y_sample = y[N_CTX:].reshape(DEC_BATCH, DEC_SEQ, D_MODEL)
    return (y_prompt, y_sample, new_k.reshape(BATCH, DEPTH, SEQ, D_HEADS, 2, D_HD),
            new_v.reshape(BATCH, DEPTH, SEQ, D_HEADS, 2 * D_HD), new_ret, new_gdn)
```
